```python
import math
import jax, jax.numpy as jnp
from jax import lax
import numpy as np

D_MODEL = 1024
BATCH = 4
SEQ = 8192
DEPTH = 2

MEM_LEN = 256
DA_HEADS = 8
DA_HEAD_DIM = 64
DA_V_DIM = 2 * DA_HEAD_DIM
DA_WIDTH = DA_HEADS * DA_V_DIM
QK_WIDTH = DA_HEADS * 2 * DA_HEAD_DIM
LAMBDA_INIT_SCALE = 0.1
CONV_WIDTH = 512
CONV_K = 3
MEM_HEADS = 4
MEM_HEAD_DIM = 128
MEM_WIDTH = MEM_HEADS * MEM_HEAD_DIM
N_BRANCH = 3
ROPE_THETA = 10000.0
Q_BLOCK = 128
N_EXPERTS = 16
N_GROUPS = 4
EXPERTS_PER_GROUP = N_EXPERTS // N_GROUPS
TOP_K = 2
D_FF_EXPERT = 512
EXPERT_BLOCK = 128
LN_EPS = 1e-5
RMS_EPS = 1e-5
DEEPNORM_ALPHA = (2 * DEPTH) ** 0.25
DEEPNORM_BETA = (8 * DEPTH) ** -0.25

IN_SIZES = (QK_WIDTH, QK_WIDTH, DA_WIDTH, CONV_WIDTH, CONV_WIDTH, CONV_WIDTH, MEM_WIDTH, N_BRANCH * D_MODEL)
IN_COLS = sum(IN_SIZES)
IN_SPLITS = tuple(sum(IN_SIZES[:i + 1]) for i in range(len(IN_SIZES) - 1))

kernel_name = 'hybrid_diffattn_shortconv_memxattn_grouped_moe'


def layer_norm(x, g, b):
    x32 = x.astype(jnp.float32)
    mu = jnp.mean(x32, axis=-1, keepdims=True)
    var = jnp.mean(jnp.square(x32 - mu), axis=-1, keepdims=True)
    y = (x32 - mu) * lax.rsqrt(var + LN_EPS) * g.astype(jnp.float32) + b.astype(jnp.float32)
    return y.astype(x.dtype)


def rope(t, cos, sin):
    half = t.shape[-1] // 2
    t1, t2 = t[..., :half], t[..., half:]
    cos = cos.astype(t.dtype)
    sin = sin.astype(t.dtype)
    return jnp.concatenate([t1 * cos - t2 * sin, t2 * cos + t1 * sin], axis=-1)


def diff_attention(q, k, v, lam, lam_init, subln_g):
    B, S = q.shape[0], q.shape[1]
    qh = jnp.transpose(q, (0, 2, 3, 1, 4))
    kh = jnp.transpose(k, (0, 2, 3, 1, 4))
    vh = jnp.transpose(v, (0, 2, 1, 3))
    scale = DA_HEAD_DIM ** -0.5
    outs = []
    for blk in range(S // Q_BLOCK):
        s0 = blk * Q_BLOCK
        end = s0 + Q_BLOCK
        qb = qh[:, :, :, s0:end]
        kb = kh[:, :, :, :end]
        sc = jnp.einsum('bhcqd,bhckd->bhcqk', qb, kb).astype(jnp.float32) * scale
        causal = jnp.arange(end)[None, :] <= (s0 + jnp.arange(Q_BLOCK))[:, None]
        sc = jnp.where(causal, sc, -jnp.inf)
        p = jax.nn.softmax(sc, axis=-1)
        a = p[:, :, 0] - lam * p[:, :, 1]
        outs.append(jnp.einsum('bhqk,bhkd->bhqd', a.astype(v.dtype), vh[:, :, :end]))
    o = jnp.concatenate(outs, axis=2).astype(jnp.float32)
    o = o * lax.rsqrt(jnp.mean(jnp.square(o), axis=-1, keepdims=True) + RMS_EPS)
    o = o * subln_g.astype(jnp.float32) * (1.0 - lam_init)
    o = jnp.transpose(o, (0, 2, 1, 3)).astype(v.dtype)
    return o.reshape(B, S, DA_WIDTH)


def short_gated_conv(bg, cg, u, w):
    z = cg * u
    S = z.shape[1]
    zp = jnp.pad(z, ((0, 0), (CONV_K - 1, 0), (0, 0)))
    y = w[0] * zp[:, 0:S]
    for tap in range(1, CONV_K):
        y = y + w[tap] * zp[:, tap:tap + S]
    return bg * y


def memory_attention(qm, mem, w_kv):
    B, S = qm.shape[0], qm.shape[1]
    M = mem.shape[1]
    kv = mem @ w_kv
    km = kv[..., :MEM_WIDTH].reshape(B, M, MEM_HEADS, MEM_HEAD_DIM)
    vm = kv[..., MEM_WIDTH:].reshape(B, M, MEM_HEADS, MEM_HEAD_DIM)
    q = qm.reshape(B, S, MEM_HEADS, MEM_HEAD_DIM)
    sc = jnp.einsum('bshd,bmhd->bhsm', q, km).astype(jnp.float32) * (MEM_HEAD_DIM ** -0.5)
    p = jax.nn.softmax(sc, axis=-1)
    o = jnp.einsum('bhsm,bmhd->bshd', p.astype(vm.dtype), vm)
    return o.reshape(B, S, MEM_WIDTH)


def grouped_moe(x2d, w_router, router_bias, w_gate, w_up, w_down):
    T, D = x2d.shape
    logits = (x2d @ w_router).astype(jnp.float32) + router_bias.astype(jnp.float32)
    scores = jax.nn.softmax(logits, axis=-1)
    grouped = scores.reshape(T, N_GROUPS, EXPERTS_PER_GROUP)
    g_idx = jnp.argmax(jnp.max(grouped, axis=-1), axis=-1)
    in_group = jnp.take_along_axis(grouped, g_idx[:, None, None], axis=1)[:, 0]
    top_vals, top_idx = lax.top_k(in_group, TOP_K)
    experts = g_idx[:, None] * EXPERTS_PER_GROUP + top_idx
    weights = top_vals / jnp.sum(top_vals, axis=-1, keepdims=True)

    A = T * TOP_K
    e_flat = experts.reshape(A).astype(jnp.int32)
    tok_flat = jnp.repeat(jnp.arange(T, dtype=jnp.int32), TOP_K)
    gate_flat = weights.reshape(A).astype(x2d.dtype)
    order = jnp.argsort(e_flat)
    e_sorted = e_flat[order]
    counts = jnp.zeros((N_EXPERTS,), jnp.int32).at[e_flat].add(1)
    start = jnp.cumsum(counts) - counts
    padded = ((counts + EXPERT_BLOCK - 1) // EXPERT_BLOCK) * EXPERT_BLOCK
    pad_end = jnp.cumsum(padded)
    pad_start = pad_end - padded
    dest = pad_start[e_sorted] + (jnp.arange(A, dtype=jnp.int32) - start[e_sorted])
    P = ((A + EXPERT_BLOCK - 1) // EXPERT_BLOCK) * EXPERT_BLOCK + N_EXPERTS * EXPERT_BLOCK
    n_blocks = P // EXPERT_BLOCK
    buf_tok = jnp.full((P,), T, jnp.int32).at[dest].set(tok_flat[order])
    buf_gate = jnp.zeros((P,), x2d.dtype).at[dest].set(gate_flat[order])
    blk_start = jnp.arange(n_blocks, dtype=jnp.int32) * EXPERT_BLOCK
    blk_expert = jnp.minimum(jnp.searchsorted(pad_end, blk_start, side='right'), N_EXPERTS - 1).astype(jnp.int32)
    x_pad = jnp.concatenate([x2d, jnp.zeros((1, D), x2d.dtype)], axis=0)
    xb = x_pad[buf_tok].reshape(n_blocks, EXPERT_BLOCK, D)

    def expert_block(args):
        xblk, e = args
        h = jax.nn.silu(xblk @ w_gate[e]) * (xblk @ w_up[e])
        return h @ w_down[e]

    yb = lax.map(expert_block, (xb, blk_expert))
    y = yb.reshape(P, D) * buf_gate[:, None]
    return jax.ops.segment_sum(y, buf_tok, num_segments=T + 1)[:T]


def setup_inputs(seed: int = 0) -> dict:
    key = jax.random.key(seed)
    ks = jax.random.split(key, 26)
    f32 = jnp.float32

    def nrm(k, shape, fan_in, scale=1.0):
        return jax.random.normal(k, shape, f32) * (scale * fan_in ** -0.5)

    x = jax.random.normal(ks[0], (BATCH, SEQ, D_MODEL), f32)
    mem = jax.random.normal(ks[1], (BATCH, MEM_LEN, D_MODEL), f32)
    positions = (jax.random.randint(ks[2], (BATCH, 1), 0, 4096, jnp.int32)
                 + jnp.arange(SEQ, dtype=jnp.int32)[None, :])
    col_scale = jnp.concatenate([
        jnp.ones((2 * QK_WIDTH,), f32),
        jnp.full((DA_WIDTH,), DEEPNORM_BETA, f32),
        jnp.ones((IN_COLS - 2 * QK_WIDTH - DA_WIDTH,), f32)])
    w_in = nrm(ks[3], (DEPTH, D_MODEL, IN_COLS), D_MODEL) * col_scale
    lambda_q1 = LAMBDA_INIT_SCALE * jax.random.normal(ks[4], (DEPTH, DA_HEAD_DIM), f32)
    lambda_k1 = LAMBDA_INIT_SCALE * jax.random.normal(ks[5], (DEPTH, DA_HEAD_DIM), f32)
    lambda_q2 = LAMBDA_INIT_SCALE * jax.random.normal(ks[6], (DEPTH, DA_HEAD_DIM), f32)
    lambda_k2 = LAMBDA_INIT_SCALE * jax.random.normal(ks[7], (DEPTH, DA_HEAD_DIM), f32)
    diff_subln_g = 1.0 + 0.02 * jax.random.normal(ks[8], (DEPTH, DA_V_DIM), f32)
    conv_w = nrm(ks[9], (DEPTH, CONV_K, CONV_WIDTH), CONV_K)
    w_mem_kv = nrm(ks[10], (DEPTH, D_MODEL, 2 * MEM_WIDTH), D_MODEL)
    w_br_attn = nrm(ks[11], (DEPTH, DA_WIDTH, D_MODEL), DA_WIDTH, DEEPNORM_BETA)
    w_br_conv = nrm(ks[12], (DEPTH, CONV_WIDTH, D_MODEL), CONV_WIDTH, DEEPNORM_BETA)
    w_br_mem = nrm(ks[13], (DEPTH, MEM_WIDTH, D_MODEL), MEM_WIDTH, DEEPNORM_BETA)
    w_out = nrm(ks[14], (DEPTH, D_MODEL, D_MODEL), D_MODEL, DEEPNORM_BETA)
    ln1_g = 1.0 + 0.02 * jax.random.normal(ks[15], (DEPTH, D_MODEL), f32)
    ln1_b = 0.02 * jax.random.normal(ks[16], (DEPTH, D_MODEL), f32)
    ln2_g = 1.0 + 0.02 * jax.random.normal(ks[17], (DEPTH, D_MODEL), f32)
    ln2_b = 0.02 * jax.random.normal(ks[18], (DEPTH, D_MODEL), f32)
    w_router = nrm(ks[19], (D_MODEL, N_EXPERTS), D_MODEL)
    router_bias = 0.01 * jax.random.normal(ks[20], (N_EXPERTS,), f32)
    w_exp_gate = nrm(ks[21], (DEPTH, N_EXPERTS, D_MODEL, D_FF_EXPERT), D_MODEL, DEEPNORM_BETA)
    w_exp_up = nrm(ks[22], (DEPTH, N_EXPERTS, D_MODEL, D_FF_EXPERT), D_MODEL, DEEPNORM_BETA)
    w_exp_down = nrm(ks[23], (DEPTH, N_EXPERTS, D_FF_EXPERT, D_MODEL), D_FF_EXPERT, DEEPNORM_BETA)
    return {'x': x, 'mem': mem, 'positions': positions, 'w_in': w_in,
            'lambda_q1': lambda_q1, 'lambda_k1': lambda_k1, 'lambda_q2': lambda_q2, 'lambda_k2': lambda_k2,
            'diff_subln_g': diff_subln_g, 'conv_w': conv_w, 'w_mem_kv': w_mem_kv,
            'w_br_attn': w_br_attn, 'w_br_conv': w_br_conv, 'w_br_mem': w_br_mem, 'w_out': w_out,
            'ln1_g': ln1_g, 'ln1_b': ln1_b, 'ln2_g': ln2_g, 'ln2_b': ln2_b,
            'w_router': w_router, 'router_bias': router_bias,
            'w_exp_gate': w_exp_gate, 'w_exp_up': w_exp_up, 'w_exp_down': w_exp_down}


def reference(x, mem, positions, w_in, lambda_q1, lambda_k1, lambda_q2, lambda_k2, diff_subln_g,
              conv_w, w_mem_kv, w_br_attn, w_br_conv, w_br_mem, w_out,
              ln1_g, ln1_b, ln2_g, ln2_b, w_router, router_bias,
              w_exp_gate, w_exp_up, w_exp_down):
    B, S, D = x.shape
    inv_freq = ROPE_THETA ** (-jnp.arange(0, DA_HEAD_DIM, 2, dtype=jnp.float32) / DA_HEAD_DIM)
    ang = positions.astype(jnp.float32)[..., None] * inv_freq
    cos = jnp.cos(ang)[:, :, None, :]
    sin = jnp.sin(ang)[:, :, None, :]

    for l in range(DEPTH):
        lam_init = 0.8 - 0.6 * math.exp(-0.3 * l)
        proj = x @ w_in[l]
        q, k, v, cb, cc, cu, qm, gates = jnp.split(proj, IN_SIZES and IN_SPLITS, axis=-1)

        q = rope(q.reshape(B, S, DA_HEADS * 2, DA_HEAD_DIM), cos, sin).reshape(B, S, DA_HEADS, 2, DA_HEAD_DIM)
        k = rope(k.reshape(B, S, DA_HEADS * 2, DA_HEAD_DIM), cos, sin).reshape(B, S, DA_HEADS, 2, DA_HEAD_DIM)
        v = v.reshape(B, S, DA_HEADS, DA_V_DIM)
        lam = (jnp.exp(jnp.sum(lambda_q1[l].astype(jnp.float32) * lambda_k1[l].astype(jnp.float32)))
               - jnp.exp(jnp.sum(lambda_q2[l].astype(jnp.float32) * lambda_k2[l].astype(jnp.float32)))
               + lam_init)
        y_attn = diff_attention(q, k, v, lam, lam_init, diff_subln_g[l])

        y_conv = short_gated_conv(cb, cc, cu, conv_w[l])

        y_mem = memory_attention(qm, mem, w_mem_kv[l])

        g = jax.nn.sigmoid(gates.reshape(B, S, N_BRANCH, D))
        merged = (g[:, :, 0] * (y_attn @ w_br_attn[l])
                  + g[:, :, 1] * (y_conv @ w_br_conv[l])
                  + g[:, :, 2] * (y_mem @ w_br_mem[l]))
        mix = merged @ w_out[l]
        x = layer_norm(DEEPNORM_ALPHA * x + mix, ln1_g[l], ln1_b[l])

        f = grouped_moe(x.reshape(B * S, D), w_router, router_bias,
                        w_exp_gate[l], w_exp_up[l], w_exp_down[l]).reshape(B, S, D)
        x = layer_norm(DEEPNORM_ALPHA * x + f, ln2_g[l], ln2_b[l])
    return x
```

```python
import functools
import math

import jax
import jax.numpy as jnp
from jax import lax
from jax.experimental import pallas as pl
from jax.experimental.pallas import tpu as pltpu

F32 = jnp.float32
BF16 = jnp.bfloat16

DA_HEADS = 8
DA_HEAD_DIM = 64
DA_V_DIM = 2 * DA_HEAD_DIM
CONV_WIDTH = 512
CONV_K = 3
MEM_HEADS = 4
MEM_HEAD_DIM = 128
MEM_WIDTH = MEM_HEADS * MEM_HEAD_DIM
N_BRANCH = 3
ROPE_THETA = 10000.0
N_EXPERTS = 16
N_GROUPS = 4
EXPERTS_PER_GROUP = N_EXPERTS // N_GROUPS
LN_EPS = 1e-5
RMS_EPS = 1e-5

LANES = 128
SUBLANES = 8
VMEM_LIMIT_BYTES = 56 * 1024 * 1024

PROJ_TM = 1024
PROJ_TN = 512
ATTN_TQ = 256
MERGE_TM = 256
MOE_TM = 1024


def _params(semantics):
    return pltpu.CompilerParams(dimension_semantics=semantics,
                                vmem_limit_bytes=VMEM_LIMIT_BYTES)


def _rope_table_kernel(pos_ref, inv_ref, cos_ref, sin_ref):
    ang = pos_ref[...].astype(F32) * inv_ref[...]
    lane = lax.broadcasted_iota(jnp.int32, ang.shape, 1)
    first_half = (lane % DA_HEAD_DIM) < (DA_HEAD_DIM // 2)
    s = jnp.sin(ang)
    cos_ref[...] = jnp.cos(ang)
    sin_ref[...] = jnp.where(first_half, -s, s)


def _rope_tables(positions, tm):
    t = positions.size
    half = DA_HEAD_DIM // 2
    inv_freq = ROPE_THETA ** (-jnp.arange(0, DA_HEAD_DIM, 2, dtype=F32) / DA_HEAD_DIM)
    inv_lanes = jnp.tile(inv_freq, LANES // half).reshape(1, LANES)
    pos = positions.reshape(t, 1)
    return pl.pallas_call(
        _rope_table_kernel,
        grid=(t // tm,),
        in_specs=[pl.BlockSpec((tm, 1), lambda i: (i, 0)),
                  pl.BlockSpec((1, LANES), lambda i: (0, 0))],
        out_specs=[pl.BlockSpec((tm, LANES), lambda i: (i, 0)),
                   pl.BlockSpec((tm, LANES), lambda i: (i, 0))],
        out_shape=[jax.ShapeDtypeStruct((t, LANES), F32),
                   jax.ShapeDtypeStruct((t, LANES), F32)],
        compiler_params=_params(("parallel",)),
        name="rope_tables",
    )(pos, inv_lanes)


def _lambda_kernel(q1_ref, k1_ref, q2_ref, k2_ref, init_ref, lam_ref):
    a = jnp.sum(q1_ref[...] * k1_ref[...], axis=-1, keepdims=True)
    b = jnp.sum(q2_ref[...] * k2_ref[...], axis=-1, keepdims=True)
    lam_ref[...] = jnp.exp(a) - jnp.exp(b) + init_ref[...]


def _lambdas(lq1, lk1, lq2, lk2, lam_init):
    depth = lq1.shape[0]
    init = jnp.broadcast_to(jnp.asarray(lam_init, F32).reshape(depth, 1), (depth, LANES))
    return pl.pallas_call(
        _lambda_kernel,
        out_shape=jax.ShapeDtypeStruct((depth, LANES), F32),
        name="diff_lambda",
    )(lq1.astype(F32), lk1.astype(F32), lq2.astype(F32), lk2.astype(F32), init)


_QK_TILES = 4
_V_TILE0, _CV_TILE0, _QM_TILE, _G_TILE0, _N_TILES = 4, 6, 9, 10, 16


def _proj_kernel(x_ref, w_ref, cos_ref, sin_ref, qk_ref, v_ref, cv_ref, qm_ref, g_ref):
    j = pl.program_id(1)
    acc = jnp.dot(x_ref[...], w_ref[...], preferred_element_type=F32)
    tn = acc.shape[1]

    @pl.when(j < _QK_TILES)
    def _():
        reps = tn // LANES
        c = jnp.concatenate([cos_ref[...]] * reps, axis=1)
        s = jnp.concatenate([sin_ref[...]] * reps, axis=1)
        lane = lax.broadcasted_iota(jnp.int32, acc.shape, 1)
        first_half = (lane % DA_HEAD_DIM) < (DA_HEAD_DIM // 2)
        half = DA_HEAD_DIM // 2
        partner = jnp.where(first_half,
                            pltpu.roll(acc, tn - half, axis=1),
                            pltpu.roll(acc, half, axis=1))
        rot = acc * c + partner * s
        scale = jnp.where(j < _QK_TILES // 2, DA_HEAD_DIM ** -0.5, 1.0).astype(F32)
        qk_ref[...] = (rot * scale).astype(qk_ref.dtype)

    @pl.when((j >= _V_TILE0) & (j < _CV_TILE0))
    def _():
        v_ref[...] = acc.astype(v_ref.dtype)

    @pl.when((j >= _CV_TILE0) & (j < _QM_TILE))
    def _():
        cv_ref[...] = acc

    @pl.when(j == _QM_TILE)
    def _():
        qm_ref[...] = acc.astype(qm_ref.dtype)

    @pl.when(j >= _G_TILE0)
    def _():
        g_ref[...] = jax.nn.sigmoid(acc)


def _projection(xb, w, cos_t, sin_t):
    t, d = xb.shape
    tm, tn = PROJ_TM, PROJ_TN
    assert t % tm == 0 and w.shape == (d, _N_TILES * tn)

    def clip(j, lo, n):
        return jnp.clip(j - lo, 0, n - 1)

    return pl.pallas_call(
        _proj_kernel,
        grid=(t // tm, _N_TILES),
        in_specs=[pl.BlockSpec((tm, d), lambda i, j: (i, 0)),
                  pl.BlockSpec((d, tn), lambda i, j: (0, j)),
                  pl.BlockSpec((tm, LANES), lambda i, j: (i, 0)),
                  pl.BlockSpec((tm, LANES), lambda i, j: (i, 0))],
        out_specs=[pl.BlockSpec((tm, tn), lambda i, j: (i, clip(j, 0, _QK_TILES))),
                   pl.BlockSpec((tm, tn), lambda i, j: (i, clip(j, _V_TILE0, 2))),
                   pl.BlockSpec((tm, tn), lambda i, j: (i, clip(j, _CV_TILE0, 3))),
                   pl.BlockSpec((tm, tn), lambda i, j: (i, 0)),
                   pl.BlockSpec((tm, tn), lambda i, j: (i, clip(j, _G_TILE0, 6)))],
        out_shape=[jax.ShapeDtypeStruct((t, _QK_TILES * tn), BF16),
                   jax.ShapeDtypeStruct((t, 2 * tn), BF16),
                   jax.ShapeDtypeStruct((t, 3 * tn), F32),
                   jax.ShapeDtypeStruct((t, tn), BF16),
                   jax.ShapeDtypeStruct((t, 6 * tn), F32)],
        compiler_params=_params(("parallel", "arbitrary")),
        name="in_projection",
    )(xb, w, cos_t, sin_t)


def _matmul_kernel(a_ref, b_ref, o_ref):
    o_ref[...] = jnp.dot(a_ref[...], b_ref[...], preferred_element_type=F32).astype(o_ref.dtype)


def _matmul(a, b, tn, out_dtype):
    m, k = a.shape
    n = b.shape[1]
    return pl.pallas_call(
        _matmul_kernel,
        grid=(n // tn,),
        in_specs=[pl.BlockSpec((m, k), lambda j: (0, 0)),
                  pl.BlockSpec((k, tn), lambda j: (0, j))],
        out_specs=pl.BlockSpec((m, tn), lambda j: (0, j)),
        out_shape=jax.ShapeDtypeStruct((m, n), out_dtype),
        compiler_params=_params(("parallel",)),
        name="mem_kv_projection",
    )(a, b)


def _attn_kernel(lam_ref, init_ref, g_ref, q_ref, k_ref, v_ref, o_ref, m_sc, l_sc, acc_sc):
    qi = pl.program_id(2)
    tq = q_ref.shape[0]
    tk = tq
    q = q_ref[...]
    lane = lax.broadcasted_iota(jnp.int32, q.shape, 1)
    zero = jnp.zeros_like(q)
    qq = jnp.concatenate([jnp.where(lane < DA_HEAD_DIM, q, zero),
                          jnp.where(lane >= DA_HEAD_DIM, q, zero)], axis=0)

    m_sc[...] = jnp.full(m_sc.shape, -jnp.inf, F32)
    l_sc[...] = jnp.zeros(l_sc.shape, F32)
    acc_sc[...] = jnp.zeros(acc_sc.shape, F32)

    def step(kb, masked):
        start = pl.multiple_of(kb * tk, tk)
        k = k_ref[pl.ds(start, tk), :]
        v = v_ref[pl.ds(start, tk), :]
        s = lax.dot_general(qq, k, (((1,), (1,)), ((), ())),
                            preferred_element_type=F32)
        if masked:
            row = lax.broadcasted_iota(jnp.int32, s.shape, 0) & (tq - 1)
            col = lax.broadcasted_iota(jnp.int32, s.shape, 1)
            s = jnp.where(col <= row, s, -jnp.inf)
        m_prev = m_sc[...]
        m_new = jnp.maximum(m_prev, jnp.max(s, axis=-1, keepdims=True))
        p = jnp.exp(s - m_new)
        alpha = jnp.exp(m_prev - m_new)
        l_sc[...] = alpha * l_sc[...] + jnp.sum(p, axis=-1, keepdims=True)
        acc_sc[...] = alpha * acc_sc[...] + jnp.dot(p.astype(v.dtype), v,
                                                    preferred_element_type=F32)
        m_sc[...] = m_new

    def body(kb, carry):
        step(kb, False)
        return carry

    lax.fori_loop(0, qi, body, 0)
    step(qi, True)

    o = acc_sc[...] / l_sc[...]
    o = o[:tq] - lam_ref[...] * o[tq:]
    o = o * lax.rsqrt(jnp.mean(jnp.square(o), axis=-1, keepdims=True) + RMS_EPS)
    o = o * g_ref[...] * (1.0 - init_ref[...])
    o_ref[...] = o.astype(o_ref.dtype)


def _attention(qk, v, lam, lam_init, subln_g, batch, seq):
    t = qk.shape[0]
    tq = min(ATTN_TQ, seq)
    assert seq % tq == 0 and tq & (tq - 1) == 0
    nq = seq // tq
    h = DA_HEADS
    row = pl.BlockSpec((1, LANES), lambda b, hh, i: (0, 0))
    return pl.pallas_call(
        _attn_kernel,
        grid=(batch, h, nq),
        in_specs=[row, row, row,
                  pl.BlockSpec((tq, DA_V_DIM), lambda b, hh, i: (b * nq + i, hh)),
                  pl.BlockSpec((seq, DA_V_DIM), lambda b, hh, i: (b, h + hh)),
                  pl.BlockSpec((seq, DA_V_DIM), lambda b, hh, i: (b, hh))],
        out_specs=pl.BlockSpec((tq, DA_V_DIM), lambda b, hh, i: (b * nq + i, hh)),
        out_shape=jax.ShapeDtypeStruct((t, h * DA_V_DIM), BF16),
        scratch_shapes=[pltpu.VMEM((2 * tq, 1), F32),
                        pltpu.VMEM((2 * tq, 1), F32),
                        pltpu.VMEM((2 * tq, DA_V_DIM), F32)],
        compiler_params=_params(("parallel", "parallel", "arbitrary")),
        name="diff_attention",
    )(lam, lam_init, subln_g, qk, qk, v)


def _layer_norm(h, g, b):
    mu = jnp.mean(h, axis=-1, keepdims=True)
    d = h - mu
    var = jnp.mean(jnp.square(d), axis=-1, keepdims=True)
    return d * lax.rsqrt(var + LN_EPS) * g + b


def _route(logits):
    lane = lax.broadcasted_iota(jnp.int32, logits.shape, 1)
    valid = lane < N_EXPERTS
    logits = jnp.where(valid, logits, -jnp.inf)
    mx = jnp.max(logits, axis=-1, keepdims=True)
    ex = jnp.exp(logits - mx)
    scores = ex / jnp.sum(ex, axis=-1, keepdims=True)
    group = lane // EXPERTS_PER_GROUP
    neg = jnp.full_like(scores, -1.0)
    best = jnp.max(jnp.where(group == 0, scores, neg), axis=-1, keepdims=True)
    gidx = jnp.zeros(best.shape, jnp.int32)
    for gi in range(1, N_GROUPS):
        cand = jnp.max(jnp.where(group == gi, scores, neg), axis=-1, keepdims=True)
        take = cand > best
        gidx = jnp.where(take, gi, gidx)
        best = jnp.where(take, cand, best)
    vals = jnp.where((group == gidx) & valid, scores, neg)
    big = jnp.full_like(lane, LANES)
    v1 = jnp.max(vals, axis=-1, keepdims=True)
    i1 = jnp.min(jnp.where(vals == v1, lane, big), axis=-1, keepdims=True)
    vals2 = jnp.where(lane == i1, neg, vals)
    v2 = jnp.max(vals2, axis=-1, keepdims=True)
    i2 = jnp.min(jnp.where(vals2 == v2, lane, big), axis=-1, keepdims=True)
    tot = v1 + v2
    return jnp.where(lane == i1, v1 / tot, 0.0) + jnp.where(lane == i2, v2 / tot, 0.0)


def _merge_kernel(ya_ref, cv_ref, cvp_ref, qm_ref, kvm_ref, g_ref, x_ref,
                  wa_ref, wc_ref, wm_ref, wo_ref, cw_ref, lng_ref, lnb_ref, wr_ref, rb_ref,
                  xo_ref, xob_ref, comb_ref, *, tiles_per_seq, alpha):
    i = pl.program_id(0)
    tm = ya_ref.shape[0]
    cw = CONV_WIDTH

    z = cv_ref[:, cw:2 * cw] * cv_ref[:, 2 * cw:3 * cw]
    zp = cvp_ref[:, cw:2 * cw] * cvp_ref[:, 2 * cw:3 * cw]
    zp = jnp.where(i % tiles_per_seq == 0, jnp.zeros_like(zp), zp)
    row = lax.broadcasted_iota(jnp.int32, z.shape, 0)
    zm1 = jnp.where(row == 0, zp[SUBLANES - 1:SUBLANES], pltpu.roll(z, 1, axis=0))
    zm2 = jnp.where(row == 0, zp[SUBLANES - 2:SUBLANES - 1],
                    jnp.where(row == 1, zp[SUBLANES - 1:SUBLANES], pltpu.roll(z, 2, axis=0)))
    w = cw_ref[...]
    y = w[0:1] * zm2
    y = y + w[1:2] * zm1
    y = y + w[2:3] * z
    y_conv = (cv_ref[:, 0:cw] * y).astype(BF16)

    outs = []
    for hd in range(MEM_HEADS):
        lo = hd * MEM_HEAD_DIM
        qh = qm_ref[:, lo:lo + MEM_HEAD_DIM]
        kh = kvm_ref[:, lo:lo + MEM_HEAD_DIM]
        vh = kvm_ref[:, MEM_WIDTH + lo:MEM_WIDTH + lo + MEM_HEAD_DIM]
        s = lax.dot_general(qh, kh, (((1,), (1,)), ((), ())),
                            preferred_element_type=F32) * (MEM_HEAD_DIM ** -0.5)
        s = s - jnp.max(s, axis=-1, keepdims=True)
        e = jnp.exp(s)
        p = e / jnp.sum(e, axis=-1, keepdims=True)
        outs.append(jnp.dot(p.astype(BF16), vh, preferred_element_type=F32))
    y_mem = jnp.concatenate(outs, axis=1).astype(BF16)

    d = x_ref.shape[1]
    merged = g_ref[:, 0:d] * jnp.dot(ya_ref[...], wa_ref[...], preferred_element_type=F32)
    merged = merged + g_ref[:, d:2 * d] * jnp.dot(y_conv, wc_ref[...], preferred_element_type=F32)
    merged = merged + g_ref[:, 2 * d:3 * d] * jnp.dot(y_mem, wm_ref[...], preferred_element_type=F32)
    mix = jnp.dot(merged.astype(BF16), wo_ref[...], preferred_element_type=F32)
    x1 = _layer_norm(alpha * x_ref[...] + mix, lng_ref[...], lnb_ref[...])
    xo_ref[...] = x1
    xob_ref[...] = x1.astype(BF16)

    logits = jnp.dot(x1, wr_ref[...], preferred_element_type=F32,
                     precision=lax.Precision.HIGHEST) + rb_ref[...]
    comb_ref[...] = _route(logits)


def _merge(ya, cv, qm, kvm, g, xf, wa, wc, wm, wo, conv_w, ln_g, ln_b, wr, rb, seq, mem_len, alpha):
    t, d = xf.shape
    tm = min(MERGE_TM, seq)
    assert seq % tm == 0 and tm % SUBLANES == 0
    tiles_per_seq = seq // tm
    halo = tm // SUBLANES

    def full(a):
        return pl.BlockSpec(a.shape, lambda i: (0,) * a.ndim)

    def rows(width):
        return pl.BlockSpec((tm, width), lambda i: (i, 0))

    kern = functools.partial(_merge_kernel, tiles_per_seq=tiles_per_seq, alpha=alpha)
    return pl.pallas_call(
        kern,
        grid=(t // tm,),
        in_specs=[rows(ya.shape[1]),
                  rows(cv.shape[1]),
                  pl.BlockSpec((SUBLANES, cv.shape[1]), lambda i: (jnp.maximum(i * halo - 1, 0), 0)),
                  rows(qm.shape[1]),
                  pl.BlockSpec((mem_len, kvm.shape[1]), lambda i: (i // tiles_per_seq, 0)),
                  rows(g.shape[1]),
                  rows(d),
                  full(wa), full(wc), full(wm), full(wo), full(conv_w),
                  full(ln_g), full(ln_b), full(wr), full(rb)],
        out_specs=[rows(d), rows(d), rows(LANES)],
        out_shape=[jax.ShapeDtypeStruct((t, d), F32),
                   jax.ShapeDtypeStruct((t, d), BF16),
                   jax.ShapeDtypeStruct((t, LANES), F32)],
        compiler_params=_params(("parallel",)),
        name="merge_ln_router",
    )(ya, cv, cv, qm, kvm, g, xf, wa, wc, wm, wo, conv_w, ln_g, ln_b, wr, rb)


def _moe_kernel(xb_ref, xf_ref, comb_ref, wg_ref, wu_ref, wd_ref, lng_ref, lnb_ref,
                xo_ref, xob_ref, acc_sc, *, alpha):
    e = pl.program_id(1)

    @pl.when(e == 0)
    def _():
        acc_sc[...] = jnp.zeros(acc_sc.shape, F32)

    x = xb_ref[...]
    hg = jnp.dot(x, wg_ref[...], preferred_element_type=F32)
    hu = jnp.dot(x, wu_ref[...], preferred_element_type=F32)
    h = (hg * jax.nn.sigmoid(hg)) * hu
    y = jnp.dot(h.astype(BF16), wd_ref[...], preferred_element_type=F32)
    comb = comb_ref[...]
    lane = lax.broadcasted_iota(jnp.int32, comb.shape, 1)
    w = jnp.sum(jnp.where(lane == e, comb, 0.0), axis=-1, keepdims=True)
    acc_sc[...] += jnp.where(w != 0.0, w * y, 0.0)

    @pl.when(e == pl.num_programs(1) - 1)
    def _():
        x2 = _layer_norm(alpha * xf_ref[...] + acc_sc[...], lng_ref[...], lnb_ref[...])
        xo_ref[...] = x2
        xob_ref[...] = x2.astype(BF16)


def _moe(xb, xf, comb, wg, wu, wd, ln_g, ln_b, alpha):
    t, d = xf.shape
    tm = min(MOE_TM, t)
    assert t % tm == 0
    n_e, _, dff = wg.shape
    rows = lambda width: pl.BlockSpec((tm, width), lambda i, e: (i, 0))
    vec = pl.BlockSpec((1, d), lambda i, e: (0, 0))
    return pl.pallas_call(
        functools.partial(_moe_kernel, alpha=alpha),
        grid=(t // tm, n_e),
        in_specs=[rows(d), rows(d), rows(LANES),
                  pl.BlockSpec((None, d, dff), lambda i, e: (e, 0, 0)),
                  pl.BlockSpec((None, d, dff), lambda i, e: (e, 0, 0)),
                  pl.BlockSpec((None, dff, d), lambda i, e: (e, 0, 0)),
                  vec, vec],
        out_specs=[rows(d), rows(d)],
        out_shape=[jax.ShapeDtypeStruct((t, d), F32),
                   jax.ShapeDtypeStruct((t, d), BF16)],
        scratch_shapes=[pltpu.VMEM((tm, d), F32)],
        compiler_params=_params(("parallel", "arbitrary")),
        name="moe_dense",
    )(xb, xf, comb, wg, wu, wd, ln_g, ln_b)


def kernel(x, mem, positions, w_in, lambda_q1, lambda_k1, lambda_q2, lambda_k2, diff_subln_g, conv_w, w_mem_kv, w_br_attn, w_br_conv, w_br_mem, w_out, ln1_g, ln1_b, ln2_g, ln2_b, w_router, router_bias, w_exp_gate, w_exp_up, w_exp_down):
    batch, seq, d = x.shape
    depth = w_in.shape[0]
    mem_len = mem.shape[1]
    t = batch * seq
    alpha = (2 * depth) ** 0.25

    cos_t, sin_t = _rope_tables(positions, min(PROJ_TM, t))
    lam_init = [0.8 - 0.6 * math.exp(-0.3 * l) for l in range(depth)]
    lam = _lambdas(lambda_q1, lambda_k1, lambda_q2, lambda_k2, lam_init)
    init_rows = jnp.broadcast_to(jnp.asarray(lam_init, F32).reshape(depth, 1), (depth, LANES))

    wr = jnp.zeros((d, LANES), F32).at[:, :N_EXPERTS].set(w_router.astype(F32))
    rb = jnp.zeros((1, LANES), F32).at[0, :N_EXPERTS].set(router_bias.astype(F32))
    mem_b = mem.reshape(batch * mem_len, d).astype(BF16)

    xf = x.reshape(t, d).astype(F32)
    xb = xf.astype(BF16)
    for l in range(depth):
        qk, v, cv, qm, g = _projection(xb, w_in[l].astype(BF16), cos_t, sin_t)
        ya = _attention(qk, v, lam[l:l + 1], init_rows[l:l + 1],
                        diff_subln_g[l].astype(F32).reshape(1, DA_V_DIM), batch, seq)
        kvm = _matmul(mem_b, w_mem_kv[l].astype(BF16), MEM_WIDTH, BF16)
        xf, xb, comb = _merge(ya, cv, qm, kvm, g, xf,
                              w_br_attn[l].astype(BF16), w_br_conv[l].astype(BF16),
                              w_br_mem[l].astype(BF16), w_out[l].astype(BF16),
                              conv_w[l].astype(F32),
                              ln1_g[l].astype(F32).reshape(1, d), ln1_b[l].astype(F32).reshape(1, d),
                              wr, rb, seq, mem_len, alpha)
        xf, xb = _moe(xb, xf, comb, w_exp_gate[l].astype(BF16), w_exp_up[l].astype(BF16),
                      w_exp_down[l].astype(BF16),
                      ln2_g[l].astype(F32).reshape(1, d), ln2_b[l].astype(F32).reshape(1, d), alpha)
    return xf.reshape(batch, seq, d).astype(x.dtype)
```

```python
import functools
import math

import jax
import jax.numpy as jnp
from jax import lax
from jax.experimental import pallas as pl
from jax.experimental.pallas import tpu as pltpu

F32 = jnp.float32
BF16 = jnp.bfloat16

DA_HEADS = 8
DA_HEAD_DIM = 64
DA_V_DIM = 2 * DA_HEAD_DIM
CONV_WIDTH = 512
CONV_K = 3
MEM_HEADS = 4
MEM_HEAD_DIM = 128
MEM_WIDTH = MEM_HEADS * MEM_HEAD_DIM
N_BRANCH = 3
ROPE_THETA = 10000.0
N_EXPERTS = 16
N_GROUPS = 4
EXPERTS_PER_GROUP = N_EXPERTS // N_GROUPS
LN_EPS = 1e-5
RMS_EPS = 1e-5

LANES = 128
SUBLANES = 8
VMEM_LIMIT_BYTES = 56 * 1024 * 1024

PROJ_TM = 1024
PROJ_TN = 512
ATTN_TQ = 512
MERGE_TM = 256
MOE_TM = 1024


def _params(semantics):
    return pltpu.CompilerParams(dimension_semantics=semantics,
                                vmem_limit_bytes=VMEM_LIMIT_BYTES)


def _rope_table_kernel(pos_ref, inv_ref, cos_ref, sin_ref):
    ang = pos_ref[...].astype(F32) * inv_ref[...]
    lane = lax.broadcasted_iota(jnp.int32, ang.shape, 1)
    first_half = (lane % DA_HEAD_DIM) < (DA_HEAD_DIM // 2)
    s = jnp.sin(ang)
    cos_ref[...] = jnp.cos(ang)
    sin_ref[...] = jnp.where(first_half, -s, s)


def _rope_tables(positions, tm):
    t = positions.size
    half = DA_HEAD_DIM // 2
    inv_freq = ROPE_THETA ** (-jnp.arange(0, DA_HEAD_DIM, 2, dtype=F32) / DA_HEAD_DIM)
    inv_lanes = jnp.tile(inv_freq, LANES // half).reshape(1, LANES)
    pos = positions.reshape(t, 1)
    return pl.pallas_call(
        _rope_table_kernel,
        grid=(t // tm,),
        in_specs=[pl.BlockSpec((tm, 1), lambda i: (i, 0)),
                  pl.BlockSpec((1, LANES), lambda i: (0, 0))],
        out_specs=[pl.BlockSpec((tm, LANES), lambda i: (i, 0)),
                   pl.BlockSpec((tm, LANES), lambda i: (i, 0))],
        out_shape=[jax.ShapeDtypeStruct((t, LANES), F32),
                   jax.ShapeDtypeStruct((t, LANES), F32)],
        compiler_params=_params(("parallel",)),
        name="rope_tables",
    )(pos, inv_lanes)


def _lambda_kernel(q1_ref, k1_ref, q2_ref, k2_ref, init_ref, lam_ref):
    a = jnp.sum(q1_ref[...] * k1_ref[...], axis=-1, keepdims=True)
    b = jnp.sum(q2_ref[...] * k2_ref[...], axis=-1, keepdims=True)
    lam_ref[...] = jnp.exp(a) - jnp.exp(b) + init_ref[...]


def _lambdas(lq1, lk1, lq2, lk2, lam_init):
    depth = lq1.shape[0]
    init = jnp.broadcast_to(jnp.asarray(lam_init, F32).reshape(depth, 1), (depth, LANES))
    return pl.pallas_call(
        _lambda_kernel,
        out_shape=jax.ShapeDtypeStruct((depth, LANES), F32),
        name="diff_lambda",
    )(lq1.astype(F32), lk1.astype(F32), lq2.astype(F32), lk2.astype(F32), init)


_Q_TILE0, _K_TILE0, _V_TILE0, _CV_TILE0, _QM_TILE, _G_TILE0, _N_TILES = 0, 2, 4, 6, 9, 10, 16
_Q_SCALE = (DA_HEAD_DIM ** -0.5) * math.log2(math.e)


def _rotary(acc, cos_ref, sin_ref):
    tn = acc.shape[1]
    reps = tn // LANES
    c = jnp.concatenate([cos_ref[...]] * reps, axis=1)
    s = jnp.concatenate([sin_ref[...]] * reps, axis=1)
    lane = lax.broadcasted_iota(jnp.int32, acc.shape, 1)
    half = DA_HEAD_DIM // 2
    first_half = (lane % DA_HEAD_DIM) < half
    partner = jnp.where(first_half,
                        pltpu.roll(acc, tn - half, axis=1),
                        pltpu.roll(acc, half, axis=1))
    return acc * c + partner * s


def _proj_kernel(x_ref, w_ref, cos_ref, sin_ref, qt_ref, k_ref, vt_ref, cv_ref, qm_ref, g_ref):
    j = pl.program_id(1)
    acc = jnp.dot(x_ref[...], w_ref[...], preferred_element_type=F32)

    @pl.when(j < _K_TILE0)
    def _():
        qt_ref[...] = (_rotary(acc, cos_ref, sin_ref) * _Q_SCALE).T.astype(qt_ref.dtype)

    @pl.when((j >= _K_TILE0) & (j < _V_TILE0))
    def _():
        k_ref[...] = _rotary(acc, cos_ref, sin_ref).astype(k_ref.dtype)

    @pl.when((j >= _V_TILE0) & (j < _CV_TILE0))
    def _():
        vt_ref[...] = acc.T.astype(vt_ref.dtype)

    @pl.when((j >= _CV_TILE0) & (j < _QM_TILE))
    def _():
        cv_ref[...] = acc

    @pl.when(j == _QM_TILE)
    def _():
        qm_ref[...] = acc.astype(qm_ref.dtype)

    @pl.when(j >= _G_TILE0)
    def _():
        g_ref[...] = jax.nn.sigmoid(acc)


def _projection(xb, w, cos_t, sin_t, batch, seq):
    t, d = xb.shape
    tm, tn = min(PROJ_TM, seq), PROJ_TN
    assert seq % tm == 0 and w.shape == (d, _N_TILES * tn)
    tps = seq // tm

    def clip(j, lo, n):
        return jnp.clip(j - lo, 0, n - 1)

    return pl.pallas_call(
        _proj_kernel,
        grid=(t // tm, _N_TILES),
        in_specs=[pl.BlockSpec((tm, d), lambda i, j: (i, 0)),
                  pl.BlockSpec((d, tn), lambda i, j: (0, j)),
                  pl.BlockSpec((tm, LANES), lambda i, j: (i, 0)),
                  pl.BlockSpec((tm, LANES), lambda i, j: (i, 0))],
        out_specs=[pl.BlockSpec((None, tn, tm), lambda i, j: (i // tps, clip(j, _Q_TILE0, 2), i % tps)),
                   pl.BlockSpec((tm, tn), lambda i, j: (i, clip(j, _K_TILE0, 2))),
                   pl.BlockSpec((None, tn, tm), lambda i, j: (i // tps, clip(j, _V_TILE0, 2), i % tps)),
                   pl.BlockSpec((tm, tn), lambda i, j: (i, clip(j, _CV_TILE0, 3))),
                   pl.BlockSpec((tm, tn), lambda i, j: (i, 0)),
                   pl.BlockSpec((tm, tn), lambda i, j: (i, clip(j, _G_TILE0, 6)))],
        out_shape=[jax.ShapeDtypeStruct((batch, 2 * tn, seq), BF16),
                   jax.ShapeDtypeStruct((t, 2 * tn), BF16),
                   jax.ShapeDtypeStruct((batch, 2 * tn, seq), BF16),
                   jax.ShapeDtypeStruct((t, 3 * tn), F32),
                   jax.ShapeDtypeStruct((t, tn), BF16),
                   jax.ShapeDtypeStruct((t, 6 * tn), F32)],
        compiler_params=_params(("parallel", "arbitrary")),
        name="in_projection",
    )(xb, w, cos_t, sin_t)


def _matmul_kernel(a_ref, b_ref, o_ref):
    o_ref[...] = jnp.dot(a_ref[...], b_ref[...], preferred_element_type=F32).astype(o_ref.dtype)


def _matmul(a, b, tn, out_dtype):
    m, k = a.shape
    n = b.shape[1]
    return pl.pallas_call(
        _matmul_kernel,
        grid=(n // tn,),
        in_specs=[pl.BlockSpec((m, k), lambda j: (0, 0)),
                  pl.BlockSpec((k, tn), lambda j: (0, j))],
        out_specs=pl.BlockSpec((m, tn), lambda j: (0, j)),
        out_shape=jax.ShapeDtypeStruct((m, n), out_dtype),
        compiler_params=_params(("parallel",)),
        name="mem_kv_projection",
    )(a, b)


def _sublane_all(op, x):
    for shift in (4, 2, 1):
        x = op(x, pltpu.roll(x, shift, axis=0))
    return x


def _attn_kernel(lam_ref, init_ref, g_ref, qt_ref, k_ref, vt_ref, o_ref, m_sc, l_sc, acc_sc):
    qi = pl.program_id(2)
    dv, tq = qt_ref.shape
    tk, r = tq, 2 * tq
    qt = qt_ref[...]
    sub = lax.broadcasted_iota(jnp.int32, qt.shape, 0)
    zero = jnp.zeros_like(qt)
    qq = jnp.concatenate([jnp.where(sub < DA_HEAD_DIM, qt, zero),
                          jnp.where(sub >= DA_HEAD_DIM, qt, zero)], axis=1)

    m_sc[...] = jnp.full(m_sc.shape, -jnp.inf, F32)
    l_sc[...] = jnp.zeros(l_sc.shape, F32)
    acc_sc[...] = jnp.zeros(acc_sc.shape, F32)

    def step(kb, masked):
        start = pl.multiple_of(kb * tk, tk)
        k = k_ref[pl.ds(start, tk), :]
        vt = vt_ref[:, pl.ds(start, tk)]
        s = jnp.dot(k, qq, preferred_element_type=F32)
        if masked:
            key = lax.broadcasted_iota(jnp.int32, s.shape, 0)
            qry = lax.broadcasted_iota(jnp.int32, s.shape, 1) & (tq - 1)
            s = jnp.where(key <= qry, s, -jnp.inf)
        s3 = s.reshape(tk // SUBLANES, SUBLANES, r)
        m_prev = m_sc[...]
        m_new = jnp.maximum(m_prev, _sublane_all(jnp.maximum, jnp.max(s3, axis=0)))
        p3 = jnp.exp2(s3 - m_new[None])
        alpha = jnp.exp2(m_prev - m_new)
        l_sc[...] = alpha * l_sc[...] + jnp.sum(p3, axis=0)
        pv = jnp.dot(vt, p3.reshape(tk, r).astype(vt.dtype), preferred_element_type=F32)
        acc3 = acc_sc[...].reshape(dv // SUBLANES, SUBLANES, r) * alpha[None]
        acc_sc[...] = acc3.reshape(dv, r) + pv
        m_sc[...] = m_new

    def body(kb, carry):
        step(kb, False)
        return carry

    lax.fori_loop(0, qi, body, 0)
    step(qi, True)

    l = _sublane_all(jnp.add, l_sc[...])
    ot = (acc_sc[...].reshape(dv // SUBLANES, SUBLANES, r) / l[None]).reshape(dv, r)
    o = ot[:, :tq] - lam_ref[0:1, 0:1] * ot[:, tq:]
    o = o * lax.rsqrt(jnp.mean(jnp.square(o), axis=0, keepdims=True) + RMS_EPS)
    o = o.T * g_ref[...] * (1.0 - init_ref[...])
    o_ref[...] = o.astype(o_ref.dtype)


def _attention(qt, k, vt, lam, lam_init, subln_g, batch, seq):
    t = k.shape[0]
    tq = min(ATTN_TQ, seq)
    assert seq % tq == 0 and tq & (tq - 1) == 0
    nq = seq // tq
    h = DA_HEADS
    row = pl.BlockSpec((1, LANES), lambda b, hh, i: (0, 0))
    return pl.pallas_call(
        _attn_kernel,
        grid=(batch, h, nq),
        in_specs=[row, row, row,
                  pl.BlockSpec((None, DA_V_DIM, tq), lambda b, hh, i: (b, hh, i)),
                  pl.BlockSpec((seq, DA_V_DIM), lambda b, hh, i: (b, hh)),
                  pl.BlockSpec((None, DA_V_DIM, seq), lambda b, hh, i: (b, hh, 0))],
        out_specs=pl.BlockSpec((tq, DA_V_DIM), lambda b, hh, i: (b * nq + i, hh)),
        out_shape=jax.ShapeDtypeStruct((t, h * DA_V_DIM), BF16),
        scratch_shapes=[pltpu.VMEM((SUBLANES, 2 * tq), F32),
                        pltpu.VMEM((SUBLANES, 2 * tq), F32),
                        pltpu.VMEM((DA_V_DIM, 2 * tq), F32)],
        compiler_params=_params(("parallel", "parallel", "arbitrary")),
        name="diff_attention",
    )(lam, lam_init, subln_g, qt, k, vt)


def _layer_norm(h, g, b):
    mu = jnp.mean(h, axis=-1, keepdims=True)
    d = h - mu
    var = jnp.mean(jnp.square(d), axis=-1, keepdims=True)
    return d * lax.rsqrt(var + LN_EPS) * g + b


def _route(logits):
    lane = lax.broadcasted_iota(jnp.int32, logits.shape, 1)
    valid = lane < N_EXPERTS
    logits = jnp.where(valid, logits, -jnp.inf)
    mx = jnp.max(logits, axis=-1, keepdims=True)
    ex = jnp.exp(logits - mx)
    scores = ex / jnp.sum(ex, axis=-1, keepdims=True)
    group = lane // EXPERTS_PER_GROUP
    neg = jnp.full_like(scores, -1.0)
    best = jnp.max(jnp.where(group == 0, scores, neg), axis=-1, keepdims=True)
    gidx = jnp.zeros(best.shape, jnp.int32)
    for gi in range(1, N_GROUPS):
        cand = jnp.max(jnp.where(group == gi, scores, neg), axis=-1, keepdims=True)
        take = cand > best
        gidx = jnp.where(take, gi, gidx)
        best = jnp.where(take, cand, best)
    vals = jnp.where((group == gidx) & valid, scores, neg)
    big = jnp.full_like(lane, LANES)
    v1 = jnp.max(vals, axis=-1, keepdims=True)
    i1 = jnp.min(jnp.where(vals == v1, lane, big), axis=-1, keepdims=True)
    vals2 = jnp.where(lane == i1, neg, vals)
    v2 = jnp.max(vals2, axis=-1, keepdims=True)
    i2 = jnp.min(jnp.where(vals2 == v2, lane, big), axis=-1, keepdims=True)
    tot = v1 + v2
    return jnp.where(lane == i1, v1 / tot, 0.0) + jnp.where(lane == i2, v2 / tot, 0.0)


def _merge_kernel(ya_ref, cv_ref, cvp_ref, qm_ref, kvm_ref, g_ref, x_ref,
                  wa_ref, wc_ref, wm_ref, wo_ref, cw_ref, lng_ref, lnb_ref, wr_ref, rb_ref,
                  xo_ref, xob_ref, comb_ref, *, tiles_per_seq, alpha):
    i = pl.program_id(0)
    tm = ya_ref.shape[0]
    cw = CONV_WIDTH

    z = cv_ref[:, cw:2 * cw] * cv_ref[:, 2 * cw:3 * cw]
    zp = cvp_ref[:, cw:2 * cw] * cvp_ref[:, 2 * cw:3 * cw]
    zp = jnp.where(i % tiles_per_seq == 0, jnp.zeros_like(zp), zp)
    row = lax.broadcasted_iota(jnp.int32, z.shape, 0)
    zm1 = jnp.where(row == 0, zp[SUBLANES - 1:SUBLANES], pltpu.roll(z, 1, axis=0))
    zm2 = jnp.where(row == 0, zp[SUBLANES - 2:SUBLANES - 1],
                    jnp.where(row == 1, zp[SUBLANES - 1:SUBLANES], pltpu.roll(z, 2, axis=0)))
    w = cw_ref[...]
    y = w[0:1] * zm2
    y = y + w[1:2] * zm1
    y = y + w[2:3] * z
    y_conv = (cv_ref[:, 0:cw] * y).astype(BF16)

    outs = []
    for hd in range(MEM_HEADS):
        lo = hd * MEM_HEAD_DIM
        qh = qm_ref[:, lo:lo + MEM_HEAD_DIM]
        kh = kvm_ref[:, lo:lo + MEM_HEAD_DIM]
        vh = kvm_ref[:, MEM_WIDTH + lo:MEM_WIDTH + lo + MEM_HEAD_DIM]
        s = lax.dot_general(qh, kh, (((1,), (1,)), ((), ())),
                            preferred_element_type=F32) * (MEM_HEAD_DIM ** -0.5)
        s = s - jnp.max(s, axis=-1, keepdims=True)
        e = jnp.exp(s)
        p = e / jnp.sum(e, axis=-1, keepdims=True)
        outs.append(jnp.dot(p.astype(BF16), vh, preferred_element_type=F32))
    y_mem = jnp.concatenate(outs, axis=1).astype(BF16)

    d = x_ref.shape[1]
    merged = g_ref[:, 0:d] * jnp.dot(ya_ref[...], wa_ref[...], preferred_element_type=F32)
    merged = merged + g_ref[:, d:2 * d] * jnp.dot(y_conv, wc_ref[...], preferred_element_type=F32)
    merged = merged + g_ref[:, 2 * d:3 * d] * jnp.dot(y_mem, wm_ref[...], preferred_element_type=F32)
    mix = jnp.dot(merged.astype(BF16), wo_ref[...], preferred_element_type=F32)
    x1 = _layer_norm(alpha * x_ref[...] + mix, lng_ref[...], lnb_ref[...])
    xo_ref[...] = x1
    xob_ref[...] = x1.astype(BF16)

    logits = jnp.dot(x1, wr_ref[...], preferred_element_type=F32,
                     precision=lax.Precision.HIGHEST) + rb_ref[...]
    comb_ref[...] = _route(logits)


def _merge(ya, cv, qm, kvm, g, xf, wa, wc, wm, wo, conv_w, ln_g, ln_b, wr, rb, seq, mem_len, alpha):
    t, d = xf.shape
    tm = min(MERGE_TM, seq)
    assert seq % tm == 0 and tm % SUBLANES == 0
    tiles_per_seq = seq // tm
    halo = tm // SUBLANES

    def full(a):
        return pl.BlockSpec(a.shape, lambda i: (0,) * a.ndim)

    def rows(width):
        return pl.BlockSpec((tm, width), lambda i: (i, 0))

    kern = functools.partial(_merge_kernel, tiles_per_seq=tiles_per_seq, alpha=alpha)
    return pl.pallas_call(
        kern,
        grid=(t // tm,),
        in_specs=[rows(ya.shape[1]),
                  rows(cv.shape[1]),
                  pl.BlockSpec((SUBLANES, cv.shape[1]), lambda i: (jnp.maximum(i * halo - 1, 0), 0)),
                  rows(qm.shape[1]),
                  pl.BlockSpec((mem_len, kvm.shape[1]), lambda i: (i // tiles_per_seq, 0)),
                  rows(g.shape[1]),
                  rows(d),
                  full(wa), full(wc), full(wm), full(wo), full(conv_w),
                  full(ln_g), full(ln_b), full(wr), full(rb)],
        out_specs=[rows(d), rows(d), rows(LANES)],
        out_shape=[jax.ShapeDtypeStruct((t, d), F32),
                   jax.ShapeDtypeStruct((t, d), BF16),
                   jax.ShapeDtypeStruct((t, LANES), F32)],
        compiler_params=_params(("parallel",)),
        name="merge_ln_router",
    )(ya, cv, cv, qm, kvm, g, xf, wa, wc, wm, wo, conv_w, ln_g, ln_b, wr, rb)


def _moe_kernel(xb_ref, xf_ref, comb_ref, wg_ref, wu_ref, wd_ref, lng_ref, lnb_ref,
                xo_ref, xob_ref, acc_sc, *, alpha):
    e = pl.program_id(1)

    @pl.when(e == 0)
    def _():
        acc_sc[...] = jnp.zeros(acc_sc.shape, F32)

    x = xb_ref[...]
    hg = jnp.dot(x, wg_ref[...], preferred_element_type=F32)
    hu = jnp.dot(x, wu_ref[...], preferred_element_type=F32)
    h = (hg * jax.nn.sigmoid(hg)) * hu
    y = jnp.dot(h.astype(BF16), wd_ref[...], preferred_element_type=F32)
    comb = comb_ref[...]
    lane = lax.broadcasted_iota(jnp.int32, comb.shape, 1)
    w = jnp.sum(jnp.where(lane == e, comb, 0.0), axis=-1, keepdims=True)
    acc_sc[...] += jnp.where(w != 0.0, w * y, 0.0)

    @pl.when(e == pl.num_programs(1) - 1)
    def _():
        x2 = _layer_norm(alpha * xf_ref[...] + acc_sc[...], lng_ref[...], lnb_ref[...])
        xo_ref[...] = x2
        xob_ref[...] = x2.astype(BF16)


def _moe(xb, xf, comb, wg, wu, wd, ln_g, ln_b, alpha):
    t, d = xf.shape
    tm = min(MOE_TM, t)
    assert t % tm == 0
    n_e, _, dff = wg.shape
    rows = lambda width: pl.BlockSpec((tm, width), lambda i, e: (i, 0))
    vec = pl.BlockSpec((1, d), lambda i, e: (0, 0))
    return pl.pallas_call(
        functools.partial(_moe_kernel, alpha=alpha),
        grid=(t // tm, n_e),
        in_specs=[rows(d), rows(d), rows(LANES),
                  pl.BlockSpec((None, d, dff), lambda i, e: (e, 0, 0)),
                  pl.BlockSpec((None, d, dff), lambda i, e: (e, 0, 0)),
                  pl.BlockSpec((None, dff, d), lambda i, e: (e, 0, 0)),
                  vec, vec],
        out_specs=[rows(d), rows(d)],
        out_shape=[jax.ShapeDtypeStruct((t, d), F32),
                   jax.ShapeDtypeStruct((t, d), BF16)],
        scratch_shapes=[pltpu.VMEM((tm, d), F32)],
        compiler_params=_params(("parallel", "arbitrary")),
        name="moe_dense",
    )(xb, xf, comb, wg, wu, wd, ln_g, ln_b)


def kernel(x, mem, positions, w_in, lambda_q1, lambda_k1, lambda_q2, lambda_k2, diff_subln_g, conv_w, w_mem_kv, w_br_attn, w_br_conv, w_br_mem, w_out, ln1_g, ln1_b, ln2_g, ln2_b, w_router, router_bias, w_exp_gate, w_exp_up, w_exp_down):
    batch, seq, d = x.shape
    depth = w_in.shape[0]
    mem_len = mem.shape[1]
    t = batch * seq
    alpha = (2 * depth) ** 0.25

    cos_t, sin_t = _rope_tables(positions, min(PROJ_TM, t))
    lam_init = [0.8 - 0.6 * math.exp(-0.3 * l) for l in range(depth)]
    lam = _lambdas(lambda_q1, lambda_k1, lambda_q2, lambda_k2, lam_init)
    init_rows = jnp.broadcast_to(jnp.asarray(lam_init, F32).reshape(depth, 1), (depth, LANES))

    wr = jnp.zeros((d, LANES), F32).at[:, :N_EXPERTS].set(w_router.astype(F32))
    rb = jnp.zeros((1, LANES), F32).at[0, :N_EXPERTS].set(router_bias.astype(F32))
    mem_b = mem.reshape(batch * mem_len, d).astype(BF16)

    xf = x.reshape(t, d).astype(F32)
    xb = xf.astype(BF16)
    for l in range(depth):
        qt, k, vt, cv, qm, g = _projection(xb, w_in[l].astype(BF16), cos_t, sin_t, batch, seq)
        ya = _attention(qt, k, vt, lam[l:l + 1], init_rows[l:l + 1],
                        diff_subln_g[l].astype(F32).reshape(1, DA_V_DIM), batch, seq)
        kvm = _matmul(mem_b, w_mem_kv[l].astype(BF16), MEM_WIDTH, BF16)
        xf, xb, comb = _merge(ya, cv, qm, kvm, g, xf,
                              w_br_attn[l].astype(BF16), w_br_conv[l].astype(BF16),
                              w_br_mem[l].astype(BF16), w_out[l].astype(BF16),
                              conv_w[l].astype(F32),
                              ln1_g[l].astype(F32).reshape(1, d), ln1_b[l].astype(F32).reshape(1, d),
                              wr, rb, seq, mem_len, alpha)
        xf, xb = _moe(xb, xf, comb, w_exp_gate[l].astype(BF16), w_exp_up[l].astype(BF16),
                      w_exp_down[l].astype(BF16),
                      ln2_g[l].astype(F32).reshape(1, d), ln2_b[l].astype(F32).reshape(1, d), alpha)
    return xf.reshape(batch, seq, d).astype(x.dtype)
```

```python
import functools
import math

import jax
import jax.numpy as jnp
from jax import lax
from jax.experimental import pallas as pl
from jax.experimental.pallas import tpu as pltpu

F32 = jnp.float32
BF16 = jnp.bfloat16

DA_HEADS = 8
DA_HEAD_DIM = 64
DA_V_DIM = 2 * DA_HEAD_DIM
CONV_WIDTH = 512
CONV_K = 3
MEM_HEADS = 4
MEM_HEAD_DIM = 128
MEM_WIDTH = MEM_HEADS * MEM_HEAD_DIM
N_BRANCH = 3
ROPE_THETA = 10000.0
N_EXPERTS = 16
N_GROUPS = 4
EXPERTS_PER_GROUP = N_EXPERTS // N_GROUPS
LN_EPS = 1e-5
RMS_EPS = 1e-5

LANES = 128
SUBLANES = 8
VMEM_LIMIT_BYTES = 56 * 1024 * 1024

PROJ_TM = 1024
PROJ_TN = 512
ATTN_TQ = 512
MERGE_TM = 256
MOE_TM = 1024


def _params(semantics, flags=None):
    return pltpu.CompilerParams(dimension_semantics=semantics,
                                vmem_limit_bytes=VMEM_LIMIT_BYTES, flags=flags)


def _rope_table_kernel(pos_ref, inv_ref, cos_ref, sin_ref):
    ang = pos_ref[...].astype(F32) * inv_ref[...]
    lane = lax.broadcasted_iota(jnp.int32, ang.shape, 1)
    first_half = (lane % DA_HEAD_DIM) < (DA_HEAD_DIM // 2)
    s = jnp.sin(ang)
    cos_ref[...] = jnp.cos(ang)
    sin_ref[...] = jnp.where(first_half, -s, s)


def _rope_tables(positions, tm):
    t = positions.size
    half = DA_HEAD_DIM // 2
    inv_freq = ROPE_THETA ** (-jnp.arange(0, DA_HEAD_DIM, 2, dtype=F32) / DA_HEAD_DIM)
    inv_lanes = jnp.tile(inv_freq, LANES // half).reshape(1, LANES)
    pos = positions.reshape(t, 1)
    return pl.pallas_call(
        _rope_table_kernel,
        grid=(t // tm,),
        in_specs=[pl.BlockSpec((tm, 1), lambda i: (i, 0)),
                  pl.BlockSpec((1, LANES), lambda i: (0, 0))],
        out_specs=[pl.BlockSpec((tm, LANES), lambda i: (i, 0)),
                   pl.BlockSpec((tm, LANES), lambda i: (i, 0))],
        out_shape=[jax.ShapeDtypeStruct((t, LANES), F32),
                   jax.ShapeDtypeStruct((t, LANES), F32)],
        compiler_params=_params(("parallel",)),
        name="rope_tables",
    )(pos, inv_lanes)


def _lambda_kernel(q1_ref, k1_ref, q2_ref, k2_ref, init_ref, lam_ref):
    a = jnp.sum(q1_ref[...] * k1_ref[...], axis=-1, keepdims=True)
    b = jnp.sum(q2_ref[...] * k2_ref[...], axis=-1, keepdims=True)
    lam_ref[...] = jnp.exp(a) - jnp.exp(b) + init_ref[...]


def _lambdas(lq1, lk1, lq2, lk2, lam_init):
    depth = lq1.shape[0]
    init = jnp.broadcast_to(jnp.asarray(lam_init, F32).reshape(depth, 1), (depth, LANES))
    return pl.pallas_call(
        _lambda_kernel,
        out_shape=jax.ShapeDtypeStruct((depth, LANES), F32),
        name="diff_lambda",
    )(lq1.astype(F32), lk1.astype(F32), lq2.astype(F32), lk2.astype(F32), init)


_Q_TILE0, _K_TILE0, _V_TILE0, _CV_TILE0, _QM_TILE, _G_TILE0, _N_TILES = 0, 2, 4, 6, 9, 10, 16
_Q_SCALE = (DA_HEAD_DIM ** -0.5) * math.log2(math.e)


def _rotary(acc, cos_ref, sin_ref):
    tn = acc.shape[1]
    reps = tn // LANES
    c = jnp.concatenate([cos_ref[...]] * reps, axis=1)
    s = jnp.concatenate([sin_ref[...]] * reps, axis=1)
    lane = lax.broadcasted_iota(jnp.int32, acc.shape, 1)
    half = DA_HEAD_DIM // 2
    first_half = (lane % DA_HEAD_DIM) < half
    partner = jnp.where(first_half,
                        pltpu.roll(acc, tn - half, axis=1),
                        pltpu.roll(acc, half, axis=1))
    return acc * c + partner * s


def _proj_kernel(x_ref, w_ref, cos_ref, sin_ref, qt_ref, k_ref, vt_ref, cv_ref, qm_ref, g_ref):
    j = pl.program_id(1)
    acc = jnp.dot(x_ref[...], w_ref[...], preferred_element_type=F32)

    @pl.when(j < _K_TILE0)
    def _():
        qt_ref[...] = (_rotary(acc, cos_ref, sin_ref) * _Q_SCALE).T.astype(qt_ref.dtype)

    @pl.when((j >= _K_TILE0) & (j < _V_TILE0))
    def _():
        k_ref[...] = _rotary(acc, cos_ref, sin_ref).astype(k_ref.dtype)

    @pl.when((j >= _V_TILE0) & (j < _CV_TILE0))
    def _():
        vt_ref[...] = acc.T.astype(vt_ref.dtype)

    @pl.when((j >= _CV_TILE0) & (j < _QM_TILE))
    def _():
        cv_ref[...] = acc

    @pl.when(j == _QM_TILE)
    def _():
        qm_ref[...] = acc.astype(qm_ref.dtype)

    @pl.when(j >= _G_TILE0)
    def _():
        g_ref[...] = jax.nn.sigmoid(acc)


def _projection(xb, w, cos_t, sin_t, batch, seq):
    t, d = xb.shape
    tm, tn = min(PROJ_TM, seq), PROJ_TN
    assert seq % tm == 0 and w.shape == (d, _N_TILES * tn)
    tps = seq // tm

    def clip(j, lo, n):
        return jnp.clip(j - lo, 0, n - 1)

    return pl.pallas_call(
        _proj_kernel,
        grid=(t // tm, _N_TILES),
        in_specs=[pl.BlockSpec((tm, d), lambda i, j: (i, 0)),
                  pl.BlockSpec((d, tn), lambda i, j: (0, j)),
                  pl.BlockSpec((tm, LANES), lambda i, j: (i, 0)),
                  pl.BlockSpec((tm, LANES), lambda i, j: (i, 0))],
        out_specs=[pl.BlockSpec((None, tn, tm), lambda i, j: (i // tps, clip(j, _Q_TILE0, 2), i % tps)),
                   pl.BlockSpec((tm, tn), lambda i, j: (i, clip(j, _K_TILE0, 2))),
                   pl.BlockSpec((None, tn, tm), lambda i, j: (i // tps, clip(j, _V_TILE0, 2), i % tps)),
                   pl.BlockSpec((tm, tn), lambda i, j: (i, clip(j, _CV_TILE0, 3))),
                   pl.BlockSpec((tm, tn), lambda i, j: (i, 0)),
                   pl.BlockSpec((tm, tn), lambda i, j: (i, clip(j, _G_TILE0, 6)))],
        out_shape=[jax.ShapeDtypeStruct((batch, 2 * tn, seq), BF16),
                   jax.ShapeDtypeStruct((t, 2 * tn), BF16),
                   jax.ShapeDtypeStruct((batch, 2 * tn, seq), BF16),
                   jax.ShapeDtypeStruct((t, 3 * tn), F32),
                   jax.ShapeDtypeStruct((t, tn), BF16),
                   jax.ShapeDtypeStruct((t, 6 * tn), F32)],
        compiler_params=_params(("parallel", "arbitrary")),
        name="in_projection",
    )(xb, w, cos_t, sin_t)


def _matmul_kernel(a_ref, b_ref, o_ref):
    o_ref[...] = jnp.dot(a_ref[...], b_ref[...], preferred_element_type=F32).astype(o_ref.dtype)


def _matmul(a, b, tn, out_dtype):
    m, k = a.shape
    n = b.shape[1]
    return pl.pallas_call(
        _matmul_kernel,
        grid=(n // tn,),
        in_specs=[pl.BlockSpec((m, k), lambda j: (0, 0)),
                  pl.BlockSpec((k, tn), lambda j: (0, j))],
        out_specs=pl.BlockSpec((m, tn), lambda j: (0, j)),
        out_shape=jax.ShapeDtypeStruct((m, n), out_dtype),
        compiler_params=_params(("parallel",)),
        name="mem_kv_projection",
    )(a, b)


def _sublane_all(op, x):
    for shift in (4, 2, 1):
        x = op(x, pltpu.roll(x, shift, axis=0))
    return x


def _attn_kernel(lam_ref, init_ref, g_ref, qt_ref, k_ref, vt_ref, o_ref,
                 s0_sc, s1_sc, x0_sc, x1_sc, p0_sc, p1_sc, a0_sc, a1_sc, m_sc, l_sc, acc_sc):
    s_sc, x_sc, p_sc, a_sc = (s0_sc, s1_sc), (x0_sc, x1_sc), (p0_sc, p1_sc), (a0_sc, a1_sc)
    qi = pl.program_id(2)
    dv, tq = qt_ref.shape
    tk, r = tq, 2 * tq
    qt = qt_ref[...]
    sub = lax.broadcasted_iota(jnp.int32, qt.shape, 0)
    zero = jnp.zeros_like(qt)
    qq = jnp.concatenate([jnp.where(sub < DA_HEAD_DIM, qt, zero),
                          jnp.where(sub >= DA_HEAD_DIM, qt, zero)], axis=1)
    ones = jnp.ones((2 * SUBLANES, tk), vt_ref.dtype)

    m_sc[...] = jnp.full(m_sc.shape, -jnp.inf, F32)
    l_sc[...] = jnp.zeros(l_sc.shape, F32)
    acc_sc[...] = jnp.zeros(acc_sc.shape, F32)

    def scores(kb, slot, masked=False):
        start = pl.multiple_of(kb * tk, tk)
        s = jnp.dot(k_ref[pl.ds(start, tk), :], qq, preferred_element_type=F32)
        if masked:
            key = lax.broadcasted_iota(jnp.int32, s.shape, 0)
            qry = lax.broadcasted_iota(jnp.int32, s.shape, 1) & (tq - 1)
            s = jnp.where(key <= qry, s, -jnp.inf)
        s_sc[slot][...] = s
        x_sc[slot][...] = jnp.max(s.reshape(tk // SUBLANES, SUBLANES, r), axis=0)

    def softmax(slot):
        s3 = s_sc[slot][...].reshape(tk // SUBLANES, SUBLANES, r)
        m_prev = m_sc[...]
        m_new = jnp.maximum(m_prev, _sublane_all(jnp.maximum, x_sc[slot][...]))
        p_sc[slot][...] = jnp.exp2(s3 - m_new[None]).reshape(tk, r).astype(p_sc[slot].dtype)
        a_sc[slot][...] = jnp.exp2(m_prev - m_new)
        m_sc[...] = m_new

    def pv(kb, slot):
        start = pl.multiple_of(kb * tk, tk)
        lhs = jnp.concatenate([vt_ref[:, pl.ds(start, tk)], ones], axis=0)
        res = jnp.dot(lhs, p_sc[slot][...], preferred_element_type=F32)
        alpha = a_sc[slot][...]
        acc3 = acc_sc[...].reshape(dv // SUBLANES, SUBLANES, r) * alpha[None]
        acc_sc[...] = acc3.reshape(dv, r) + res[:dv]
        l_sc[...] = alpha * l_sc[...] + res[dv:dv + SUBLANES]

    n = qi
    scores(qi, 0, masked=True)

    def odd_iter(t):
        scores(t, 0)
        pv(jnp.where(t == 1, qi, t - 2), 0)
        softmax(1)

    def even_iter(t):
        scores(t, 1)
        pv(t - 2, 1)
        softmax(0)

    @pl.when(n == 0)
    def _():
        softmax(0)
        pv(qi, 0)

    @pl.when(n >= 1)
    def _():
        scores(0, 1)
        softmax(0)

        def pair(j, carry):
            odd_iter(2 * j + 1)
            even_iter(2 * j + 2)
            return carry

        lax.fori_loop(0, (n - 1) // 2, pair, 0)

        @pl.when(n % 2 == 0)
        def _():
            odd_iter(n - 1)
            softmax(0)
            pv(n - 2, 1)
            pv(n - 1, 0)

        @pl.when(n % 2 == 1)
        def _():
            softmax(1)
            pv(jnp.where(n == 1, qi, n - 2), 0)
            pv(n - 1, 1)

    ot = (acc_sc[...].reshape(dv // SUBLANES, SUBLANES, r) / l_sc[...][None]).reshape(dv, r)
    o = ot[:, :tq] - lam_ref[0:1, 0:1] * ot[:, tq:]
    o = o * lax.rsqrt(jnp.mean(jnp.square(o), axis=0, keepdims=True) + RMS_EPS)
    o = o.T * g_ref[...] * (1.0 - init_ref[...])
    o_ref[...] = o.astype(o_ref.dtype)


def _attention(qt, k, vt, lam, lam_init, subln_g, batch, seq):
    t = k.shape[0]
    tq = min(ATTN_TQ, seq)
    assert seq % tq == 0 and tq & (tq - 1) == 0
    nq = seq // tq
    h = DA_HEADS
    row = pl.BlockSpec((1, LANES), lambda b, hh, i: (0, 0))
    return pl.pallas_call(
        _attn_kernel,
        grid=(batch, h, nq),
        in_specs=[row, row, row,
                  pl.BlockSpec((None, DA_V_DIM, tq), lambda b, hh, i: (b, hh, i)),
                  pl.BlockSpec((seq, DA_V_DIM), lambda b, hh, i: (b, hh)),
                  pl.BlockSpec((None, DA_V_DIM, seq), lambda b, hh, i: (b, hh, 0))],
        out_specs=pl.BlockSpec((tq, DA_V_DIM), lambda b, hh, i: (b * nq + i, hh)),
        out_shape=jax.ShapeDtypeStruct((t, h * DA_V_DIM), BF16),
        scratch_shapes=[pltpu.VMEM((tq, 2 * tq), F32), pltpu.VMEM((tq, 2 * tq), F32),
                        pltpu.VMEM((SUBLANES, 2 * tq), F32), pltpu.VMEM((SUBLANES, 2 * tq), F32),
                        pltpu.VMEM((tq, 2 * tq), BF16), pltpu.VMEM((tq, 2 * tq), BF16),
                        pltpu.VMEM((SUBLANES, 2 * tq), F32), pltpu.VMEM((SUBLANES, 2 * tq), F32),
                        pltpu.VMEM((SUBLANES, 2 * tq), F32),
                        pltpu.VMEM((SUBLANES, 2 * tq), F32),
                        pltpu.VMEM((DA_V_DIM, 2 * tq), F32)],
        compiler_params=_params(("parallel", "parallel", "arbitrary")),
        name="diff_attention",
    )(lam, lam_init, subln_g, qt, k, vt)


def _layer_norm(h, g, b):
    mu = jnp.mean(h, axis=-1, keepdims=True)
    d = h - mu
    var = jnp.mean(jnp.square(d), axis=-1, keepdims=True)
    return d * lax.rsqrt(var + LN_EPS) * g + b


def _route(logits):
    lane = lax.broadcasted_iota(jnp.int32, logits.shape, 1)
    valid = lane < N_EXPERTS
    logits = jnp.where(valid, logits, -jnp.inf)
    mx = jnp.max(logits, axis=-1, keepdims=True)
    ex = jnp.exp(logits - mx)
    scores = ex / jnp.sum(ex, axis=-1, keepdims=True)
    group = lane // EXPERTS_PER_GROUP
    neg = jnp.full_like(scores, -1.0)
    best = jnp.max(jnp.where(group == 0, scores, neg), axis=-1, keepdims=True)
    gidx = jnp.zeros(best.shape, jnp.int32)
    for gi in range(1, N_GROUPS):
        cand = jnp.max(jnp.where(group == gi, scores, neg), axis=-1, keepdims=True)
        take = cand > best
        gidx = jnp.where(take, gi, gidx)
        best = jnp.where(take, cand, best)
    vals = jnp.where((group == gidx) & valid, scores, neg)
    big = jnp.full_like(lane, LANES)
    v1 = jnp.max(vals, axis=-1, keepdims=True)
    i1 = jnp.min(jnp.where(vals == v1, lane, big), axis=-1, keepdims=True)
    vals2 = jnp.where(lane == i1, neg, vals)
    v2 = jnp.max(vals2, axis=-1, keepdims=True)
    i2 = jnp.min(jnp.where(vals2 == v2, lane, big), axis=-1, keepdims=True)
    tot = v1 + v2
    return jnp.where(lane == i1, v1 / tot, 0.0) + jnp.where(lane == i2, v2 / tot, 0.0)


def _merge_kernel(ya_ref, cv_ref, cvp_ref, qm_ref, kvm_ref, g_ref, x_ref,
                  wa_ref, wc_ref, wm_ref, wo_ref, cw_ref, lng_ref, lnb_ref, wr_ref, rb_ref,
                  xo_ref, xob_ref, comb_ref, *, tiles_per_seq, alpha):
    i = pl.program_id(0)
    tm = ya_ref.shape[0]
    cw = CONV_WIDTH

    z = cv_ref[:, cw:2 * cw] * cv_ref[:, 2 * cw:3 * cw]
    zp = cvp_ref[:, cw:2 * cw] * cvp_ref[:, 2 * cw:3 * cw]
    zp = jnp.where(i % tiles_per_seq == 0, jnp.zeros_like(zp), zp)
    row = lax.broadcasted_iota(jnp.int32, z.shape, 0)
    zm1 = jnp.where(row == 0, zp[SUBLANES - 1:SUBLANES], pltpu.roll(z, 1, axis=0))
    zm2 = jnp.where(row == 0, zp[SUBLANES - 2:SUBLANES - 1],
                    jnp.where(row == 1, zp[SUBLANES - 1:SUBLANES], pltpu.roll(z, 2, axis=0)))
    w = cw_ref[...]
    y = w[0:1] * zm2
    y = y + w[1:2] * zm1
    y = y + w[2:3] * z
    y_conv = (cv_ref[:, 0:cw] * y).astype(BF16)

    outs = []
    for hd in range(MEM_HEADS):
        lo = hd * MEM_HEAD_DIM
        qh = qm_ref[:, lo:lo + MEM_HEAD_DIM]
        kh = kvm_ref[:, lo:lo + MEM_HEAD_DIM]
        vh = kvm_ref[:, MEM_WIDTH + lo:MEM_WIDTH + lo + MEM_HEAD_DIM]
        s = lax.dot_general(qh, kh, (((1,), (1,)), ((), ())),
                            preferred_element_type=F32) * (MEM_HEAD_DIM ** -0.5)
        s = s - jnp.max(s, axis=-1, keepdims=True)
        e = jnp.exp(s)
        p = e / jnp.sum(e, axis=-1, keepdims=True)
        outs.append(jnp.dot(p.astype(BF16), vh, preferred_element_type=F32))
    y_mem = jnp.concatenate(outs, axis=1).astype(BF16)

    d = x_ref.shape[1]
    merged = g_ref[:, 0:d] * jnp.dot(ya_ref[...], wa_ref[...], preferred_element_type=F32)
    merged = merged + g_ref[:, d:2 * d] * jnp.dot(y_conv, wc_ref[...], preferred_element_type=F32)
    merged = merged + g_ref[:, 2 * d:3 * d] * jnp.dot(y_mem, wm_ref[...], preferred_element_type=F32)
    mix = jnp.dot(merged.astype(BF16), wo_ref[...], preferred_element_type=F32)
    x1 = _layer_norm(alpha * x_ref[...] + mix, lng_ref[...], lnb_ref[...])
    xo_ref[...] = x1
    xob_ref[...] = x1.astype(BF16)

    logits = jnp.dot(x1, wr_ref[...], preferred_element_type=F32,
                     precision=lax.Precision.HIGHEST) + rb_ref[...]
    comb_ref[...] = _route(logits)


def _merge(ya, cv, qm, kvm, g, xf, wa, wc, wm, wo, conv_w, ln_g, ln_b, wr, rb, seq, mem_len, alpha):
    t, d = xf.shape
    tm = min(MERGE_TM, seq)
    assert seq % tm == 0 and tm % SUBLANES == 0
    tiles_per_seq = seq // tm
    halo = tm // SUBLANES

    def full(a):
        return pl.BlockSpec(a.shape, lambda i: (0,) * a.ndim)

    def rows(width):
        return pl.BlockSpec((tm, width), lambda i: (i, 0))

    kern = functools.partial(_merge_kernel, tiles_per_seq=tiles_per_seq, alpha=alpha)
    return pl.pallas_call(
        kern,
        grid=(t // tm,),
        in_specs=[rows(ya.shape[1]),
                  rows(cv.shape[1]),
                  pl.BlockSpec((SUBLANES, cv.shape[1]), lambda i: (jnp.maximum(i * halo - 1, 0), 0)),
                  rows(qm.shape[1]),
                  pl.BlockSpec((mem_len, kvm.shape[1]), lambda i: (i // tiles_per_seq, 0)),
                  rows(g.shape[1]),
                  rows(d),
                  full(wa), full(wc), full(wm), full(wo), full(conv_w),
                  full(ln_g), full(ln_b), full(wr), full(rb)],
        out_specs=[rows(d), rows(d), rows(LANES)],
        out_shape=[jax.ShapeDtypeStruct((t, d), F32),
                   jax.ShapeDtypeStruct((t, d), BF16),
                   jax.ShapeDtypeStruct((t, LANES), F32)],
        compiler_params=_params(("parallel",)),
        name="merge_ln_router",
    )(ya, cv, cv, qm, kvm, g, xf, wa, wc, wm, wo, conv_w, ln_g, ln_b, wr, rb)


def _moe_kernel(xb_ref, xf_ref, comb_ref, wg_ref, wu_ref, wd_ref, lng_ref, lnb_ref,
                xo_ref, xob_ref, acc_sc, *, alpha):
    e = pl.program_id(1)

    @pl.when(e == 0)
    def _():
        acc_sc[...] = jnp.zeros(acc_sc.shape, F32)

    x = xb_ref[...]
    hg = jnp.dot(x, wg_ref[...], preferred_element_type=F32)
    hu = jnp.dot(x, wu_ref[...], preferred_element_type=F32)
    h = (hg * jax.nn.sigmoid(hg)) * hu
    y = jnp.dot(h.astype(BF16), wd_ref[...], preferred_element_type=F32)
    comb = comb_ref[...]
    lane = lax.broadcasted_iota(jnp.int32, comb.shape, 1)
    w = jnp.sum(jnp.where(lane == e, comb, 0.0), axis=-1, keepdims=True)
    acc_sc[...] += jnp.where(w != 0.0, w * y, 0.0)

    @pl.when(e == pl.num_programs(1) - 1)
    def _():
        x2 = _layer_norm(alpha * xf_ref[...] + acc_sc[...], lng_ref[...], lnb_ref[...])
        xo_ref[...] = x2
        xob_ref[...] = x2.astype(BF16)


def _moe(xb, xf, comb, wg, wu, wd, ln_g, ln_b, alpha):
    t, d = xf.shape
    tm = min(MOE_TM, t)
    assert t % tm == 0
    n_e, _, dff = wg.shape
    rows = lambda width: pl.BlockSpec((tm, width), lambda i, e: (i, 0))
    vec = pl.BlockSpec((1, d), lambda i, e: (0, 0))
    return pl.pallas_call(
        functools.partial(_moe_kernel, alpha=alpha),
        grid=(t // tm, n_e),
        in_specs=[rows(d), rows(d), rows(LANES),
                  pl.BlockSpec((None, d, dff), lambda i, e: (e, 0, 0)),
                  pl.BlockSpec((None, d, dff), lambda i, e: (e, 0, 0)),
                  pl.BlockSpec((None, dff, d), lambda i, e: (e, 0, 0)),
                  vec, vec],
        out_specs=[rows(d), rows(d)],
        out_shape=[jax.ShapeDtypeStruct((t, d), F32),
                   jax.ShapeDtypeStruct((t, d), BF16)],
        scratch_shapes=[pltpu.VMEM((tm, d), F32)],
        compiler_params=_params(("parallel", "arbitrary")),
        name="moe_dense",
    )(xb, xf, comb, wg, wu, wd, ln_g, ln_b)


def kernel(x, mem, positions, w_in, lambda_q1, lambda_k1, lambda_q2, lambda_k2, diff_subln_g, conv_w, w_mem_kv, w_br_attn, w_br_conv, w_br_mem, w_out, ln1_g, ln1_b, ln2_g, ln2_b, w_router, router_bias, w_exp_gate, w_exp_up, w_exp_down):
    batch, seq, d = x.shape
    depth = w_in.shape[0]
    mem_len = mem.shape[1]
    t = batch * seq
    alpha = (2 * depth) ** 0.25

    cos_t, sin_t = _rope_tables(positions, min(PROJ_TM, t))
    lam_init = [0.8 - 0.6 * math.exp(-0.3 * l) for l in range(depth)]
    lam = _lambdas(lambda_q1, lambda_k1, lambda_q2, lambda_k2, lam_init)
    init_rows = jnp.broadcast_to(jnp.asarray(lam_init, F32).reshape(depth, 1), (depth, LANES))

    wr = jnp.zeros((d, LANES), F32).at[:, :N_EXPERTS].set(w_router.astype(F32))
    rb = jnp.zeros((1, LANES), F32).at[0, :N_EXPERTS].set(router_bias.astype(F32))
    mem_b = mem.reshape(batch * mem_len, d).astype(BF16)

    xf = x.reshape(t, d).astype(F32)
    xb = xf.astype(BF16)
    for l in range(depth):
        qt, k, vt, cv, qm, g = _projection(xb, w_in[l].astype(BF16), cos_t, sin_t, batch, seq)
        ya = _attention(qt, k, vt, lam[l:l + 1], init_rows[l:l + 1],
                        diff_subln_g[l].astype(F32).reshape(1, DA_V_DIM), batch, seq)
        kvm = _matmul(mem_b, w_mem_kv[l].astype(BF16), MEM_WIDTH, BF16)
        xf, xb, comb = _merge(ya, cv, qm, kvm, g, xf,
                              w_br_attn[l].astype(BF16), w_br_conv[l].astype(BF16),
                              w_br_mem[l].astype(BF16), w_out[l].astype(BF16),
                              conv_w[l].astype(F32),
                              ln1_g[l].astype(F32).reshape(1, d), ln1_b[l].astype(F32).reshape(1, d),
                              wr, rb, seq, mem_len, alpha)
        xf, xb = _moe(xb, xf, comb, w_exp_gate[l].astype(BF16), w_exp_up[l].astype(BF16),
                      w_exp_down[l].astype(BF16),
                      ln2_g[l].astype(F32).reshape(1, d), ln2_b[l].astype(F32).reshape(1, d), alpha)
    return xf.reshape(batch, seq, d).astype(x.dtype)
```

```python
import functools
import math

import jax
import jax.numpy as jnp
from jax import lax
from jax.experimental import pallas as pl
from jax.experimental.pallas import tpu as pltpu

F32 = jnp.float32
BF16 = jnp.bfloat16

DA_HEADS = 8
DA_HEAD_DIM = 64
DA_V_DIM = 2 * DA_HEAD_DIM
CONV_WIDTH = 512
CONV_K = 3
MEM_HEADS = 4
MEM_HEAD_DIM = 128
MEM_WIDTH = MEM_HEADS * MEM_HEAD_DIM
N_BRANCH = 3
ROPE_THETA = 10000.0
N_EXPERTS = 16
N_GROUPS = 4
GROUP_LANE = 16
EXPERTS_PER_GROUP = N_EXPERTS // N_GROUPS
LN_EPS = 1e-5
RMS_EPS = 1e-5

LANES = 128
SUBLANES = 8
VMEM_LIMIT_BYTES = 56 * 1024 * 1024

PROJ_TM = 1024
PROJ_TN = 512
ATTN_TQ = 512
MERGE_TM = 256
MOE_TM = 1024
MOE_CHUNK = 288


def _params(semantics, flags=None):
    return pltpu.CompilerParams(dimension_semantics=semantics,
                                vmem_limit_bytes=VMEM_LIMIT_BYTES, flags=flags)


def _rope_table_kernel(pos_ref, inv_ref, cos_ref, sin_ref):
    ang = pos_ref[...].astype(F32) * inv_ref[...]
    lane = lax.broadcasted_iota(jnp.int32, ang.shape, 1)
    first_half = (lane % DA_HEAD_DIM) < (DA_HEAD_DIM // 2)
    s = jnp.sin(ang)
    cos_ref[...] = jnp.cos(ang)
    sin_ref[...] = jnp.where(first_half, -s, s)


def _rope_tables(positions, tm):
    t = positions.size
    half = DA_HEAD_DIM // 2
    inv_freq = ROPE_THETA ** (-jnp.arange(0, DA_HEAD_DIM, 2, dtype=F32) / DA_HEAD_DIM)
    inv_lanes = jnp.tile(inv_freq, LANES // half).reshape(1, LANES)
    pos = positions.reshape(t, 1)
    return pl.pallas_call(
        _rope_table_kernel,
        grid=(t // tm,),
        in_specs=[pl.BlockSpec((tm, 1), lambda i: (i, 0)),
                  pl.BlockSpec((1, LANES), lambda i: (0, 0))],
        out_specs=[pl.BlockSpec((tm, LANES), lambda i: (i, 0)),
                   pl.BlockSpec((tm, LANES), lambda i: (i, 0))],
        out_shape=[jax.ShapeDtypeStruct((t, LANES), F32),
                   jax.ShapeDtypeStruct((t, LANES), F32)],
        compiler_params=_params(("parallel",)),
        name="rope_tables",
    )(pos, inv_lanes)


def _lambda_kernel(q1_ref, k1_ref, q2_ref, k2_ref, init_ref, lam_ref):
    a = jnp.sum(q1_ref[...] * k1_ref[...], axis=-1, keepdims=True)
    b = jnp.sum(q2_ref[...] * k2_ref[...], axis=-1, keepdims=True)
    lam_ref[...] = jnp.exp(a) - jnp.exp(b) + init_ref[...]


def _lambdas(lq1, lk1, lq2, lk2, lam_init):
    depth = lq1.shape[0]
    init = jnp.broadcast_to(jnp.asarray(lam_init, F32).reshape(depth, 1), (depth, LANES))
    return pl.pallas_call(
        _lambda_kernel,
        out_shape=jax.ShapeDtypeStruct((depth, LANES), F32),
        name="diff_lambda",
    )(lq1.astype(F32), lk1.astype(F32), lq2.astype(F32), lk2.astype(F32), init)


_Q_TILE0, _K_TILE0, _V_TILE0, _CV_TILE0, _QM_TILE, _G_TILE0, _N_TILES = 0, 2, 4, 6, 9, 10, 16
_Q_SCALE = (DA_HEAD_DIM ** -0.5) * math.log2(math.e)


def _rotary(acc, cos_ref, sin_ref):
    tn = acc.shape[1]
    reps = tn // LANES
    c = jnp.concatenate([cos_ref[...]] * reps, axis=1)
    s = jnp.concatenate([sin_ref[...]] * reps, axis=1)
    lane = lax.broadcasted_iota(jnp.int32, acc.shape, 1)
    half = DA_HEAD_DIM // 2
    first_half = (lane % DA_HEAD_DIM) < half
    partner = jnp.where(first_half,
                        pltpu.roll(acc, tn - half, axis=1),
                        pltpu.roll(acc, half, axis=1))
    return acc * c + partner * s


def _proj_kernel(x_ref, w_ref, cos_ref, sin_ref, qt_ref, k_ref, vt_ref, cv_ref, qm_ref, g_ref):
    j = pl.program_id(1)
    acc = jnp.dot(x_ref[...], w_ref[...], preferred_element_type=F32)

    @pl.when(j < _K_TILE0)
    def _():
        qt_ref[...] = (_rotary(acc, cos_ref, sin_ref) * _Q_SCALE).T.astype(qt_ref.dtype)

    @pl.when((j >= _K_TILE0) & (j < _V_TILE0))
    def _():
        k_ref[...] = _rotary(acc, cos_ref, sin_ref).astype(k_ref.dtype)

    @pl.when((j >= _V_TILE0) & (j < _CV_TILE0))
    def _():
        vt_ref[...] = acc.T.astype(vt_ref.dtype)

    @pl.when((j >= _CV_TILE0) & (j < _QM_TILE))
    def _():
        cv_ref[...] = acc

    @pl.when(j == _QM_TILE)
    def _():
        qm_ref[...] = acc.astype(qm_ref.dtype)

    @pl.when(j >= _G_TILE0)
    def _():
        g_ref[...] = jax.nn.sigmoid(acc)


def _projection(xb, w, cos_t, sin_t, batch, seq):
    t, d = xb.shape
    tm, tn = min(PROJ_TM, seq), PROJ_TN
    assert seq % tm == 0 and w.shape == (d, _N_TILES * tn)
    tps = seq // tm

    def clip(j, lo, n):
        return jnp.clip(j - lo, 0, n - 1)

    return pl.pallas_call(
        _proj_kernel,
        grid=(t // tm, _N_TILES),
        in_specs=[pl.BlockSpec((tm, d), lambda i, j: (i, 0)),
                  pl.BlockSpec((d, tn), lambda i, j: (0, j)),
                  pl.BlockSpec((tm, LANES), lambda i, j: (i, 0)),
                  pl.BlockSpec((tm, LANES), lambda i, j: (i, 0))],
        out_specs=[pl.BlockSpec((None, tn, tm), lambda i, j: (i // tps, clip(j, _Q_TILE0, 2), i % tps)),
                   pl.BlockSpec((tm, tn), lambda i, j: (i, clip(j, _K_TILE0, 2))),
                   pl.BlockSpec((None, tn, tm), lambda i, j: (i // tps, clip(j, _V_TILE0, 2), i % tps)),
                   pl.BlockSpec((tm, tn), lambda i, j: (i, clip(j, _CV_TILE0, 3))),
                   pl.BlockSpec((tm, tn), lambda i, j: (i, 0)),
                   pl.BlockSpec((tm, tn), lambda i, j: (i, clip(j, _G_TILE0, 6)))],
        out_shape=[jax.ShapeDtypeStruct((batch, 2 * tn, seq), BF16),
                   jax.ShapeDtypeStruct((t, 2 * tn), BF16),
                   jax.ShapeDtypeStruct((batch, 2 * tn, seq), BF16),
                   jax.ShapeDtypeStruct((t, 3 * tn), F32),
                   jax.ShapeDtypeStruct((t, tn), BF16),
                   jax.ShapeDtypeStruct((t, 6 * tn), F32)],
        compiler_params=_params(("parallel", "arbitrary")),
        name="in_projection",
    )(xb, w, cos_t, sin_t)


def _matmul_kernel(a_ref, b_ref, o_ref):
    o_ref[...] = jnp.dot(a_ref[...], b_ref[...], preferred_element_type=F32).astype(o_ref.dtype)


def _matmul(a, b, tn, out_dtype):
    m, k = a.shape
    n = b.shape[1]
    return pl.pallas_call(
        _matmul_kernel,
        grid=(n // tn,),
        in_specs=[pl.BlockSpec((m, k), lambda j: (0, 0)),
                  pl.BlockSpec((k, tn), lambda j: (0, j))],
        out_specs=pl.BlockSpec((m, tn), lambda j: (0, j)),
        out_shape=jax.ShapeDtypeStruct((m, n), out_dtype),
        compiler_params=_params(("parallel",)),
        name="mem_kv_projection",
    )(a, b)


def _sublane_all(op, x):
    for shift in (4, 2, 1):
        x = op(x, pltpu.roll(x, shift, axis=0))
    return x


def _attn_kernel(lam_ref, init_ref, g_ref, qt_ref, k_ref, vt_ref, o_ref,
                 s0_sc, s1_sc, x0_sc, x1_sc, p0_sc, p1_sc, a0_sc, a1_sc, m_sc, l_sc, acc_sc):
    s_sc, x_sc, p_sc, a_sc = (s0_sc, s1_sc), (x0_sc, x1_sc), (p0_sc, p1_sc), (a0_sc, a1_sc)
    qi = pl.program_id(2)
    dv, tq = qt_ref.shape
    tk, r = tq, 2 * tq
    qt = qt_ref[...]
    sub = lax.broadcasted_iota(jnp.int32, qt.shape, 0)
    zero = jnp.zeros_like(qt)
    qq = jnp.concatenate([jnp.where(sub < DA_HEAD_DIM, qt, zero),
                          jnp.where(sub >= DA_HEAD_DIM, qt, zero)], axis=1)
    ones = jnp.ones((2 * SUBLANES, tk), vt_ref.dtype)

    m_sc[...] = jnp.full(m_sc.shape, -jnp.inf, F32)
    l_sc[...] = jnp.zeros(l_sc.shape, F32)
    acc_sc[...] = jnp.zeros(acc_sc.shape, F32)

    def scores(kb, slot, masked=False):
        start = pl.multiple_of(kb * tk, tk)
        s = jnp.dot(k_ref[pl.ds(start, tk), :], qq, preferred_element_type=F32)
        if masked:
            key = lax.broadcasted_iota(jnp.int32, s.shape, 0)
            qry = lax.broadcasted_iota(jnp.int32, s.shape, 1) & (tq - 1)
            s = jnp.where(key <= qry, s, -jnp.inf)
        s_sc[slot][...] = s
        x_sc[slot][...] = jnp.max(s.reshape(tk // SUBLANES, SUBLANES, r), axis=0)

    def softmax(slot):
        s3 = s_sc[slot][...].reshape(tk // SUBLANES, SUBLANES, r)
        m_prev = m_sc[...]
        m_new = jnp.maximum(m_prev, _sublane_all(jnp.maximum, x_sc[slot][...]))
        p_sc[slot][...] = jnp.exp2(s3 - m_new[None]).reshape(tk, r).astype(p_sc[slot].dtype)
        a_sc[slot][...] = jnp.exp2(m_prev - m_new)
        m_sc[...] = m_new

    def pv(kb, slot):
        start = pl.multiple_of(kb * tk, tk)
        lhs = jnp.concatenate([vt_ref[:, pl.ds(start, tk)], ones], axis=0)
        res = jnp.dot(lhs, p_sc[slot][...], preferred_element_type=F32)
        alpha = a_sc[slot][...]
        acc3 = acc_sc[...].reshape(dv // SUBLANES, SUBLANES, r) * alpha[None]
        acc_sc[...] = acc3.reshape(dv, r) + res[:dv]
        l_sc[...] = alpha * l_sc[...] + res[dv:dv + SUBLANES]

    n = qi
    scores(qi, 0, masked=True)

    def odd_iter(t):
        scores(t, 0)
        pv(jnp.where(t == 1, qi, t - 2), 0)
        softmax(1)

    def even_iter(t):
        scores(t, 1)
        pv(t - 2, 1)
        softmax(0)

    @pl.when(n == 0)
    def _():
        softmax(0)
        pv(qi, 0)

    @pl.when(n >= 1)
    def _():
        scores(0, 1)
        softmax(0)

        def pair(j, carry):
            odd_iter(2 * j + 1)
            even_iter(2 * j + 2)
            return carry

        lax.fori_loop(0, (n - 1) // 2, pair, 0)

        @pl.when(n % 2 == 0)
        def _():
            odd_iter(n - 1)
            softmax(0)
            pv(n - 2, 1)
            pv(n - 1, 0)

        @pl.when(n % 2 == 1)
        def _():
            softmax(1)
            pv(jnp.where(n == 1, qi, n - 2), 0)
            pv(n - 1, 1)

    ot = (acc_sc[...].reshape(dv // SUBLANES, SUBLANES, r) / l_sc[...][None]).reshape(dv, r)
    o = ot[:, :tq] - lam_ref[0:1, 0:1] * ot[:, tq:]
    o = o * lax.rsqrt(jnp.mean(jnp.square(o), axis=0, keepdims=True) + RMS_EPS)
    o = o.T * g_ref[...] * (1.0 - init_ref[...])
    o_ref[...] = o.astype(o_ref.dtype)


def _attention(qt, k, vt, lam, lam_init, subln_g, batch, seq):
    t = k.shape[0]
    tq = min(ATTN_TQ, seq)
    assert seq % tq == 0 and tq & (tq - 1) == 0
    nq = seq // tq
    h = DA_HEADS
    row = pl.BlockSpec((1, LANES), lambda b, hh, i: (0, 0))
    return pl.pallas_call(
        _attn_kernel,
        grid=(batch, h, nq),
        in_specs=[row, row, row,
                  pl.BlockSpec((None, DA_V_DIM, tq), lambda b, hh, i: (b, hh, i)),
                  pl.BlockSpec((seq, DA_V_DIM), lambda b, hh, i: (b, hh)),
                  pl.BlockSpec((None, DA_V_DIM, seq), lambda b, hh, i: (b, hh, 0))],
        out_specs=pl.BlockSpec((tq, DA_V_DIM), lambda b, hh, i: (b * nq + i, hh)),
        out_shape=jax.ShapeDtypeStruct((t, h * DA_V_DIM), BF16),
        scratch_shapes=[pltpu.VMEM((tq, 2 * tq), F32), pltpu.VMEM((tq, 2 * tq), F32),
                        pltpu.VMEM((SUBLANES, 2 * tq), F32), pltpu.VMEM((SUBLANES, 2 * tq), F32),
                        pltpu.VMEM((tq, 2 * tq), BF16), pltpu.VMEM((tq, 2 * tq), BF16),
                        pltpu.VMEM((SUBLANES, 2 * tq), F32), pltpu.VMEM((SUBLANES, 2 * tq), F32),
                        pltpu.VMEM((SUBLANES, 2 * tq), F32),
                        pltpu.VMEM((SUBLANES, 2 * tq), F32),
                        pltpu.VMEM((DA_V_DIM, 2 * tq), F32)],
        compiler_params=_params(("parallel", "parallel", "arbitrary")),
        name="diff_attention",
    )(lam, lam_init, subln_g, qt, k, vt)


def _layer_norm(h, g, b):
    mu = jnp.mean(h, axis=-1, keepdims=True)
    d = h - mu
    var = jnp.mean(jnp.square(d), axis=-1, keepdims=True)
    return d * lax.rsqrt(var + LN_EPS) * g + b


def _route(logits):
    lane = lax.broadcasted_iota(jnp.int32, logits.shape, 1)
    valid = lane < N_EXPERTS
    logits = jnp.where(valid, logits, -jnp.inf)
    mx = jnp.max(logits, axis=-1, keepdims=True)
    ex = jnp.exp(logits - mx)
    scores = ex / jnp.sum(ex, axis=-1, keepdims=True)
    group = lane // EXPERTS_PER_GROUP
    neg = jnp.full_like(scores, -1.0)
    best = jnp.max(jnp.where(group == 0, scores, neg), axis=-1, keepdims=True)
    gidx = jnp.zeros(best.shape, jnp.int32)
    for gi in range(1, N_GROUPS):
        cand = jnp.max(jnp.where(group == gi, scores, neg), axis=-1, keepdims=True)
        take = cand > best
        gidx = jnp.where(take, gi, gidx)
        best = jnp.where(take, cand, best)
    vals = jnp.where((group == gidx) & valid, scores, neg)
    big = jnp.full_like(lane, LANES)
    v1 = jnp.max(vals, axis=-1, keepdims=True)
    i1 = jnp.min(jnp.where(vals == v1, lane, big), axis=-1, keepdims=True)
    vals2 = jnp.where(lane == i1, neg, vals)
    v2 = jnp.max(vals2, axis=-1, keepdims=True)
    i2 = jnp.min(jnp.where(vals2 == v2, lane, big), axis=-1, keepdims=True)
    tot = v1 + v2
    comb = jnp.where(lane == i1, v1 / tot, 0.0) + jnp.where(lane == i2, v2 / tot, 0.0)
    return jnp.where(lane == GROUP_LANE, gidx.astype(F32), comb)


def _merge_kernel(ya_ref, cv_ref, cvp_ref, qm_ref, kvm_ref, g_ref, x_ref,
                  wa_ref, wc_ref, wm_ref, wo_ref, cw_ref, lng_ref, lnb_ref, wrh_ref, wrl_ref, rb_ref,
                  xo_ref, xob_ref, comb_ref, *, tiles_per_seq, alpha):
    i = pl.program_id(0)
    tm = ya_ref.shape[0]
    cw = CONV_WIDTH

    z = cv_ref[:, cw:2 * cw] * cv_ref[:, 2 * cw:3 * cw]
    zp = cvp_ref[:, cw:2 * cw] * cvp_ref[:, 2 * cw:3 * cw]
    zp = jnp.where(i % tiles_per_seq == 0, jnp.zeros_like(zp), zp)
    row = lax.broadcasted_iota(jnp.int32, z.shape, 0)
    zm1 = jnp.where(row == 0, zp[SUBLANES - 1:SUBLANES], pltpu.roll(z, 1, axis=0))
    zm2 = jnp.where(row == 0, zp[SUBLANES - 2:SUBLANES - 1],
                    jnp.where(row == 1, zp[SUBLANES - 1:SUBLANES], pltpu.roll(z, 2, axis=0)))
    w = cw_ref[...]
    y = w[0:1] * zm2
    y = y + w[1:2] * zm1
    y = y + w[2:3] * z
    y_conv = (cv_ref[:, 0:cw] * y).astype(BF16)

    outs = []
    for hd in range(MEM_HEADS):
        lo = hd * MEM_HEAD_DIM
        qh = qm_ref[:, lo:lo + MEM_HEAD_DIM]
        kh = kvm_ref[:, lo:lo + MEM_HEAD_DIM]
        vh = kvm_ref[:, MEM_WIDTH + lo:MEM_WIDTH + lo + MEM_HEAD_DIM]
        s = lax.dot_general(qh, kh, (((1,), (1,)), ((), ())),
                            preferred_element_type=F32) * (MEM_HEAD_DIM ** -0.5)
        s = s - jnp.max(s, axis=-1, keepdims=True)
        e = jnp.exp(s)
        p = e / jnp.sum(e, axis=-1, keepdims=True)
        outs.append(jnp.dot(p.astype(BF16), vh, preferred_element_type=F32))
    y_mem = jnp.concatenate(outs, axis=1).astype(BF16)

    d = x_ref.shape[1]
    merged = g_ref[:, 0:d] * jnp.dot(ya_ref[...], wa_ref[...], preferred_element_type=F32)
    merged = merged + g_ref[:, d:2 * d] * jnp.dot(y_conv, wc_ref[...], preferred_element_type=F32)
    merged = merged + g_ref[:, 2 * d:3 * d] * jnp.dot(y_mem, wm_ref[...], preferred_element_type=F32)
    mix = jnp.dot(merged.astype(BF16), wo_ref[...], preferred_element_type=F32)
    x1 = _layer_norm(alpha * x_ref[...] + mix, lng_ref[...], lnb_ref[...])
    xo_ref[...] = x1
    xob_ref[...] = x1.astype(BF16)

    x1h = x1.astype(BF16)
    x1l = (x1 - x1h.astype(F32)).astype(BF16)
    logits = (jnp.dot(x1h, wrh_ref[...], preferred_element_type=F32)
              + (jnp.dot(x1h, wrl_ref[...], preferred_element_type=F32)
                 + jnp.dot(x1l, wrh_ref[...], preferred_element_type=F32))) + rb_ref[...]
    comb_ref[...] = _route(logits)


def _merge(ya, cv, qm, kvm, g, xf, wa, wc, wm, wo, conv_w, ln_g, ln_b, wrh, wrl, rb, seq, mem_len, alpha):
    t, d = xf.shape
    tm = min(MERGE_TM, seq)
    assert seq % tm == 0 and tm % SUBLANES == 0
    tiles_per_seq = seq // tm
    halo = tm // SUBLANES

    def full(a):
        return pl.BlockSpec(a.shape, lambda i: (0,) * a.ndim)

    def rows(width):
        return pl.BlockSpec((tm, width), lambda i: (i, 0))

    kern = functools.partial(_merge_kernel, tiles_per_seq=tiles_per_seq, alpha=alpha)
    return pl.pallas_call(
        kern,
        grid=(t // tm,),
        in_specs=[rows(ya.shape[1]),
                  rows(cv.shape[1]),
                  pl.BlockSpec((SUBLANES, cv.shape[1]), lambda i: (jnp.maximum(i * halo - 1, 0), 0)),
                  rows(qm.shape[1]),
                  pl.BlockSpec((mem_len, kvm.shape[1]), lambda i: (i // tiles_per_seq, 0)),
                  rows(g.shape[1]),
                  rows(d),
                  full(wa), full(wc), full(wm), full(wo), full(conv_w),
                  full(ln_g), full(ln_b), full(wrh), full(wrl), full(rb)],
        out_specs=[rows(d), rows(d), rows(LANES)],
        out_shape=[jax.ShapeDtypeStruct((t, d), F32),
                   jax.ShapeDtypeStruct((t, d), BF16),
                   jax.ShapeDtypeStruct((t, LANES), F32)],
        compiler_params=_params(("parallel",)),
        name="merge_ln_router",
    )(ya, cv, cv, qm, kvm, g, xf, wa, wc, wm, wo, conv_w, ln_g, ln_b, wrh, wrl, rb)


def _moe_kernel(xb_ref, xf_ref, comb_ref, tri_ref, wg_ref, wu_ref, wd_ref, lng_ref, lnb_ref,
                xo_ref, xob_ref,
                acc_sc, pos_sc, cnt_sc, xg_sc, wts_sc, y_sc, *, alpha, chunk):
    e = pl.program_id(1)
    grp = e // EXPERTS_PER_GROUP
    k = e % EXPERTS_PER_GROUP
    tm = xb_ref.shape[0]
    lane = lax.broadcasted_iota(jnp.int32, (tm, LANES), 1)

    @pl.when(e == 0)
    def _():
        acc_sc[...] = jnp.zeros(acc_sc.shape, F32)
        gidx = jnp.sum(jnp.where(lane == GROUP_LANE, comb_ref[...], 0.0), axis=1, keepdims=True)
        member = (lane.astype(F32) == gidx).astype(BF16)
        before = jnp.dot(tri_ref[...], member, preferred_element_type=F32)
        pos_sc[...] = jnp.sum(jnp.where(lane.astype(F32) == gidx, before, 0.0), axis=1, keepdims=True)
        cnt_sc[...] = jnp.sum(member.astype(F32), axis=0, keepdims=True)

    cnt = jnp.sum(jnp.where(lane[0:1] == grp, cnt_sc[...], 0.0)).astype(jnp.int32)
    n_chunks = (cnt + (chunk - 1)) // chunk

    def onehot(c):
        gidx = jnp.sum(jnp.where(lane == GROUP_LANE, comb_ref[...], 0.0), axis=1, keepdims=True)
        slot = jnp.where(gidx == grp.astype(F32), pos_sc[...], -1.0) - (c * chunk).astype(F32)
        col = lax.broadcasted_iota(jnp.int32, (tm, chunk), 1).astype(F32)
        return (col == slot).astype(BF16)

    def rows(c):
        return pl.ds(pl.multiple_of(c * chunk, chunk), chunk)

    @pl.when(k == 0)
    def _():
        comb = jnp.where(lane < N_EXPERTS, comb_ref[...], 0.0)
        comb_hi = comb.astype(BF16)
        comb_lo = (comb - comb_hi.astype(F32)).astype(BF16)

        def gather(c, carry):
            pt = onehot(c)
            dn = (((0,), (0,)), ((), ()))
            xg_sc[rows(c), :] = lax.dot_general(pt, xb_ref[...], dn,
                                                preferred_element_type=F32).astype(BF16)
            wts_sc[rows(c), :] = (lax.dot_general(pt, comb_hi, dn, preferred_element_type=F32)
                                  + lax.dot_general(pt, comb_lo, dn, preferred_element_type=F32))
            y_sc[rows(c), :] = jnp.zeros((chunk, y_sc.shape[1]), F32)
            return carry

        lax.fori_loop(0, n_chunks, gather, 0)

    def ffn(c, carry):
        xg = xg_sc[rows(c), :]
        hg = jnp.dot(xg, wg_ref[...], preferred_element_type=F32)
        hu = jnp.dot(xg, wu_ref[...], preferred_element_type=F32)
        h = (hg * jax.nn.sigmoid(hg)) * hu
        y = jnp.dot(h.astype(BF16), wd_ref[...], preferred_element_type=F32)
        wts = wts_sc[rows(c), :]
        wl = lax.broadcasted_iota(jnp.int32, wts.shape, 1)
        w = jnp.sum(jnp.where(wl == e, wts, 0.0), axis=1, keepdims=True)
        y_sc[rows(c), :] += jnp.where(w != 0.0, w * y, 0.0)
        return carry

    lax.fori_loop(0, n_chunks, ffn, 0)

    @pl.when(k == EXPERTS_PER_GROUP - 1)
    def _():
        def scatter(c, carry):
            acc_sc[...] += jnp.dot(onehot(c), y_sc[rows(c), :].astype(BF16),
                                   preferred_element_type=F32)
            return carry

        lax.fori_loop(0, n_chunks, scatter, 0)

    @pl.when(e == pl.num_programs(1) - 1)
    def _():
        x2 = _layer_norm(alpha * xf_ref[...] + acc_sc[...], lng_ref[...], lnb_ref[...])
        xo_ref[...] = x2
        xob_ref[...] = x2.astype(BF16)


def _moe(xb, xf, comb, wg, wu, wd, ln_g, ln_b, alpha):
    t, d = xf.shape
    tm = min(MOE_TM, t)
    assert t % tm == 0
    n_e, _, dff = wg.shape
    chunk = min(MOE_CHUNK, tm)
    cap = -(-tm // chunk) * chunk
    tri = jnp.tril(jnp.ones((tm, tm), BF16), -1)
    rows = lambda width: pl.BlockSpec((tm, width), lambda i, e: (i, 0))
    vec = pl.BlockSpec((1, d), lambda i, e: (0, 0))
    return pl.pallas_call(
        functools.partial(_moe_kernel, alpha=alpha, chunk=chunk),
        grid=(t // tm, n_e),
        in_specs=[rows(d), rows(d), rows(LANES),
                  pl.BlockSpec((tm, tm), lambda i, e: (0, 0)),
                  pl.BlockSpec((None, d, dff), lambda i, e: (e, 0, 0)),
                  pl.BlockSpec((None, d, dff), lambda i, e: (e, 0, 0)),
                  pl.BlockSpec((None, dff, d), lambda i, e: (e, 0, 0)),
                  vec, vec],
        out_specs=[rows(d), rows(d)],
        out_shape=[jax.ShapeDtypeStruct((t, d), F32),
                   jax.ShapeDtypeStruct((t, d), BF16)],
        scratch_shapes=[pltpu.VMEM((tm, d), F32),
                        pltpu.VMEM((tm, 1), F32),
                        pltpu.VMEM((1, LANES), F32),
                        pltpu.VMEM((cap, d), BF16),
                        pltpu.VMEM((cap, LANES), F32),
                        pltpu.VMEM((cap, d), F32)],
        compiler_params=_params(("parallel", "arbitrary")),
        name="moe_grouped",
    )(xb, xf, comb, tri, wg, wu, wd, ln_g, ln_b)


def kernel(x, mem, positions, w_in, lambda_q1, lambda_k1, lambda_q2, lambda_k2, diff_subln_g, conv_w, w_mem_kv, w_br_attn, w_br_conv, w_br_mem, w_out, ln1_g, ln1_b, ln2_g, ln2_b, w_router, router_bias, w_exp_gate, w_exp_up, w_exp_down):
    batch, seq, d = x.shape
    depth = w_in.shape[0]
    mem_len = mem.shape[1]
    t = batch * seq
    alpha = (2 * depth) ** 0.25

    cos_t, sin_t = _rope_tables(positions, min(PROJ_TM, t))
    lam_init = [0.8 - 0.6 * math.exp(-0.3 * l) for l in range(depth)]
    lam = _lambdas(lambda_q1, lambda_k1, lambda_q2, lambda_k2, lam_init)
    init_rows = jnp.broadcast_to(jnp.asarray(lam_init, F32).reshape(depth, 1), (depth, LANES))

    wr = jnp.zeros((d, LANES), F32).at[:, :N_EXPERTS].set(w_router.astype(F32))
    rb = jnp.zeros((1, LANES), F32).at[0, :N_EXPERTS].set(router_bias.astype(F32))
    wrh = wr.astype(BF16)
    wrl = (wr - wrh.astype(F32)).astype(BF16)
    mem_b = mem.reshape(batch * mem_len, d).astype(BF16)

    xf = x.reshape(t, d).astype(F32)
    xb = xf.astype(BF16)
    for l in range(depth):
        qt, k, vt, cv, qm, g = _projection(xb, w_in[l].astype(BF16), cos_t, sin_t, batch, seq)
        ya = _attention(qt, k, vt, lam[l:l + 1], init_rows[l:l + 1],
                        diff_subln_g[l].astype(F32).reshape(1, DA_V_DIM), batch, seq)
        kvm = _matmul(mem_b, w_mem_kv[l].astype(BF16), MEM_WIDTH, BF16)
        xf, xb, comb = _merge(ya, cv, qm, kvm, g, xf,
                              w_br_attn[l].astype(BF16), w_br_conv[l].astype(BF16),
                              w_br_mem[l].astype(BF16), w_out[l].astype(BF16),
                              conv_w[l].astype(F32),
                              ln1_g[l].astype(F32).reshape(1, d), ln1_b[l].astype(F32).reshape(1, d),
                              wrh, wrl, rb, seq, mem_len, alpha)
        xf, xb = _moe(xb, xf, comb, w_exp_gate[l].astype(BF16), w_exp_up[l].astype(BF16),
                      w_exp_down[l].astype(BF16),
                      ln2_g[l].astype(F32).reshape(1, d), ln2_b[l].astype(F32).reshape(1, d), alpha)
    return xf.reshape(batch, seq, d).astype(x.dtype)
```

```python
import functools
import math

import jax
import jax.numpy as jnp
from jax import lax
from jax.experimental import pallas as pl
from jax.experimental.pallas import tpu as pltpu

F32 = jnp.float32
BF16 = jnp.bfloat16

DA_HEADS = 8
DA_HEAD_DIM = 64
DA_V_DIM = 2 * DA_HEAD_DIM
CONV_WIDTH = 512
CONV_K = 3
MEM_HEADS = 4
MEM_HEAD_DIM = 128
MEM_WIDTH = MEM_HEADS * MEM_HEAD_DIM
N_BRANCH = 3
ROPE_THETA = 10000.0
N_EXPERTS = 16
N_GROUPS = 4
GROUP_LANE = 16
EXPERTS_PER_GROUP = N_EXPERTS // N_GROUPS
LN_EPS = 1e-5
RMS_EPS = 1e-5

LANES = 128
SUBLANES = 8
VMEM_LIMIT_BYTES = 56 * 1024 * 1024

PROJ_TM = 1024
PROJ_TN = 512
ATTN_TQ = 1024
MERGE_TM = 256
MOE_TM = 1024
MOE_CHUNK = 288


def _params(semantics, flags=None):
    return pltpu.CompilerParams(dimension_semantics=semantics,
                                vmem_limit_bytes=VMEM_LIMIT_BYTES, flags=flags)


def _rope_table_kernel(pos_ref, inv_ref, cos_ref, sin_ref):
    ang = pos_ref[...].astype(F32) * inv_ref[...]
    lane = lax.broadcasted_iota(jnp.int32, ang.shape, 1)
    first_half = (lane % DA_HEAD_DIM) < (DA_HEAD_DIM // 2)
    s = jnp.sin(ang)
    cos_ref[...] = jnp.cos(ang)
    sin_ref[...] = jnp.where(first_half, -s, s)


def _rope_tables(positions, tm):
    t = positions.size
    half = DA_HEAD_DIM // 2
    inv_freq = ROPE_THETA ** (-jnp.arange(0, DA_HEAD_DIM, 2, dtype=F32) / DA_HEAD_DIM)
    inv_lanes = jnp.tile(inv_freq, LANES // half).reshape(1, LANES)
    pos = positions.reshape(t, 1)
    return pl.pallas_call(
        _rope_table_kernel,
        grid=(t // tm,),
        in_specs=[pl.BlockSpec((tm, 1), lambda i: (i, 0)),
                  pl.BlockSpec((1, LANES), lambda i: (0, 0))],
        out_specs=[pl.BlockSpec((tm, LANES), lambda i: (i, 0)),
                   pl.BlockSpec((tm, LANES), lambda i: (i, 0))],
        out_shape=[jax.ShapeDtypeStruct((t, LANES), F32),
                   jax.ShapeDtypeStruct((t, LANES), F32)],
        compiler_params=_params(("parallel",)),
        name="rope_tables",
    )(pos, inv_lanes)


def _lambda_kernel(q1_ref, k1_ref, q2_ref, k2_ref, init_ref, lam_ref):
    a = jnp.sum(q1_ref[...] * k1_ref[...], axis=-1, keepdims=True)
    b = jnp.sum(q2_ref[...] * k2_ref[...], axis=-1, keepdims=True)
    lam_ref[...] = jnp.exp(a) - jnp.exp(b) + init_ref[...]


def _lambdas(lq1, lk1, lq2, lk2, lam_init):
    depth = lq1.shape[0]
    init = jnp.broadcast_to(jnp.asarray(lam_init, F32).reshape(depth, 1), (depth, LANES))
    return pl.pallas_call(
        _lambda_kernel,
        out_shape=jax.ShapeDtypeStruct((depth, LANES), F32),
        name="diff_lambda",
    )(lq1.astype(F32), lk1.astype(F32), lq2.astype(F32), lk2.astype(F32), init)


_Q_TILE0, _K_TILE0, _V_TILE0, _CV_TILE0, _QM_TILE, _G_TILE0, _N_TILES = 0, 2, 4, 6, 9, 10, 16
_Q_SCALE = (DA_HEAD_DIM ** -0.5) * math.log2(math.e)


def _rotary(acc, cos_ref, sin_ref):
    tn = acc.shape[1]
    reps = tn // LANES
    c = jnp.concatenate([cos_ref[...]] * reps, axis=1)
    s = jnp.concatenate([sin_ref[...]] * reps, axis=1)
    lane = lax.broadcasted_iota(jnp.int32, acc.shape, 1)
    half = DA_HEAD_DIM // 2
    first_half = (lane % DA_HEAD_DIM) < half
    partner = jnp.where(first_half,
                        pltpu.roll(acc, tn - half, axis=1),
                        pltpu.roll(acc, half, axis=1))
    return acc * c + partner * s


def _proj_segment_kernel(x_ref, w_ref, *refs, epilogue):
    *aux, o_ref = refs
    acc = jnp.dot(x_ref[...], w_ref[...], preferred_element_type=F32)
    o_ref[...] = epilogue(acc, *aux).astype(o_ref.dtype)


def _projection(xb, w, cos_t, sin_t, batch, seq):
    t, d = xb.shape
    tm, tn = min(PROJ_TM, seq), PROJ_TN
    assert seq % tm == 0 and w.shape == (d, _N_TILES * tn)
    tps = seq // tm
    tables = [cos_t, sin_t]

    def segment(name, tile0, n_tiles, epilogue, aux, out_dtype, transposed):
        if transposed:
            out_spec = pl.BlockSpec((None, tn, tm), lambda i, j: (i // tps, j, i % tps))
            out_shape = jax.ShapeDtypeStruct((batch, n_tiles * tn, seq), out_dtype)
        else:
            out_spec = pl.BlockSpec((tm, tn), lambda i, j: (i, j))
            out_shape = jax.ShapeDtypeStruct((t, n_tiles * tn), out_dtype)
        return pl.pallas_call(
            functools.partial(_proj_segment_kernel, epilogue=epilogue),
            grid=(t // tm, n_tiles),
            in_specs=[pl.BlockSpec((tm, d), lambda i, j: (i, 0)),
                      pl.BlockSpec((d, tn), lambda i, j: (0, tile0 + j))]
                     + [pl.BlockSpec((tm, LANES), lambda i, j: (i, 0)) for _ in aux],
            out_specs=out_spec,
            out_shape=out_shape,
            compiler_params=_params(("parallel", "arbitrary")),
            name=name,
        )(xb, w, *aux)

    qt = segment("proj_q", _Q_TILE0, 2,
                 lambda acc, c, s: (_rotary(acc, c, s) * _Q_SCALE).T, tables, BF16, True)
    k = segment("proj_k", _K_TILE0, 2, _rotary, tables, BF16, False)
    vt = segment("proj_v", _V_TILE0, 2, lambda acc: acc.T, [], BF16, True)
    cv = segment("proj_conv", _CV_TILE0, 3, lambda acc: acc, [], F32, False)
    qm = segment("proj_mem_q", _QM_TILE, 1, lambda acc: acc, [], BF16, False)
    g = segment("proj_gates", _G_TILE0, 6, jax.nn.sigmoid, [], F32, False)
    return qt, k, vt, cv, qm, g


def _matmul_kernel(a_ref, b_ref, o_ref):
    o_ref[...] = jnp.dot(a_ref[...], b_ref[...], preferred_element_type=F32).astype(o_ref.dtype)


def _matmul(a, b, tn, out_dtype):
    m, k = a.shape
    n = b.shape[1]
    return pl.pallas_call(
        _matmul_kernel,
        grid=(n // tn,),
        in_specs=[pl.BlockSpec((m, k), lambda j: (0, 0)),
                  pl.BlockSpec((k, tn), lambda j: (0, j))],
        out_specs=pl.BlockSpec((m, tn), lambda j: (0, j)),
        out_shape=jax.ShapeDtypeStruct((m, n), out_dtype),
        compiler_params=_params(("parallel",)),
        name="mem_kv_projection",
    )(a, b)


def _sublane_all(op, x):
    for shift in (4, 2, 1):
        x = op(x, pltpu.roll(x, shift, axis=0))
    return x


def _attn_kernel(lam_ref, init_ref, g_ref, qt_ref, k_ref, vt_ref, o_ref,
                 s0_sc, s1_sc, x0_sc, x1_sc, p0_sc, p1_sc, a0_sc, a1_sc, m_sc, l_sc, acc_sc):
    s_sc, x_sc, p_sc, a_sc = (s0_sc, s1_sc), (x0_sc, x1_sc), (p0_sc, p1_sc), (a0_sc, a1_sc)
    qi = pl.program_id(2)
    dv, tq = qt_ref.shape
    tk, r = s0_sc.shape
    assert tq == 2 * tk and r == 2 * tq
    qt = qt_ref[...]
    sub = lax.broadcasted_iota(jnp.int32, qt.shape, 0)
    zero = jnp.zeros_like(qt)
    qq = jnp.concatenate([jnp.where(sub < DA_HEAD_DIM, qt, zero),
                          jnp.where(sub >= DA_HEAD_DIM, qt, zero)], axis=1)
    ones = jnp.ones((2 * SUBLANES, tk), vt_ref.dtype)

    m_sc[...] = jnp.full(m_sc.shape, -jnp.inf, F32)
    l_sc[...] = jnp.zeros(l_sc.shape, F32)
    acc_sc[...] = jnp.zeros(acc_sc.shape, F32)

    def scores(kb, slot, diag=None):
        start = pl.multiple_of(kb * tk, tk)
        s = jnp.dot(k_ref[pl.ds(start, tk), :], qq, preferred_element_type=F32)
        if diag is not None:
            key = lax.broadcasted_iota(jnp.int32, s.shape, 0) + diag * tk
            qry = lax.broadcasted_iota(jnp.int32, s.shape, 1) & (tq - 1)
            s = jnp.where(key <= qry, s, -jnp.inf)
        s_sc[slot][...] = s
        x_sc[slot][...] = jnp.max(s.reshape(tk // SUBLANES, SUBLANES, r), axis=0)

    def softmax(slot):
        s3 = s_sc[slot][...].reshape(tk // SUBLANES, SUBLANES, r)
        m_prev = m_sc[...]
        m_new = jnp.maximum(m_prev, _sublane_all(jnp.maximum, x_sc[slot][...]))
        p_sc[slot][...] = jnp.exp2(s3 - m_new[None]).reshape(tk, r).astype(p_sc[slot].dtype)
        a_sc[slot][...] = jnp.exp2(m_prev - m_new)
        m_sc[...] = m_new

    def pv(kb, slot):
        start = pl.multiple_of(kb * tk, tk)
        lhs = jnp.concatenate([vt_ref[:, pl.ds(start, tk)], ones], axis=0)
        res = jnp.dot(lhs, p_sc[slot][...], preferred_element_type=F32)
        alpha = a_sc[slot][...]
        acc3 = acc_sc[...].reshape(dv // SUBLANES, SUBLANES, r) * alpha[None]
        acc_sc[...] = acc3.reshape(dv, r) + res[:dv]
        l_sc[...] = alpha * l_sc[...] + res[dv:dv + SUBLANES]

    def blk(t):
        return jnp.where(t < 2, 2 * qi + t, t - 2)

    scores(2 * qi, 0, diag=0)
    scores(2 * qi + 1, 1, diag=1)
    softmax(0)

    def pair(j, carry):
        t = 2 * j + 1
        scores(t - 1, 0)
        softmax(1)
        pv(blk(t - 1), 0)
        scores(t, 1)
        softmax(0)
        pv(blk(t), 1)
        return carry

    lax.fori_loop(0, qi, pair, 0)

    n = 2 * qi + 1
    softmax(1)
    pv(blk(n - 1), 0)
    pv(blk(n), 1)

    ot = (acc_sc[...].reshape(dv // SUBLANES, SUBLANES, r) / l_sc[...][None]).reshape(dv, r)
    o = ot[:, :tq] - lam_ref[0:1, 0:1] * ot[:, tq:]
    o = o * lax.rsqrt(jnp.mean(jnp.square(o), axis=0, keepdims=True) + RMS_EPS)
    o = o.T * g_ref[...] * (1.0 - init_ref[...])
    o_ref[...] = o.astype(o_ref.dtype)


def _attention(qt, k, vt, lam, lam_init, subln_g, batch, seq):
    t = k.shape[0]
    tq = min(ATTN_TQ, seq)
    tk = tq // 2
    assert seq % tq == 0 and tq & (tq - 1) == 0
    nq = seq // tq
    h = DA_HEADS
    r = 2 * tq
    row = pl.BlockSpec((1, LANES), lambda b, hh, i: (0, 0))
    return pl.pallas_call(
        _attn_kernel,
        grid=(batch, h, nq),
        in_specs=[row, row, row,
                  pl.BlockSpec((None, DA_V_DIM, tq), lambda b, hh, i: (b, hh, i)),
                  pl.BlockSpec((seq, DA_V_DIM), lambda b, hh, i: (b, hh)),
                  pl.BlockSpec((None, DA_V_DIM, seq), lambda b, hh, i: (b, hh, 0))],
        out_specs=pl.BlockSpec((tq, DA_V_DIM), lambda b, hh, i: (b * nq + i, hh)),
        out_shape=jax.ShapeDtypeStruct((t, h * DA_V_DIM), BF16),
        scratch_shapes=[pltpu.VMEM((tk, r), F32), pltpu.VMEM((tk, r), F32),
                        pltpu.VMEM((SUBLANES, r), F32), pltpu.VMEM((SUBLANES, r), F32),
                        pltpu.VMEM((tk, r), BF16), pltpu.VMEM((tk, r), BF16),
                        pltpu.VMEM((SUBLANES, r), F32), pltpu.VMEM((SUBLANES, r), F32),
                        pltpu.VMEM((SUBLANES, r), F32),
                        pltpu.VMEM((SUBLANES, r), F32),
                        pltpu.VMEM((DA_V_DIM, r), F32)],
        compiler_params=_params(("parallel", "parallel", "arbitrary")),
        name="diff_attention",
    )(lam, lam_init, subln_g, qt, k, vt)


def _layer_norm(h, g, b):
    mu = jnp.mean(h, axis=-1, keepdims=True)
    d = h - mu
    var = jnp.mean(jnp.square(d), axis=-1, keepdims=True)
    return d * lax.rsqrt(var + LN_EPS) * g + b


def _route(logits):
    lane = lax.broadcasted_iota(jnp.int32, logits.shape, 1)
    valid = lane < N_EXPERTS
    logits = jnp.where(valid, logits, -jnp.inf)
    mx = jnp.max(logits, axis=-1, keepdims=True)
    ex = jnp.exp(logits - mx)
    scores = ex / jnp.sum(ex, axis=-1, keepdims=True)
    group = lane // EXPERTS_PER_GROUP
    neg = jnp.full_like(scores, -1.0)
    best = jnp.max(jnp.where(group == 0, scores, neg), axis=-1, keepdims=True)
    gidx = jnp.zeros(best.shape, jnp.int32)
    for gi in range(1, N_GROUPS):
        cand = jnp.max(jnp.where(group == gi, scores, neg), axis=-1, keepdims=True)
        take = cand > best
        gidx = jnp.where(take, gi, gidx)
        best = jnp.where(take, cand, best)
    vals = jnp.where((group == gidx) & valid, scores, neg)
    big = jnp.full_like(lane, LANES)
    v1 = jnp.max(vals, axis=-1, keepdims=True)
    i1 = jnp.min(jnp.where(vals == v1, lane, big), axis=-1, keepdims=True)
    vals2 = jnp.where(lane == i1, neg, vals)
    v2 = jnp.max(vals2, axis=-1, keepdims=True)
    i2 = jnp.min(jnp.where(vals2 == v2, lane, big), axis=-1, keepdims=True)
    tot = v1 + v2
    comb = jnp.where(lane == i1, v1 / tot, 0.0) + jnp.where(lane == i2, v2 / tot, 0.0)
    return jnp.where(lane == GROUP_LANE, gidx.astype(F32), comb)


def _merge_kernel(ya_ref, cv_ref, cvp_ref, qm_ref, kvm_ref, g_ref, x_ref,
                  wa_ref, wc_ref, wm_ref, wo_ref, cw_ref, lng_ref, lnb_ref, wrh_ref, wrl_ref, rb_ref,
                  xo_ref, xob_ref, comb_ref, *, tiles_per_seq, alpha):
    i = pl.program_id(0)
    tm = ya_ref.shape[0]
    cw = CONV_WIDTH

    z = cv_ref[:, cw:2 * cw] * cv_ref[:, 2 * cw:3 * cw]
    zp = cvp_ref[:, cw:2 * cw] * cvp_ref[:, 2 * cw:3 * cw]
    zp = jnp.where(i % tiles_per_seq == 0, jnp.zeros_like(zp), zp)
    row = lax.broadcasted_iota(jnp.int32, z.shape, 0)
    zm1 = jnp.where(row == 0, zp[SUBLANES - 1:SUBLANES], pltpu.roll(z, 1, axis=0))
    zm2 = jnp.where(row == 0, zp[SUBLANES - 2:SUBLANES - 1],
                    jnp.where(row == 1, zp[SUBLANES - 1:SUBLANES], pltpu.roll(z, 2, axis=0)))
    w = cw_ref[...]
    y = w[0:1] * zm2
    y = y + w[1:2] * zm1
    y = y + w[2:3] * z
    y_conv = (cv_ref[:, 0:cw] * y).astype(BF16)

    outs = []
    for hd in range(MEM_HEADS):
        lo = hd * MEM_HEAD_DIM
        qh = qm_ref[:, lo:lo + MEM_HEAD_DIM]
        kh = kvm_ref[:, lo:lo + MEM_HEAD_DIM]
        vh = kvm_ref[:, MEM_WIDTH + lo:MEM_WIDTH + lo + MEM_HEAD_DIM]
        s = lax.dot_general(qh, kh, (((1,), (1,)), ((), ())),
                            preferred_element_type=F32) * (MEM_HEAD_DIM ** -0.5)
        s = s - jnp.max(s, axis=-1, keepdims=True)
        e = jnp.exp(s)
        p = e / jnp.sum(e, axis=-1, keepdims=True)
        outs.append(jnp.dot(p.astype(BF16), vh, preferred_element_type=F32))
    y_mem = jnp.concatenate(outs, axis=1).astype(BF16)

    d = x_ref.shape[1]
    merged = g_ref[:, 0:d] * jnp.dot(ya_ref[...], wa_ref[...], preferred_element_type=F32)
    merged = merged + g_ref[:, d:2 * d] * jnp.dot(y_conv, wc_ref[...], preferred_element_type=F32)
    merged = merged + g_ref[:, 2 * d:3 * d] * jnp.dot(y_mem, wm_ref[...], preferred_element_type=F32)
    mix = jnp.dot(merged.astype(BF16), wo_ref[...], preferred_element_type=F32)
    x1 = _layer_norm(alpha * x_ref[...] + mix, lng_ref[...], lnb_ref[...])
    xo_ref[...] = x1
    xob_ref[...] = x1.astype(BF16)

    x1h = x1.astype(BF16)
    x1l = (x1 - x1h.astype(F32)).astype(BF16)
    logits = (jnp.dot(x1h, wrh_ref[...], preferred_element_type=F32)
              + (jnp.dot(x1h, wrl_ref[...], preferred_element_type=F32)
                 + jnp.dot(x1l, wrh_ref[...], preferred_element_type=F32))) + rb_ref[...]
    comb_ref[...] = _route(logits)


def _merge(ya, cv, qm, kvm, g, xf, wa, wc, wm, wo, conv_w, ln_g, ln_b, wrh, wrl, rb, seq, mem_len, alpha):
    t, d = xf.shape
    tm = min(MERGE_TM, seq)
    assert seq % tm == 0 and tm % SUBLANES == 0
    tiles_per_seq = seq // tm
    halo = tm // SUBLANES

    def full(a):
        return pl.BlockSpec(a.shape, lambda i: (0,) * a.ndim)

    def rows(width):
        return pl.BlockSpec((tm, width), lambda i: (i, 0))

    kern = functools.partial(_merge_kernel, tiles_per_seq=tiles_per_seq, alpha=alpha)
    return pl.pallas_call(
        kern,
        grid=(t // tm,),
        in_specs=[rows(ya.shape[1]),
                  rows(cv.shape[1]),
                  pl.BlockSpec((SUBLANES, cv.shape[1]), lambda i: (jnp.maximum(i * halo - 1, 0), 0)),
                  rows(qm.shape[1]),
                  pl.BlockSpec((mem_len, kvm.shape[1]), lambda i: (i // tiles_per_seq, 0)),
                  rows(g.shape[1]),
                  rows(d),
                  full(wa), full(wc), full(wm), full(wo), full(conv_w),
                  full(ln_g), full(ln_b), full(wrh), full(wrl), full(rb)],
        out_specs=[rows(d), rows(d), rows(LANES)],
        out_shape=[jax.ShapeDtypeStruct((t, d), F32),
                   jax.ShapeDtypeStruct((t, d), BF16),
                   jax.ShapeDtypeStruct((t, LANES), F32)],
        compiler_params=_params(("parallel",)),
        name="merge_ln_router",
    )(ya, cv, cv, qm, kvm, g, xf, wa, wc, wm, wo, conv_w, ln_g, ln_b, wrh, wrl, rb)


def _moe_kernel(xb_ref, xf_ref, comb_ref, tri_ref, wg_ref, wu_ref, wd_ref, lng_ref, lnb_ref,
                xo_ref, xob_ref,
                acc_sc, pos_sc, cnt_sc, xg_sc, wts_sc, y_sc, *, alpha, chunk):
    e = pl.program_id(1)
    grp = e // EXPERTS_PER_GROUP
    k = e % EXPERTS_PER_GROUP
    tm = xb_ref.shape[0]
    lane = lax.broadcasted_iota(jnp.int32, (tm, LANES), 1)

    @pl.when(e == 0)
    def _():
        acc_sc[...] = jnp.zeros(acc_sc.shape, F32)
        gidx = jnp.sum(jnp.where(lane == GROUP_LANE, comb_ref[...], 0.0), axis=1, keepdims=True)
        member = (lane.astype(F32) == gidx).astype(BF16)
        before = jnp.dot(tri_ref[...], member, preferred_element_type=F32)
        pos_sc[...] = jnp.sum(jnp.where(lane.astype(F32) == gidx, before, 0.0), axis=1, keepdims=True)
        cnt_sc[...] = jnp.sum(member.astype(F32), axis=0, keepdims=True)

    cnt = jnp.sum(jnp.where(lane[0:1] == grp, cnt_sc[...], 0.0)).astype(jnp.int32)
    n_chunks = (cnt + (chunk - 1)) // chunk

    def onehot(c):
        gidx = jnp.sum(jnp.where(lane == GROUP_LANE, comb_ref[...], 0.0), axis=1, keepdims=True)
        slot = jnp.where(gidx == grp.astype(F32), pos_sc[...], -1.0) - (c * chunk).astype(F32)
        col = lax.broadcasted_iota(jnp.int32, (tm, chunk), 1).astype(F32)
        return (col == slot).astype(BF16)

    def rows(c):
        return pl.ds(pl.multiple_of(c * chunk, chunk), chunk)

    @pl.when(k == 0)
    def _():
        comb = jnp.where(lane < N_EXPERTS, comb_ref[...], 0.0)
        comb_hi = comb.astype(BF16)
        comb_lo = (comb - comb_hi.astype(F32)).astype(BF16)

        def gather(c, carry):
            pt = onehot(c)
            dn = (((0,), (0,)), ((), ()))
            xg_sc[rows(c), :] = lax.dot_general(pt, xb_ref[...], dn,
                                                preferred_element_type=F32).astype(BF16)
            wts_sc[rows(c), :] = (lax.dot_general(pt, comb_hi, dn, preferred_element_type=F32)
                                  + lax.dot_general(pt, comb_lo, dn, preferred_element_type=F32))
            y_sc[rows(c), :] = jnp.zeros((chunk, y_sc.shape[1]), F32)
            return carry

        lax.fori_loop(0, n_chunks, gather, 0)

    def ffn(c, carry):
        xg = xg_sc[rows(c), :]
        hg = jnp.dot(xg, wg_ref[...], preferred_element_type=F32)
        hu = jnp.dot(xg, wu_ref[...], preferred_element_type=F32)
        h = (hg * jax.nn.sigmoid(hg)) * hu
        y = jnp.dot(h.astype(BF16), wd_ref[...], preferred_element_type=F32)
        wts = wts_sc[rows(c), :]
        wl = lax.broadcasted_iota(jnp.int32, wts.shape, 1)
        w = jnp.sum(jnp.where(wl == e, wts, 0.0), axis=1, keepdims=True)
        y_sc[rows(c), :] += jnp.where(w != 0.0, w * y, 0.0)
        return carry

    lax.fori_loop(0, n_chunks, ffn, 0)

    @pl.when(k == EXPERTS_PER_GROUP - 1)
    def _():
        def scatter(c, carry):
            acc_sc[...] += jnp.dot(onehot(c), y_sc[rows(c), :].astype(BF16),
                                   preferred_element_type=F32)
            return carry

        lax.fori_loop(0, n_chunks, scatter, 0)

    @pl.when(e == pl.num_programs(1) - 1)
    def _():
        x2 = _layer_norm(alpha * xf_ref[...] + acc_sc[...], lng_ref[...], lnb_ref[...])
        xo_ref[...] = x2
        xob_ref[...] = x2.astype(BF16)


def _moe(xb, xf, comb, wg, wu, wd, ln_g, ln_b, alpha):
    t, d = xf.shape
    tm = min(MOE_TM, t)
    assert t % tm == 0
    n_e, _, dff = wg.shape
    chunk = min(MOE_CHUNK, tm)
    cap = -(-tm // chunk) * chunk
    tri = jnp.tril(jnp.ones((tm, tm), BF16), -1)
    rows = lambda width: pl.BlockSpec((tm, width), lambda i, e: (i, 0))
    vec = pl.BlockSpec((1, d), lambda i, e: (0, 0))
    return pl.pallas_call(
        functools.partial(_moe_kernel, alpha=alpha, chunk=chunk),
        grid=(t // tm, n_e),
        in_specs=[rows(d), rows(d), rows(LANES),
                  pl.BlockSpec((tm, tm), lambda i, e: (0, 0)),
                  pl.BlockSpec((None, d, dff), lambda i, e: (e, 0, 0)),
                  pl.BlockSpec((None, d, dff), lambda i, e: (e, 0, 0)),
                  pl.BlockSpec((None, dff, d), lambda i, e: (e, 0, 0)),
                  vec, vec],
        out_specs=[rows(d), rows(d)],
        out_shape=[jax.ShapeDtypeStruct((t, d), F32),
                   jax.ShapeDtypeStruct((t, d), BF16)],
        scratch_shapes=[pltpu.VMEM((tm, d), F32),
                        pltpu.VMEM((tm, 1), F32),
                        pltpu.VMEM((1, LANES), F32),
                        pltpu.VMEM((cap, d), BF16),
                        pltpu.VMEM((cap, LANES), F32),
                        pltpu.VMEM((cap, d), F32)],
        compiler_params=_params(("parallel", "arbitrary")),
        name="moe_grouped",
    )(xb, xf, comb, tri, wg, wu, wd, ln_g, ln_b)


def kernel(x, mem, positions, w_in, lambda_q1, lambda_k1, lambda_q2, lambda_k2, diff_subln_g, conv_w, w_mem_kv, w_br_attn, w_br_conv, w_br_mem, w_out, ln1_g, ln1_b, ln2_g, ln2_b, w_router, router_bias, w_exp_gate, w_exp_up, w_exp_down):
    batch, seq, d = x.shape
    depth = w_in.shape[0]
    mem_len = mem.shape[1]
    t = batch * seq
    alpha = (2 * depth) ** 0.25

    cos_t, sin_t = _rope_tables(positions, min(PROJ_TM, t))
    lam_init = [0.8 - 0.6 * math.exp(-0.3 * l) for l in range(depth)]
    lam = _lambdas(lambda_q1, lambda_k1, lambda_q2, lambda_k2, lam_init)
    init_rows = jnp.broadcast_to(jnp.asarray(lam_init, F32).reshape(depth, 1), (depth, LANES))

    wr = jnp.zeros((d, LANES), F32).at[:, :N_EXPERTS].set(w_router.astype(F32))
    rb = jnp.zeros((1, LANES), F32).at[0, :N_EXPERTS].set(router_bias.astype(F32))
    wrh = wr.astype(BF16)
    wrl = (wr - wrh.astype(F32)).astype(BF16)
    mem_b = mem.reshape(batch * mem_len, d).astype(BF16)

    xf = x.reshape(t, d).astype(F32)
    xb = xf.astype(BF16)
    for l in range(depth):
        qt, k, vt, cv, qm, g = _projection(xb, w_in[l].astype(BF16), cos_t, sin_t, batch, seq)
        ya = _attention(qt, k, vt, lam[l:l + 1], init_rows[l:l + 1],
                        diff_subln_g[l].astype(F32).reshape(1, DA_V_DIM), batch, seq)
        kvm = _matmul(mem_b, w_mem_kv[l].astype(BF16), MEM_WIDTH, BF16)
        xf, xb, comb = _merge(ya, cv, qm, kvm, g, xf,
                              w_br_attn[l].astype(BF16), w_br_conv[l].astype(BF16),
                              w_br_mem[l].astype(BF16), w_out[l].astype(BF16),
                              conv_w[l].astype(F32),
                              ln1_g[l].astype(F32).reshape(1, d), ln1_b[l].astype(F32).reshape(1, d),
                              wrh, wrl, rb, seq, mem_len, alpha)
        xf, xb = _moe(xb, xf, comb, w_exp_gate[l].astype(BF16), w_exp_up[l].astype(BF16),
                      w_exp_down[l].astype(BF16),
                      ln2_g[l].astype(F32).reshape(1, d), ln2_b[l].astype(F32).reshape(1, d), alpha)
    return xf.reshape(batch, seq, d).astype(x.dtype)
```

```python
import functools
import math

import jax
import jax.numpy as jnp
from jax import lax
from jax.experimental import pallas as pl
from jax.experimental.pallas import tpu as pltpu

F32 = jnp.float32
BF16 = jnp.bfloat16

DA_HEADS = 8
DA_HEAD_DIM = 64
DA_V_DIM = 2 * DA_HEAD_DIM
CONV_WIDTH = 512
CONV_K = 3
MEM_HEADS = 4
MEM_HEAD_DIM = 128
MEM_WIDTH = MEM_HEADS * MEM_HEAD_DIM
N_BRANCH = 3
ROPE_THETA = 10000.0
N_EXPERTS = 16
N_GROUPS = 4
GROUP_LANE = 16
EXPERTS_PER_GROUP = N_EXPERTS // N_GROUPS
LN_EPS = 1e-5
RMS_EPS = 1e-5

LANES = 128
SUBLANES = 8
VMEM_LIMIT_BYTES = 56 * 1024 * 1024

PROJ_TM = 2048
PROJ_TN = 512
ATTN_TQ = 1024
MERGE_TM = 512
MOE_TM = 1024
MOE_CHUNK = 288


def _params(semantics, flags=None):
    return pltpu.CompilerParams(dimension_semantics=semantics,
                                vmem_limit_bytes=VMEM_LIMIT_BYTES, flags=flags)


def _rope_table_kernel(pos_ref, inv_ref, cos_ref, sin_ref):
    ang = pos_ref[...].astype(F32) * inv_ref[...]
    lane = lax.broadcasted_iota(jnp.int32, ang.shape, 1)
    first_half = (lane % DA_HEAD_DIM) < (DA_HEAD_DIM // 2)
    s = jnp.sin(ang)
    cos_ref[...] = jnp.cos(ang)
    sin_ref[...] = jnp.where(first_half, -s, s)


def _rope_tables(positions, tm):
    t = positions.size
    half = DA_HEAD_DIM // 2
    inv_freq = ROPE_THETA ** (-jnp.arange(0, DA_HEAD_DIM, 2, dtype=F32) / DA_HEAD_DIM)
    inv_lanes = jnp.tile(inv_freq, LANES // half).reshape(1, LANES)
    pos = positions.reshape(t, 1)
    return pl.pallas_call(
        _rope_table_kernel,
        grid=(t // tm,),
        in_specs=[pl.BlockSpec((tm, 1), lambda i: (i, 0)),
                  pl.BlockSpec((1, LANES), lambda i: (0, 0))],
        out_specs=[pl.BlockSpec((tm, LANES), lambda i: (i, 0)),
                   pl.BlockSpec((tm, LANES), lambda i: (i, 0))],
        out_shape=[jax.ShapeDtypeStruct((t, LANES), F32),
                   jax.ShapeDtypeStruct((t, LANES), F32)],
        compiler_params=_params(("parallel",)),
        name="rope_tables",
    )(pos, inv_lanes)


def _lambda_kernel(q1_ref, k1_ref, q2_ref, k2_ref, init_ref, lam_ref):
    a = jnp.sum(q1_ref[...] * k1_ref[...], axis=-1, keepdims=True)
    b = jnp.sum(q2_ref[...] * k2_ref[...], axis=-1, keepdims=True)
    lam_ref[...] = jnp.exp(a) - jnp.exp(b) + init_ref[...]


def _lambdas(lq1, lk1, lq2, lk2, lam_init):
    depth = lq1.shape[0]
    init = jnp.broadcast_to(jnp.asarray(lam_init, F32).reshape(depth, 1), (depth, LANES))
    return pl.pallas_call(
        _lambda_kernel,
        out_shape=jax.ShapeDtypeStruct((depth, LANES), F32),
        name="diff_lambda",
    )(lq1.astype(F32), lk1.astype(F32), lq2.astype(F32), lk2.astype(F32), init)


_Q_TILE0, _K_TILE0, _V_TILE0, _CV_TILE0, _QM_TILE, _G_TILE0, _N_TILES = 0, 2, 4, 6, 9, 10, 16
_Q_SCALE = (DA_HEAD_DIM ** -0.5) * math.log2(math.e)


def _rotary(acc, cos_ref, sin_ref):
    tn = acc.shape[1]
    reps = tn // LANES
    c = jnp.concatenate([cos_ref[...]] * reps, axis=1)
    s = jnp.concatenate([sin_ref[...]] * reps, axis=1)
    lane = lax.broadcasted_iota(jnp.int32, acc.shape, 1)
    half = DA_HEAD_DIM // 2
    first_half = (lane % DA_HEAD_DIM) < half
    partner = jnp.where(first_half,
                        pltpu.roll(acc, tn - half, axis=1),
                        pltpu.roll(acc, half, axis=1))
    return acc * c + partner * s


def _proj_segment_kernel(x_ref, w_ref, *refs, epilogue):
    *aux, o_ref = refs
    acc = jnp.dot(x_ref[...], w_ref[...], preferred_element_type=F32)
    o_ref[...] = epilogue(acc, *aux).astype(o_ref.dtype)


def _projection(xb, w, cos_t, sin_t, batch, seq):
    t, d = xb.shape
    tm, tn = min(PROJ_TM, seq), PROJ_TN
    assert seq % tm == 0 and w.shape == (d, _N_TILES * tn)
    tps = seq // tm
    tables = [cos_t, sin_t]

    def segment(name, tile0, n_tiles, epilogue, aux, out_dtype, transposed):
        if transposed:
            out_spec = pl.BlockSpec((None, tn, tm), lambda i, j: (i // tps, j, i % tps))
            out_shape = jax.ShapeDtypeStruct((batch, n_tiles * tn, seq), out_dtype)
        else:
            out_spec = pl.BlockSpec((tm, tn), lambda i, j: (i, j))
            out_shape = jax.ShapeDtypeStruct((t, n_tiles * tn), out_dtype)
        return pl.pallas_call(
            functools.partial(_proj_segment_kernel, epilogue=epilogue),
            grid=(t // tm, n_tiles),
            in_specs=[pl.BlockSpec((tm, d), lambda i, j: (i, 0)),
                      pl.BlockSpec((d, tn), lambda i, j: (0, tile0 + j))]
                     + [pl.BlockSpec((tm, LANES), lambda i, j: (i, 0)) for _ in aux],
            out_specs=out_spec,
            out_shape=out_shape,
            compiler_params=_params(("parallel", "arbitrary")),
            name=name,
        )(xb, w, *aux)

    qt = segment("proj_q", _Q_TILE0, 2,
                 lambda acc, c, s: (_rotary(acc, c, s) * _Q_SCALE).T, tables, BF16, True)
    k = segment("proj_k", _K_TILE0, 2, _rotary, tables, BF16, False)
    vt = segment("proj_v", _V_TILE0, 2, lambda acc: acc.T, [], BF16, True)
    cv = segment("proj_conv", _CV_TILE0, 3, lambda acc: acc, [], F32, False)
    qm = segment("proj_mem_q", _QM_TILE, 1, lambda acc: acc, [], BF16, False)
    g = segment("proj_gates", _G_TILE0, 6, jax.nn.sigmoid, [], BF16, False)
    return qt, k, vt, cv, qm, g


def _matmul_kernel(a_ref, b_ref, o_ref):
    o_ref[...] = jnp.dot(a_ref[...], b_ref[...], preferred_element_type=F32).astype(o_ref.dtype)


def _matmul(a, b, tn, out_dtype):
    m, k = a.shape
    n = b.shape[1]
    return pl.pallas_call(
        _matmul_kernel,
        grid=(n // tn,),
        in_specs=[pl.BlockSpec((m, k), lambda j: (0, 0)),
                  pl.BlockSpec((k, tn), lambda j: (0, j))],
        out_specs=pl.BlockSpec((m, tn), lambda j: (0, j)),
        out_shape=jax.ShapeDtypeStruct((m, n), out_dtype),
        compiler_params=_params(("parallel",)),
        name="mem_kv_projection",
    )(a, b)


def _sublane_all(op, x):
    for shift in (4, 2, 1):
        x = op(x, pltpu.roll(x, shift, axis=0))
    return x


def _attn_kernel(lam_ref, init_ref, g_ref, qt_ref, k_ref, vt_ref, o_ref,
                 s0_sc, s1_sc, x0_sc, x1_sc, p0_sc, p1_sc, a0_sc, a1_sc, m_sc, l_sc, acc_sc):
    s_sc, x_sc, p_sc, a_sc = (s0_sc, s1_sc), (x0_sc, x1_sc), (p0_sc, p1_sc), (a0_sc, a1_sc)
    qi = pl.program_id(2)
    dv, tq = qt_ref.shape
    tk, r = s0_sc.shape
    assert tq == 2 * tk and r == 2 * tq
    qt = qt_ref[...]
    sub = lax.broadcasted_iota(jnp.int32, qt.shape, 0)
    zero = jnp.zeros_like(qt)
    qq = jnp.concatenate([jnp.where(sub < DA_HEAD_DIM, qt, zero),
                          jnp.where(sub >= DA_HEAD_DIM, qt, zero)], axis=1)
    ones = jnp.ones((2 * SUBLANES, tk), vt_ref.dtype)

    m_sc[...] = jnp.full(m_sc.shape, -jnp.inf, F32)
    l_sc[...] = jnp.zeros(l_sc.shape, F32)
    acc_sc[...] = jnp.zeros(acc_sc.shape, F32)

    def scores(kb, slot, diag=None):
        start = pl.multiple_of(kb * tk, tk)
        s = jnp.dot(k_ref[pl.ds(start, tk), :], qq, preferred_element_type=F32)
        if diag is not None:
            key = lax.broadcasted_iota(jnp.int32, s.shape, 0) + diag * tk
            qry = lax.broadcasted_iota(jnp.int32, s.shape, 1) & (tq - 1)
            s = jnp.where(key <= qry, s, -jnp.inf)
        s_sc[slot][...] = s
        x_sc[slot][...] = jnp.max(s.reshape(tk // SUBLANES, SUBLANES, r), axis=0)

    def softmax(slot):
        s3 = s_sc[slot][...].reshape(tk // SUBLANES, SUBLANES, r)
        m_prev = m_sc[...]
        m_new = jnp.maximum(m_prev, _sublane_all(jnp.maximum, x_sc[slot][...]))
        p_sc[slot][...] = jnp.exp2(s3 - m_new[None]).reshape(tk, r).astype(p_sc[slot].dtype)
        a_sc[slot][...] = jnp.exp2(m_prev - m_new)
        m_sc[...] = m_new

    def pv(kb, slot):
        start = pl.multiple_of(kb * tk, tk)
        lhs = jnp.concatenate([vt_ref[:, pl.ds(start, tk)], ones], axis=0)
        res = jnp.dot(lhs, p_sc[slot][...], preferred_element_type=F32)
        alpha = a_sc[slot][...]
        acc3 = acc_sc[...].reshape(dv // SUBLANES, SUBLANES, r) * alpha[None]
        acc_sc[...] = acc3.reshape(dv, r) + res[:dv]
        l_sc[...] = alpha * l_sc[...] + res[dv:dv + SUBLANES]

    def blk(t):
        return jnp.where(t < 2, 2 * qi + t, t - 2)

    scores(2 * qi, 0, diag=0)
    scores(2 * qi + 1, 1, diag=1)
    softmax(0)

    def pair(j, carry):
        t = 2 * j + 1
        scores(t - 1, 0)
        softmax(1)
        pv(blk(t - 1), 0)
        scores(t, 1)
        softmax(0)
        pv(blk(t), 1)
        return carry

    lax.fori_loop(0, qi, pair, 0)

    n = 2 * qi + 1
    softmax(1)
    pv(blk(n - 1), 0)
    pv(blk(n), 1)

    ot = (acc_sc[...].reshape(dv // SUBLANES, SUBLANES, r) / l_sc[...][None]).reshape(dv, r)
    o = ot[:, :tq] - lam_ref[0:1, 0:1] * ot[:, tq:]
    o = o * lax.rsqrt(jnp.mean(jnp.square(o), axis=0, keepdims=True) + RMS_EPS)
    o = o.T * g_ref[...] * (1.0 - init_ref[...])
    o_ref[...] = o.astype(o_ref.dtype)


def _attention(qt, k, vt, lam, lam_init, subln_g, batch, seq):
    t = k.shape[0]
    tq = min(ATTN_TQ, seq)
    tk = tq // 2
    assert seq % tq == 0 and tq & (tq - 1) == 0
    nq = seq // tq
    h = DA_HEADS
    r = 2 * tq
    row = pl.BlockSpec((1, LANES), lambda b, hh, i: (0, 0))
    return pl.pallas_call(
        _attn_kernel,
        grid=(batch, h, nq),
        in_specs=[row, row, row,
                  pl.BlockSpec((None, DA_V_DIM, tq), lambda b, hh, i: (b, hh, i)),
                  pl.BlockSpec((seq, DA_V_DIM), lambda b, hh, i: (b, hh)),
                  pl.BlockSpec((None, DA_V_DIM, seq), lambda b, hh, i: (b, hh, 0))],
        out_specs=pl.BlockSpec((tq, DA_V_DIM), lambda b, hh, i: (b * nq + i, hh)),
        out_shape=jax.ShapeDtypeStruct((t, h * DA_V_DIM), BF16),
        scratch_shapes=[pltpu.VMEM((tk, r), F32), pltpu.VMEM((tk, r), F32),
                        pltpu.VMEM((SUBLANES, r), F32), pltpu.VMEM((SUBLANES, r), F32),
                        pltpu.VMEM((tk, r), BF16), pltpu.VMEM((tk, r), BF16),
                        pltpu.VMEM((SUBLANES, r), F32), pltpu.VMEM((SUBLANES, r), F32),
                        pltpu.VMEM((SUBLANES, r), F32),
                        pltpu.VMEM((SUBLANES, r), F32),
                        pltpu.VMEM((DA_V_DIM, r), F32)],
        compiler_params=_params(("parallel", "parallel", "arbitrary")),
        name="diff_attention",
    )(lam, lam_init, subln_g, qt, k, vt)


def _layer_norm(h, g, b):
    mu = jnp.mean(h, axis=-1, keepdims=True)
    d = h - mu
    var = jnp.mean(jnp.square(d), axis=-1, keepdims=True)
    return d * lax.rsqrt(var + LN_EPS) * g + b


def _route(logits):
    lane = lax.broadcasted_iota(jnp.int32, logits.shape, 1)
    valid = lane < N_EXPERTS
    logits = jnp.where(valid, logits, -jnp.inf)
    mx = jnp.max(logits, axis=-1, keepdims=True)
    ex = jnp.exp(logits - mx)
    scores = ex / jnp.sum(ex, axis=-1, keepdims=True)
    group = lane // EXPERTS_PER_GROUP
    neg = jnp.full_like(scores, -1.0)
    best = jnp.max(jnp.where(group == 0, scores, neg), axis=-1, keepdims=True)
    gidx = jnp.zeros(best.shape, jnp.int32)
    for gi in range(1, N_GROUPS):
        cand = jnp.max(jnp.where(group == gi, scores, neg), axis=-1, keepdims=True)
        take = cand > best
        gidx = jnp.where(take, gi, gidx)
        best = jnp.where(take, cand, best)
    vals = jnp.where((group == gidx) & valid, scores, neg)
    big = jnp.full_like(lane, LANES)
    v1 = jnp.max(vals, axis=-1, keepdims=True)
    i1 = jnp.min(jnp.where(vals == v1, lane, big), axis=-1, keepdims=True)
    vals2 = jnp.where(lane == i1, neg, vals)
    v2 = jnp.max(vals2, axis=-1, keepdims=True)
    i2 = jnp.min(jnp.where(vals2 == v2, lane, big), axis=-1, keepdims=True)
    tot = v1 + v2
    comb = jnp.where(lane == i1, v1 / tot, 0.0) + jnp.where(lane == i2, v2 / tot, 0.0)
    return jnp.where(lane == GROUP_LANE, gidx.astype(F32), comb)


def _merge_kernel(ya_ref, cv_ref, cvp_ref, qm_ref, kvm_ref, g_ref, x_ref,
                  wa_ref, wc_ref, wm_ref, wo_ref, cw_ref, lng_ref, lnb_ref, wrh_ref, wrl_ref, rb_ref,
                  xo_ref, xob_ref, comb_ref, *, tiles_per_seq, alpha):
    i = pl.program_id(0)
    tm = ya_ref.shape[0]
    cw = CONV_WIDTH

    z = cv_ref[:, cw:2 * cw] * cv_ref[:, 2 * cw:3 * cw]
    zp = cvp_ref[:, cw:2 * cw] * cvp_ref[:, 2 * cw:3 * cw]
    zp = jnp.where(i % tiles_per_seq == 0, jnp.zeros_like(zp), zp)
    row = lax.broadcasted_iota(jnp.int32, z.shape, 0)
    zm1 = jnp.where(row == 0, zp[SUBLANES - 1:SUBLANES], pltpu.roll(z, 1, axis=0))
    zm2 = jnp.where(row == 0, zp[SUBLANES - 2:SUBLANES - 1],
                    jnp.where(row == 1, zp[SUBLANES - 1:SUBLANES], pltpu.roll(z, 2, axis=0)))
    w = cw_ref[...]
    y = w[0:1] * zm2
    y = y + w[1:2] * zm1
    y = y + w[2:3] * z
    y_conv = (cv_ref[:, 0:cw] * y).astype(BF16)

    outs = []
    for hd in range(MEM_HEADS):
        lo = hd * MEM_HEAD_DIM
        qh = qm_ref[:, lo:lo + MEM_HEAD_DIM]
        kh = kvm_ref[:, lo:lo + MEM_HEAD_DIM]
        vh = kvm_ref[:, MEM_WIDTH + lo:MEM_WIDTH + lo + MEM_HEAD_DIM]
        s = lax.dot_general(qh, kh, (((1,), (1,)), ((), ())),
                            preferred_element_type=F32) * (MEM_HEAD_DIM ** -0.5)
        s = s - jnp.max(s, axis=-1, keepdims=True)
        e = jnp.exp(s)
        p = e / jnp.sum(e, axis=-1, keepdims=True)
        outs.append(jnp.dot(p.astype(BF16), vh, preferred_element_type=F32))
    y_mem = jnp.concatenate(outs, axis=1).astype(BF16)

    d = x_ref.shape[1]
    merged = g_ref[:, 0:d] * jnp.dot(ya_ref[...], wa_ref[...], preferred_element_type=F32)
    merged = merged + g_ref[:, d:2 * d] * jnp.dot(y_conv, wc_ref[...], preferred_element_type=F32)
    merged = merged + g_ref[:, 2 * d:3 * d] * jnp.dot(y_mem, wm_ref[...], preferred_element_type=F32)
    mix = jnp.dot(merged.astype(BF16), wo_ref[...], preferred_element_type=F32)
    x1 = _layer_norm(alpha * x_ref[...] + mix, lng_ref[...], lnb_ref[...])
    xo_ref[...] = x1
    xob_ref[...] = x1.astype(BF16)

    x1h = x1.astype(BF16)
    x1l = (x1 - x1h.astype(F32)).astype(BF16)
    logits = (jnp.dot(x1h, wrh_ref[...], preferred_element_type=F32)
              + (jnp.dot(x1h, wrl_ref[...], preferred_element_type=F32)
                 + jnp.dot(x1l, wrh_ref[...], preferred_element_type=F32))) + rb_ref[...]
    comb_ref[...] = _route(logits)


def _merge(ya, cv, qm, kvm, g, xf, wa, wc, wm, wo, conv_w, ln_g, ln_b, wrh, wrl, rb, seq, mem_len, alpha):
    t, d = xf.shape
    tm = min(MERGE_TM, seq)
    assert seq % tm == 0 and tm % SUBLANES == 0
    tiles_per_seq = seq // tm
    halo = tm // SUBLANES

    def full(a):
        return pl.BlockSpec(a.shape, lambda i: (0,) * a.ndim)

    def rows(width):
        return pl.BlockSpec((tm, width), lambda i: (i, 0))

    kern = functools.partial(_merge_kernel, tiles_per_seq=tiles_per_seq, alpha=alpha)
    return pl.pallas_call(
        kern,
        grid=(t // tm,),
        in_specs=[rows(ya.shape[1]),
                  rows(cv.shape[1]),
                  pl.BlockSpec((SUBLANES, cv.shape[1]), lambda i: (jnp.maximum(i * halo - 1, 0), 0)),
                  rows(qm.shape[1]),
                  pl.BlockSpec((mem_len, kvm.shape[1]), lambda i: (i // tiles_per_seq, 0)),
                  rows(g.shape[1]),
                  rows(d),
                  full(wa), full(wc), full(wm), full(wo), full(conv_w),
                  full(ln_g), full(ln_b), full(wrh), full(wrl), full(rb)],
        out_specs=[rows(d), rows(d), rows(LANES)],
        out_shape=[jax.ShapeDtypeStruct((t, d), F32),
                   jax.ShapeDtypeStruct((t, d), BF16),
                   jax.ShapeDtypeStruct((t, LANES), F32)],
        compiler_params=_params(("parallel",)),
        name="merge_ln_router",
    )(ya, cv, cv, qm, kvm, g, xf, wa, wc, wm, wo, conv_w, ln_g, ln_b, wrh, wrl, rb)


def _moe_kernel(xb_ref, xf_ref, comb_ref, tri_ref, wg_ref, wu_ref, wd_ref, lng_ref, lnb_ref,
                xo_ref, xob_ref,
                acc_sc, pos_sc, posrow_sc, grow_sc, cnt_sc, xg_sc, wts_sc, y_sc, *, alpha, chunk):
    e = pl.program_id(1)
    grp = e // EXPERTS_PER_GROUP
    k = e % EXPERTS_PER_GROUP
    tm = xb_ref.shape[0]
    lane = lax.broadcasted_iota(jnp.int32, (tm, LANES), 1)

    @pl.when(e == 0)
    def _():
        acc_sc[...] = jnp.zeros(acc_sc.shape, F32)
        gidx = jnp.sum(jnp.where(lane == GROUP_LANE, comb_ref[...], 0.0), axis=1, keepdims=True)
        member = (lane.astype(F32) == gidx).astype(BF16)
        before = jnp.dot(tri_ref[...], member, preferred_element_type=F32)
        pos_sc[...] = jnp.sum(jnp.where(lane.astype(F32) == gidx, before, 0.0), axis=1, keepdims=True)
        cnt_sc[...] = jnp.sum(member.astype(F32), axis=0, keepdims=True)
        grow = comb_ref[...].T[GROUP_LANE:GROUP_LANE + 1]
        sub = lax.broadcasted_iota(jnp.int32, (LANES, tm), 0).astype(F32)
        posrow_sc[...] = jnp.sum(jnp.where(sub == grow, before.T, 0.0), axis=0, keepdims=True)
        grow_sc[...] = grow

    cnt = jnp.sum(jnp.where(lane[0:1] == grp, cnt_sc[...], 0.0)).astype(jnp.int32)
    n_chunks = (cnt + (chunk - 1)) // chunk

    def onehot(c):
        gidx = jnp.sum(jnp.where(lane == GROUP_LANE, comb_ref[...], 0.0), axis=1, keepdims=True)
        slot = jnp.where(gidx == grp.astype(F32), pos_sc[...], -1.0) - (c * chunk).astype(F32)
        col = lax.broadcasted_iota(jnp.int32, (tm, chunk), 1).astype(F32)
        return (col == slot).astype(BF16)

    def onehot_rows(c):
        slot = (jnp.where(grow_sc[...] == grp.astype(F32), posrow_sc[...], -1.0)
                - (c * chunk).astype(F32))
        row = lax.broadcasted_iota(jnp.int32, (chunk, tm), 0).astype(F32)
        return (row == slot).astype(BF16)

    def rows(c):
        return pl.ds(pl.multiple_of(c * chunk, chunk), chunk)

    @pl.when(k == 0)
    def _():
        comb = jnp.where(lane < N_EXPERTS, comb_ref[...], 0.0)
        comb_hi = comb.astype(BF16)
        comb_lo = (comb - comb_hi.astype(F32)).astype(BF16)

        def gather(c, carry):
            sel = onehot_rows(c)
            xg_sc[rows(c), :] = jnp.dot(sel, xb_ref[...], preferred_element_type=F32).astype(BF16)
            wts_sc[rows(c), :] = (jnp.dot(sel, comb_hi, preferred_element_type=F32)
                                  + jnp.dot(sel, comb_lo, preferred_element_type=F32))
            y_sc[rows(c), :] = jnp.zeros((chunk, y_sc.shape[1]), F32)
            return carry

        lax.fori_loop(0, n_chunks, gather, 0)

    def ffn(c, carry):
        xg = xg_sc[rows(c), :]
        hg = jnp.dot(xg, wg_ref[...], preferred_element_type=F32)
        hu = jnp.dot(xg, wu_ref[...], preferred_element_type=F32)
        h = (hg * jax.nn.sigmoid(hg)) * hu
        y = jnp.dot(h.astype(BF16), wd_ref[...], preferred_element_type=F32)
        wts = wts_sc[rows(c), :]
        wl = lax.broadcasted_iota(jnp.int32, wts.shape, 1)
        w = jnp.sum(jnp.where(wl == e, wts, 0.0), axis=1, keepdims=True)
        y_sc[rows(c), :] += jnp.where(w != 0.0, w * y, 0.0)
        return carry

    lax.fori_loop(0, n_chunks, ffn, 0)

    @pl.when(k == EXPERTS_PER_GROUP - 1)
    def _():
        def scatter(c, carry):
            acc_sc[...] += jnp.dot(onehot(c), y_sc[rows(c), :].astype(BF16),
                                   preferred_element_type=F32)
            return carry

        lax.fori_loop(0, n_chunks, scatter, 0)

    @pl.when(e == pl.num_programs(1) - 1)
    def _():
        x2 = _layer_norm(alpha * xf_ref[...] + acc_sc[...], lng_ref[...], lnb_ref[...])
        xo_ref[...] = x2
        xob_ref[...] = x2.astype(BF16)


def _moe(xb, xf, comb, wg, wu, wd, ln_g, ln_b, alpha):
    t, d = xf.shape
    tm = min(MOE_TM, t)
    assert t % tm == 0
    n_e, _, dff = wg.shape
    chunk = min(MOE_CHUNK, tm)
    cap = -(-tm // chunk) * chunk
    tri = jnp.tril(jnp.ones((tm, tm), BF16), -1)
    rows = lambda width: pl.BlockSpec((tm, width), lambda i, e: (i, 0))
    vec = pl.BlockSpec((1, d), lambda i, e: (0, 0))
    return pl.pallas_call(
        functools.partial(_moe_kernel, alpha=alpha, chunk=chunk),
        grid=(t // tm, n_e),
        in_specs=[rows(d), rows(d), rows(LANES),
                  pl.BlockSpec((tm, tm), lambda i, e: (0, 0)),
                  pl.BlockSpec((None, d, dff), lambda i, e: (e, 0, 0)),
                  pl.BlockSpec((None, d, dff), lambda i, e: (e, 0, 0)),
                  pl.BlockSpec((None, dff, d), lambda i, e: (e, 0, 0)),
                  vec, vec],
        out_specs=[rows(d), rows(d)],
        out_shape=[jax.ShapeDtypeStruct((t, d), F32),
                   jax.ShapeDtypeStruct((t, d), BF16)],
        scratch_shapes=[pltpu.VMEM((tm, d), F32),
                        pltpu.VMEM((tm, 1), F32),
                        pltpu.VMEM((1, tm), F32),
                        pltpu.VMEM((1, tm), F32),
                        pltpu.VMEM((1, LANES), F32),
                        pltpu.VMEM((cap, d), BF16),
                        pltpu.VMEM((cap, LANES), F32),
                        pltpu.VMEM((cap, d), F32)],
        compiler_params=_params(("parallel", "arbitrary")),
        name="moe_grouped",
    )(xb, xf, comb, tri, wg, wu, wd, ln_g, ln_b)


def kernel(x, mem, positions, w_in, lambda_q1, lambda_k1, lambda_q2, lambda_k2, diff_subln_g, conv_w, w_mem_kv, w_br_attn, w_br_conv, w_br_mem, w_out, ln1_g, ln1_b, ln2_g, ln2_b, w_router, router_bias, w_exp_gate, w_exp_up, w_exp_down):
    batch, seq, d = x.shape
    depth = w_in.shape[0]
    mem_len = mem.shape[1]
    t = batch * seq
    alpha = (2 * depth) ** 0.25

    cos_t, sin_t = _rope_tables(positions, min(PROJ_TM, t))
    lam_init = [0.8 - 0.6 * math.exp(-0.3 * l) for l in range(depth)]
    lam = _lambdas(lambda_q1, lambda_k1, lambda_q2, lambda_k2, lam_init)
    init_rows = jnp.broadcast_to(jnp.asarray(lam_init, F32).reshape(depth, 1), (depth, LANES))

    wr = jnp.zeros((d, LANES), F32).at[:, :N_EXPERTS].set(w_router.astype(F32))
    rb = jnp.zeros((1, LANES), F32).at[0, :N_EXPERTS].set(router_bias.astype(F32))
    wrh = wr.astype(BF16)
    wrl = (wr - wrh.astype(F32)).astype(BF16)
    mem_b = mem.reshape(batch * mem_len, d).astype(BF16)

    xf = x.reshape(t, d).astype(F32)
    xb = xf.astype(BF16)
    for l in range(depth):
        qt, k, vt, cv, qm, g = _projection(xb, w_in[l].astype(BF16), cos_t, sin_t, batch, seq)
        ya = _attention(qt, k, vt, lam[l:l + 1], init_rows[l:l + 1],
                        diff_subln_g[l].astype(F32).reshape(1, DA_V_DIM), batch, seq)
        kvm = _matmul(mem_b, w_mem_kv[l].astype(BF16), MEM_WIDTH, BF16)
        xf, xb, comb = _merge(ya, cv, qm, kvm, g, xf,
                              w_br_attn[l].astype(BF16), w_br_conv[l].astype(BF16),
                              w_br_mem[l].astype(BF16), w_out[l].astype(BF16),
                              conv_w[l].astype(F32),
                              ln1_g[l].astype(F32).reshape(1, d), ln1_b[l].astype(F32).reshape(1, d),
                              wrh, wrl, rb, seq, mem_len, alpha)
        xf, xb = _moe(xb, xf, comb, w_exp_gate[l].astype(BF16), w_exp_up[l].astype(BF16),
                      w_exp_down[l].astype(BF16),
                      ln2_g[l].astype(F32).reshape(1, d), ln2_b[l].astype(F32).reshape(1, d), alpha)
    return xf.reshape(batch, seq, d).astype(x.dtype)
```

```python
import functools
import math

import jax
import jax.numpy as jnp
from jax import lax
from jax.experimental import pallas as pl
from jax.experimental.pallas import tpu as pltpu

F32 = jnp.float32
BF16 = jnp.bfloat16

DA_HEADS = 8
DA_HEAD_DIM = 64
DA_V_DIM = 2 * DA_HEAD_DIM
CONV_WIDTH = 512
CONV_K = 3
MEM_HEADS = 4
MEM_HEAD_DIM = 128
MEM_WIDTH = MEM_HEADS * MEM_HEAD_DIM
N_BRANCH = 3
ROPE_THETA = 10000.0
N_EXPERTS = 16
N_GROUPS = 4
GROUP_LANE = 16
EXPERTS_PER_GROUP = N_EXPERTS // N_GROUPS
LN_EPS = 1e-5
RMS_EPS = 1e-5

LANES = 128
SUBLANES = 8
VMEM_LIMIT_BYTES = 56 * 1024 * 1024

PROJ_TM = 2048
PROJ_TN = 512
ATTN_TQ = 1024
MERGE_TM = 512
MOE_TM = 1024
MOE_CHUNK = 288


def _params(semantics, flags=None):
    return pltpu.CompilerParams(dimension_semantics=semantics,
                                vmem_limit_bytes=VMEM_LIMIT_BYTES, flags=flags)


def _rope_table_kernel(pos_ref, inv_ref, cos_ref, sin_ref):
    ang = pos_ref[...].astype(F32) * inv_ref[...]
    lane = lax.broadcasted_iota(jnp.int32, ang.shape, 1)
    first_half = (lane % DA_HEAD_DIM) < (DA_HEAD_DIM // 2)
    s = jnp.sin(ang)
    cos_ref[...] = jnp.cos(ang)
    sin_ref[...] = jnp.where(first_half, -s, s)


def _rope_tables(positions, tm):
    t = positions.size
    half = DA_HEAD_DIM // 2
    inv_freq = ROPE_THETA ** (-jnp.arange(0, DA_HEAD_DIM, 2, dtype=F32) / DA_HEAD_DIM)
    inv_lanes = jnp.tile(inv_freq, LANES // half).reshape(1, LANES)
    pos = positions.reshape(t, 1)
    return pl.pallas_call(
        _rope_table_kernel,
        grid=(t // tm,),
        in_specs=[pl.BlockSpec((tm, 1), lambda i: (i, 0)),
                  pl.BlockSpec((1, LANES), lambda i: (0, 0))],
        out_specs=[pl.BlockSpec((tm, LANES), lambda i: (i, 0)),
                   pl.BlockSpec((tm, LANES), lambda i: (i, 0))],
        out_shape=[jax.ShapeDtypeStruct((t, LANES), F32),
                   jax.ShapeDtypeStruct((t, LANES), F32)],
        compiler_params=_params(("parallel",)),
        name="rope_tables",
    )(pos, inv_lanes)


def _lambda_kernel(q1_ref, k1_ref, q2_ref, k2_ref, init_ref, lam_ref):
    a = jnp.sum(q1_ref[...] * k1_ref[...], axis=-1, keepdims=True)
    b = jnp.sum(q2_ref[...] * k2_ref[...], axis=-1, keepdims=True)
    lam_ref[...] = jnp.exp(a) - jnp.exp(b) + init_ref[...]


def _lambdas(lq1, lk1, lq2, lk2, lam_init):
    depth = lq1.shape[0]
    init = jnp.broadcast_to(jnp.asarray(lam_init, F32).reshape(depth, 1), (depth, LANES))
    return pl.pallas_call(
        _lambda_kernel,
        out_shape=jax.ShapeDtypeStruct((depth, LANES), F32),
        name="diff_lambda",
    )(lq1.astype(F32), lk1.astype(F32), lq2.astype(F32), lk2.astype(F32), init)


_Q_TILE0, _K_TILE0, _V_TILE0, _CV_TILE0, _QM_TILE, _G_TILE0, _N_TILES = 0, 2, 4, 6, 9, 10, 16
_Q_SCALE = (DA_HEAD_DIM ** -0.5) * math.log2(math.e)


def _rotary(acc, cos_ref, sin_ref):
    tn = acc.shape[1]
    reps = tn // LANES
    c = jnp.concatenate([cos_ref[...]] * reps, axis=1)
    s = jnp.concatenate([sin_ref[...]] * reps, axis=1)
    lane = lax.broadcasted_iota(jnp.int32, acc.shape, 1)
    half = DA_HEAD_DIM // 2
    first_half = (lane % DA_HEAD_DIM) < half
    partner = jnp.where(first_half,
                        pltpu.roll(acc, tn - half, axis=1),
                        pltpu.roll(acc, half, axis=1))
    return acc * c + partner * s


def _proj_segment_kernel(x_ref, w_ref, *refs, epilogue):
    *aux, o_ref = refs
    acc = jnp.dot(x_ref[...], w_ref[...], preferred_element_type=F32)
    o_ref[...] = epilogue(acc, *aux).astype(o_ref.dtype)


def _projection(xb, w, cos_t, sin_t, batch, seq):
    t, d = xb.shape
    tm, tn = min(PROJ_TM, seq), PROJ_TN
    assert seq % tm == 0 and w.shape == (d, _N_TILES * tn)
    tps = seq // tm
    tables = [cos_t, sin_t]

    def segment(name, tile0, n_tiles, epilogue, aux, out_dtype, transposed):
        if transposed:
            out_spec = pl.BlockSpec((None, tn, tm), lambda i, j: (i // tps, j, i % tps))
            out_shape = jax.ShapeDtypeStruct((batch, n_tiles * tn, seq), out_dtype)
        else:
            out_spec = pl.BlockSpec((tm, tn), lambda i, j: (i, j))
            out_shape = jax.ShapeDtypeStruct((t, n_tiles * tn), out_dtype)
        return pl.pallas_call(
            functools.partial(_proj_segment_kernel, epilogue=epilogue),
            grid=(t // tm, n_tiles),
            in_specs=[pl.BlockSpec((tm, d), lambda i, j: (i, 0)),
                      pl.BlockSpec((d, tn), lambda i, j: (0, tile0 + j))]
                     + [pl.BlockSpec((tm, LANES), lambda i, j: (i, 0)) for _ in aux],
            out_specs=out_spec,
            out_shape=out_shape,
            compiler_params=_params(("parallel", "arbitrary")),
            name=name,
        )(xb, w, *aux)

    qt = segment("proj_q", _Q_TILE0, 2,
                 lambda acc, c, s: (_rotary(acc, c, s) * _Q_SCALE).T, tables, BF16, True)
    k = segment("proj_k", _K_TILE0, 2, _rotary, tables, BF16, False)
    vt = segment("proj_v", _V_TILE0, 2, lambda acc: acc.T, [], BF16, True)
    cv = segment("proj_conv", _CV_TILE0, 3, lambda acc: acc, [], F32, False)
    qm = segment("proj_mem_q", _QM_TILE, 1, lambda acc: acc, [], BF16, False)
    g = segment("proj_gates", _G_TILE0, 6, jax.nn.sigmoid, [], BF16, False)
    return qt, k, vt, cv, qm, g


def _matmul_kernel(a_ref, b_ref, o_ref):
    o_ref[...] = jnp.dot(a_ref[...], b_ref[...], preferred_element_type=F32).astype(o_ref.dtype)


def _matmul(a, b, tn, out_dtype):
    m, k = a.shape
    n = b.shape[1]
    return pl.pallas_call(
        _matmul_kernel,
        grid=(n // tn,),
        in_specs=[pl.BlockSpec((m, k), lambda j: (0, 0)),
                  pl.BlockSpec((k, tn), lambda j: (0, j))],
        out_specs=pl.BlockSpec((m, tn), lambda j: (0, j)),
        out_shape=jax.ShapeDtypeStruct((m, n), out_dtype),
        compiler_params=_params(("parallel",)),
        name="mem_kv_projection",
    )(a, b)


def _sublane_all(op, x):
    for shift in (4, 2, 1):
        x = op(x, pltpu.roll(x, shift, axis=0))
    return x


ATTN_MAX_EXCESS = 100.0


def _attn_kernel(lam_ref, init_ref, g_ref, qt_ref, k_ref, vt_ref, o_ref,
                 s0_sc, s1_sc, x0_sc, x1_sc, p0_sc, p1_sc, a0_sc, a1_sc, m_sc, l_sc, acc_sc, e_sc):
    s_sc, x_sc, p_sc, a_sc = (s0_sc, s1_sc), (x0_sc, x1_sc), (p0_sc, p1_sc), (a0_sc, a1_sc)
    qi = pl.program_id(2)
    dv, tq = qt_ref.shape
    tk, r = s0_sc.shape
    assert tq == 2 * tk and r == 2 * tq
    qt = qt_ref[...]
    sub = lax.broadcasted_iota(jnp.int32, qt.shape, 0)
    zero = jnp.zeros_like(qt)
    qq = jnp.concatenate([jnp.where(sub < DA_HEAD_DIM, qt, zero),
                          jnp.where(sub >= DA_HEAD_DIM, qt, zero)], axis=1)
    ones = jnp.ones((2 * SUBLANES, tk), vt_ref.dtype)

    def raw_scores(kb, diag):
        start = pl.multiple_of(kb * tk, tk)
        s = jnp.dot(k_ref[pl.ds(start, tk), :], qq, preferred_element_type=F32)
        if diag is not None:
            key = lax.broadcasted_iota(jnp.int32, s.shape, 0) + diag * tk
            qry = lax.broadcasted_iota(jnp.int32, s.shape, 1) & (tq - 1)
            s = jnp.where(key <= qry, s, -jnp.inf)
        return s.reshape(tk // SUBLANES, SUBLANES, r)

    def pv(kb, slot):
        start = pl.multiple_of(kb * tk, tk)
        lhs = jnp.concatenate([vt_ref[:, pl.ds(start, tk)], ones], axis=0)
        res = jnp.dot(lhs, p_sc[slot][...], preferred_element_type=F32)
        alpha = a_sc[slot][...]
        acc3 = acc_sc[...].reshape(dv // SUBLANES, SUBLANES, r) * alpha[None]
        acc_sc[...] = acc3.reshape(dv, r) + res[:dv]
        l_sc[...] = alpha * l_sc[...] + res[dv:dv + SUBLANES]

    def reset(m0):
        m_sc[...] = m0
        l_sc[...] = jnp.zeros(l_sc.shape, F32)
        acc_sc[...] = jnp.zeros(acc_sc.shape, F32)

    d0, d1 = 2 * qi, 2 * qi + 1

    def scores_exp(kb, slot, diag=None):
        s3 = raw_scores(kb, diag)
        x0_sc[...] = jnp.max(s3, axis=0)
        p_sc[slot][...] = jnp.exp2(s3 - m_sc[...][None]).reshape(tk, r).astype(p_sc[slot].dtype)

    def advance(slot):
        bm = _sublane_all(jnp.maximum, x0_sc[...])
        m_prev = m_sc[...]
        e_sc[...] = jnp.maximum(e_sc[...], bm - m_prev)
        m_new = jnp.maximum(m_prev, bm)
        a_sc[slot][...] = jnp.exp2(m_prev - m_new)
        m_sc[...] = m_new

    start = pl.multiple_of(qi * tq, tq)
    own = k_ref[pl.ds(start, tq), :].astype(F32).T * qt.astype(F32)
    seed = jnp.concatenate([jnp.sum(own[:DA_HEAD_DIM], axis=0, keepdims=True),
                            jnp.sum(own[DA_HEAD_DIM:], axis=0, keepdims=True)], axis=1)
    reset(jnp.broadcast_to(seed, m_sc.shape))
    e_sc[...] = jnp.full(e_sc.shape, -jnp.inf, F32)
    a0_sc[...] = jnp.ones(a0_sc.shape, F32)
    scores_exp(d0, 0, diag=0)

    def fast_pair(j, carry):
        advance(1)
        scores_exp(2 * j, 1)
        pv(jnp.where(j == 0, d0, 2 * j - 1), 0)
        advance(0)
        scores_exp(2 * j + 1, 0)
        pv(2 * j, 1)
        return carry

    lax.fori_loop(0, qi, fast_pair, 0)
    advance(1)
    scores_exp(d1, 1, diag=1)
    pv(jnp.where(qi == 0, d0, d0 - 1), 0)
    e_sc[...] = jnp.maximum(e_sc[...], _sublane_all(jnp.maximum, x0_sc[...]) - m_sc[...])
    pv(d1, 1)

    def scores(kb, slot, diag=None):
        s3 = raw_scores(kb, diag)
        s_sc[slot][...] = s3.reshape(tk, r)
        x_sc[slot][...] = jnp.max(s3, axis=0)

    def softmax(slot):
        s3 = s_sc[slot][...].reshape(tk // SUBLANES, SUBLANES, r)
        m_prev = m_sc[...]
        m_new = jnp.maximum(m_prev, _sublane_all(jnp.maximum, x_sc[slot][...]))
        p_sc[slot][...] = jnp.exp2(s3 - m_new[None]).reshape(tk, r).astype(p_sc[slot].dtype)
        a_sc[slot][...] = jnp.exp2(m_prev - m_new)
        m_sc[...] = m_new

    def blk(t):
        return jnp.where(t < 2, d0 + t, t - 2)

    @pl.when(jnp.max(e_sc[...]) > ATTN_MAX_EXCESS)
    def _():
        reset(jnp.full(m_sc.shape, -jnp.inf, F32))
        scores(d0, 0, diag=0)
        scores(d1, 1, diag=1)
        softmax(0)

        def pair(j, carry):
            t = 2 * j + 1
            scores(t - 1, 0)
            softmax(1)
            pv(blk(t - 1), 0)
            scores(t, 1)
            softmax(0)
            pv(blk(t), 1)
            return carry

        lax.fori_loop(0, qi, pair, 0)
        n = 2 * qi + 1
        softmax(1)
        pv(blk(n - 1), 0)
        pv(blk(n), 1)

    ot = (acc_sc[...].reshape(dv // SUBLANES, SUBLANES, r) / l_sc[...][None]).reshape(dv, r)
    o = ot[:, :tq] - lam_ref[0:1, 0:1] * ot[:, tq:]
    o = o * lax.rsqrt(jnp.mean(jnp.square(o), axis=0, keepdims=True) + RMS_EPS)
    o = o.T * g_ref[...] * (1.0 - init_ref[...])
    o_ref[...] = o.astype(o_ref.dtype)


def _attention(qt, k, vt, lam, lam_init, subln_g, batch, seq):
    t = k.shape[0]
    tq = min(ATTN_TQ, seq)
    tk = tq // 2
    assert seq % tq == 0 and tq & (tq - 1) == 0
    nq = seq // tq
    h = DA_HEADS
    r = 2 * tq
    row = pl.BlockSpec((1, LANES), lambda b, hh, i: (0, 0))
    return pl.pallas_call(
        _attn_kernel,
        grid=(batch, h, nq),
        in_specs=[row, row, row,
                  pl.BlockSpec((None, DA_V_DIM, tq), lambda b, hh, i: (b, hh, i)),
                  pl.BlockSpec((seq, DA_V_DIM), lambda b, hh, i: (b, hh)),
                  pl.BlockSpec((None, DA_V_DIM, seq), lambda b, hh, i: (b, hh, 0))],
        out_specs=pl.BlockSpec((tq, DA_V_DIM), lambda b, hh, i: (b * nq + i, hh)),
        out_shape=jax.ShapeDtypeStruct((t, h * DA_V_DIM), BF16),
        scratch_shapes=[pltpu.VMEM((tk, r), F32), pltpu.VMEM((tk, r), F32),
                        pltpu.VMEM((SUBLANES, r), F32), pltpu.VMEM((SUBLANES, r), F32),
                        pltpu.VMEM((tk, r), BF16), pltpu.VMEM((tk, r), BF16),
                        pltpu.VMEM((SUBLANES, r), F32), pltpu.VMEM((SUBLANES, r), F32),
                        pltpu.VMEM((SUBLANES, r), F32),
                        pltpu.VMEM((SUBLANES, r), F32),
                        pltpu.VMEM((DA_V_DIM, r), F32),
                        pltpu.VMEM((SUBLANES, r), F32)],
        compiler_params=_params(("parallel", "parallel", "arbitrary")),
        name="diff_attention",
    )(lam, lam_init, subln_g, qt, k, vt)


def _layer_norm(h, g, b):
    mu = jnp.mean(h, axis=-1, keepdims=True)
    d = h - mu
    var = jnp.mean(jnp.square(d), axis=-1, keepdims=True)
    return d * lax.rsqrt(var + LN_EPS) * g + b


def _route(logits):
    lane = lax.broadcasted_iota(jnp.int32, logits.shape, 1)
    valid = lane < N_EXPERTS
    logits = jnp.where(valid, logits, -jnp.inf)
    mx = jnp.max(logits, axis=-1, keepdims=True)
    ex = jnp.exp(logits - mx)
    scores = ex / jnp.sum(ex, axis=-1, keepdims=True)
    group = lane // EXPERTS_PER_GROUP
    neg = jnp.full_like(scores, -1.0)
    best = jnp.max(jnp.where(group == 0, scores, neg), axis=-1, keepdims=True)
    gidx = jnp.zeros(best.shape, jnp.int32)
    for gi in range(1, N_GROUPS):
        cand = jnp.max(jnp.where(group == gi, scores, neg), axis=-1, keepdims=True)
        take = cand > best
        gidx = jnp.where(take, gi, gidx)
        best = jnp.where(take, cand, best)
    vals = jnp.where((group == gidx) & valid, scores, neg)
    big = jnp.full_like(lane, LANES)
    v1 = jnp.max(vals, axis=-1, keepdims=True)
    i1 = jnp.min(jnp.where(vals == v1, lane, big), axis=-1, keepdims=True)
    vals2 = jnp.where(lane == i1, neg, vals)
    v2 = jnp.max(vals2, axis=-1, keepdims=True)
    i2 = jnp.min(jnp.where(vals2 == v2, lane, big), axis=-1, keepdims=True)
    tot = v1 + v2
    comb = jnp.where(lane == i1, v1 / tot, 0.0) + jnp.where(lane == i2, v2 / tot, 0.0)
    return jnp.where(lane == GROUP_LANE, gidx.astype(F32), comb)


def _merge_kernel(ya_ref, cv_ref, cvp_ref, qm_ref, kvm_ref, g_ref, x_ref,
                  wa_ref, wc_ref, wm_ref, wo_ref, cw_ref, lng_ref, lnb_ref, wrh_ref, wrl_ref, rb_ref,
                  xo_ref, xob_ref, comb_ref, *, tiles_per_seq, alpha):
    i = pl.program_id(0)
    tm = ya_ref.shape[0]
    cw = CONV_WIDTH

    z = cv_ref[:, cw:2 * cw] * cv_ref[:, 2 * cw:3 * cw]
    zp = cvp_ref[:, cw:2 * cw] * cvp_ref[:, 2 * cw:3 * cw]
    zp = jnp.where(i % tiles_per_seq == 0, jnp.zeros_like(zp), zp)
    row = lax.broadcasted_iota(jnp.int32, z.shape, 0)
    zm1 = jnp.where(row == 0, zp[SUBLANES - 1:SUBLANES], pltpu.roll(z, 1, axis=0))
    zm2 = jnp.where(row == 0, zp[SUBLANES - 2:SUBLANES - 1],
                    jnp.where(row == 1, zp[SUBLANES - 1:SUBLANES], pltpu.roll(z, 2, axis=0)))
    w = cw_ref[...]
    y = w[0:1] * zm2
    y = y + w[1:2] * zm1
    y = y + w[2:3] * z
    y_conv = (cv_ref[:, 0:cw] * y).astype(BF16)

    outs = []
    for hd in range(MEM_HEADS):
        lo = hd * MEM_HEAD_DIM
        qh = qm_ref[:, lo:lo + MEM_HEAD_DIM]
        kh = kvm_ref[:, lo:lo + MEM_HEAD_DIM]
        vh = kvm_ref[:, MEM_WIDTH + lo:MEM_WIDTH + lo + MEM_HEAD_DIM]
        s = lax.dot_general(qh, kh, (((1,), (1,)), ((), ())),
                            preferred_element_type=F32) * (MEM_HEAD_DIM ** -0.5)
        s = s - jnp.max(s, axis=-1, keepdims=True)
        e = jnp.exp(s)
        p = e / jnp.sum(e, axis=-1, keepdims=True)
        outs.append(jnp.dot(p.astype(BF16), vh, preferred_element_type=F32))
    y_mem = jnp.concatenate(outs, axis=1).astype(BF16)

    d = x_ref.shape[1]
    merged = g_ref[:, 0:d] * jnp.dot(ya_ref[...], wa_ref[...], preferred_element_type=F32)
    merged = merged + g_ref[:, d:2 * d] * jnp.dot(y_conv, wc_ref[...], preferred_element_type=F32)
    merged = merged + g_ref[:, 2 * d:3 * d] * jnp.dot(y_mem, wm_ref[...], preferred_element_type=F32)
    mix = jnp.dot(merged.astype(BF16), wo_ref[...], preferred_element_type=F32)
    x1 = _layer_norm(alpha * x_ref[...] + mix, lng_ref[...], lnb_ref[...])
    xo_ref[...] = x1
    xob_ref[...] = x1.astype(BF16)

    x1h = x1.astype(BF16)
    x1l = (x1 - x1h.astype(F32)).astype(BF16)
    logits = (jnp.dot(x1h, wrh_ref[...], preferred_element_type=F32)
              + (jnp.dot(x1h, wrl_ref[...], preferred_element_type=F32)
                 + jnp.dot(x1l, wrh_ref[...], preferred_element_type=F32))) + rb_ref[...]
    comb_ref[...] = _route(logits)


def _merge(ya, cv, qm, kvm, g, xf, wa, wc, wm, wo, conv_w, ln_g, ln_b, wrh, wrl, rb, seq, mem_len, alpha):
    t, d = xf.shape
    tm = min(MERGE_TM, seq)
    assert seq % tm == 0 and tm % SUBLANES == 0
    tiles_per_seq = seq // tm
    halo = tm // SUBLANES

    def full(a):
        return pl.BlockSpec(a.shape, lambda i: (0,) * a.ndim)

    def rows(width):
        return pl.BlockSpec((tm, width), lambda i: (i, 0))

    kern = functools.partial(_merge_kernel, tiles_per_seq=tiles_per_seq, alpha=alpha)
    return pl.pallas_call(
        kern,
        grid=(t // tm,),
        in_specs=[rows(ya.shape[1]),
                  rows(cv.shape[1]),
                  pl.BlockSpec((SUBLANES, cv.shape[1]), lambda i: (jnp.maximum(i * halo - 1, 0), 0)),
                  rows(qm.shape[1]),
                  pl.BlockSpec((mem_len, kvm.shape[1]), lambda i: (i // tiles_per_seq, 0)),
                  rows(g.shape[1]),
                  rows(d),
                  full(wa), full(wc), full(wm), full(wo), full(conv_w),
                  full(ln_g), full(ln_b), full(wrh), full(wrl), full(rb)],
        out_specs=[rows(d), rows(d), rows(LANES)],
        out_shape=[jax.ShapeDtypeStruct((t, d), F32),
                   jax.ShapeDtypeStruct((t, d), BF16),
                   jax.ShapeDtypeStruct((t, LANES), F32)],
        compiler_params=_params(("parallel",)),
        name="merge_ln_router",
    )(ya, cv, cv, qm, kvm, g, xf, wa, wc, wm, wo, conv_w, ln_g, ln_b, wrh, wrl, rb)


def _moe_kernel(xb_ref, xf_ref, comb_ref, tri_ref, wg_ref, wu_ref, wd_ref, lng_ref, lnb_ref,
                xo_ref, xob_ref,
                acc_sc, pos_sc, posrow_sc, grow_sc, cnt_sc, xg_sc, wts_sc, y_sc, *, alpha, chunk):
    e = pl.program_id(1)
    grp = e // EXPERTS_PER_GROUP
    k = e % EXPERTS_PER_GROUP
    tm = xb_ref.shape[0]
    lane = lax.broadcasted_iota(jnp.int32, (tm, LANES), 1)

    @pl.when(e == 0)
    def _():
        acc_sc[...] = jnp.zeros(acc_sc.shape, F32)
        gidx = jnp.sum(jnp.where(lane == GROUP_LANE, comb_ref[...], 0.0), axis=1, keepdims=True)
        member = (lane.astype(F32) == gidx).astype(BF16)
        before = jnp.dot(tri_ref[...], member, preferred_element_type=F32)
        pos_sc[...] = jnp.sum(jnp.where(lane.astype(F32) == gidx, before, 0.0), axis=1, keepdims=True)
        cnt_sc[...] = jnp.sum(member.astype(F32), axis=0, keepdims=True)
        grow = comb_ref[...].T[GROUP_LANE:GROUP_LANE + 1]
        sub = lax.broadcasted_iota(jnp.int32, (LANES, tm), 0).astype(F32)
        posrow_sc[...] = jnp.sum(jnp.where(sub == grow, before.T, 0.0), axis=0, keepdims=True)
        grow_sc[...] = grow

    cnt = jnp.sum(jnp.where(lane[0:1] == grp, cnt_sc[...], 0.0)).astype(jnp.int32)
    n_chunks = (cnt + (chunk - 1)) // chunk

    def onehot(c):
        gidx = jnp.sum(jnp.where(lane == GROUP_LANE, comb_ref[...], 0.0), axis=1, keepdims=True)
        slot = jnp.where(gidx == grp.astype(F32), pos_sc[...], -1.0) - (c * chunk).astype(F32)
        col = lax.broadcasted_iota(jnp.int32, (tm, chunk), 1).astype(F32)
        return (col == slot).astype(BF16)

    def onehot_rows(c):
        slot = (jnp.where(grow_sc[...] == grp.astype(F32), posrow_sc[...], -1.0)
                - (c * chunk).astype(F32))
        row = lax.broadcasted_iota(jnp.int32, (chunk, tm), 0).astype(F32)
        return (row == slot).astype(BF16)

    def rows(c):
        return pl.ds(pl.multiple_of(c * chunk, chunk), chunk)

    @pl.when(k == 0)
    def _():
        comb = jnp.where(lane < N_EXPERTS, comb_ref[...], 0.0)
        comb_hi = comb.astype(BF16)
        comb_lo = (comb - comb_hi.astype(F32)).astype(BF16)

        def gather(c, carry):
            sel = onehot_rows(c)
            xg_sc[rows(c), :] = jnp.dot(sel, xb_ref[...], preferred_element_type=F32).astype(BF16)
            wts_sc[rows(c), :] = (jnp.dot(sel, comb_hi, preferred_element_type=F32)
                                  + jnp.dot(sel, comb_lo, preferred_element_type=F32))
            y_sc[rows(c), :] = jnp.zeros((chunk, y_sc.shape[1]), F32)
            return carry

        lax.fori_loop(0, n_chunks, gather, 0)

    def ffn(c, carry):
        xg = xg_sc[rows(c), :]
        hg = jnp.dot(xg, wg_ref[...], preferred_element_type=F32)
        hu = jnp.dot(xg, wu_ref[...], preferred_element_type=F32)
        h = (hg * jax.nn.sigmoid(hg)) * hu
        y = jnp.dot(h.astype(BF16), wd_ref[...], preferred_element_type=F32)
        wts = wts_sc[rows(c), :]
        wl = lax.broadcasted_iota(jnp.int32, wts.shape, 1)
        w = jnp.sum(jnp.where(wl == e, wts, 0.0), axis=1, keepdims=True)
        y_sc[rows(c), :] += jnp.where(w != 0.0, w * y, 0.0)
        return carry

    lax.fori_loop(0, n_chunks, ffn, 0)

    @pl.when(k == EXPERTS_PER_GROUP - 1)
    def _():
        def scatter(c, carry):
            acc_sc[...] += jnp.dot(onehot(c), y_sc[rows(c), :].astype(BF16),
                                   preferred_element_type=F32)
            return carry

        lax.fori_loop(0, n_chunks, scatter, 0)

    @pl.when(e == pl.num_programs(1) - 1)
    def _():
        x2 = _layer_norm(alpha * xf_ref[...] + acc_sc[...], lng_ref[...], lnb_ref[...])
        xo_ref[...] = x2
        xob_ref[...] = x2.astype(BF16)


def _moe(xb, xf, comb, wg, wu, wd, ln_g, ln_b, alpha):
    t, d = xf.shape
    tm = min(MOE_TM, t)
    assert t % tm == 0
    n_e, _, dff = wg.shape
    chunk = min(MOE_CHUNK, tm)
    cap = -(-tm // chunk) * chunk
    tri = jnp.tril(jnp.ones((tm, tm), BF16), -1)
    rows = lambda width: pl.BlockSpec((tm, width), lambda i, e: (i, 0))
    vec = pl.BlockSpec((1, d), lambda i, e: (0, 0))
    return pl.pallas_call(
        functools.partial(_moe_kernel, alpha=alpha, chunk=chunk),
        grid=(t // tm, n_e),
        in_specs=[rows(d), rows(d), rows(LANES),
                  pl.BlockSpec((tm, tm), lambda i, e: (0, 0)),
                  pl.BlockSpec((None, d, dff), lambda i, e: (e, 0, 0)),
                  pl.BlockSpec((None, d, dff), lambda i, e: (e, 0, 0)),
                  pl.BlockSpec((None, dff, d), lambda i, e: (e, 0, 0)),
                  vec, vec],
        out_specs=[rows(d), rows(d)],
        out_shape=[jax.ShapeDtypeStruct((t, d), F32),
                   jax.ShapeDtypeStruct((t, d), BF16)],
        scratch_shapes=[pltpu.VMEM((tm, d), F32),
                        pltpu.VMEM((tm, 1), F32),
                        pltpu.VMEM((1, tm), F32),
                        pltpu.VMEM((1, tm), F32),
                        pltpu.VMEM((1, LANES), F32),
                        pltpu.VMEM((cap, d), BF16),
                        pltpu.VMEM((cap, LANES), F32),
                        pltpu.VMEM((cap, d), F32)],
        compiler_params=_params(("parallel", "arbitrary")),
        name="moe_grouped",
    )(xb, xf, comb, tri, wg, wu, wd, ln_g, ln_b)


def kernel(x, mem, positions, w_in, lambda_q1, lambda_k1, lambda_q2, lambda_k2, diff_subln_g, conv_w, w_mem_kv, w_br_attn, w_br_conv, w_br_mem, w_out, ln1_g, ln1_b, ln2_g, ln2_b, w_router, router_bias, w_exp_gate, w_exp_up, w_exp_down):
    batch, seq, d = x.shape
    depth = w_in.shape[0]
    mem_len = mem.shape[1]
    t = batch * seq
    alpha = (2 * depth) ** 0.25

    cos_t, sin_t = _rope_tables(positions, min(PROJ_TM, t))
    lam_init = [0.8 - 0.6 * math.exp(-0.3 * l) for l in range(depth)]
    lam = _lambdas(lambda_q1, lambda_k1, lambda_q2, lambda_k2, lam_init)
    init_rows = jnp.broadcast_to(jnp.asarray(lam_init, F32).reshape(depth, 1), (depth, LANES))

    wr = jnp.zeros((d, LANES), F32).at[:, :N_EXPERTS].set(w_router.astype(F32))
    rb = jnp.zeros((1, LANES), F32).at[0, :N_EXPERTS].set(router_bias.astype(F32))
    wrh = wr.astype(BF16)
    wrl = (wr - wrh.astype(F32)).astype(BF16)
    mem_b = mem.reshape(batch * mem_len, d).astype(BF16)

    xf = x.reshape(t, d).astype(F32)
    xb = xf.astype(BF16)
    for l in range(depth):
        qt, k, vt, cv, qm, g = _projection(xb, w_in[l].astype(BF16), cos_t, sin_t, batch, seq)
        ya = _attention(qt, k, vt, lam[l:l + 1], init_rows[l:l + 1],
                        diff_subln_g[l].astype(F32).reshape(1, DA_V_DIM), batch, seq)
        kvm = _matmul(mem_b, w_mem_kv[l].astype(BF16), MEM_WIDTH, BF16)
        xf, xb, comb = _merge(ya, cv, qm, kvm, g, xf,
                              w_br_attn[l].astype(BF16), w_br_conv[l].astype(BF16),
                              w_br_mem[l].astype(BF16), w_out[l].astype(BF16),
                              conv_w[l].astype(F32),
                              ln1_g[l].astype(F32).reshape(1, d), ln1_b[l].astype(F32).reshape(1, d),
                              wrh, wrl, rb, seq, mem_len, alpha)
        xf, xb = _moe(xb, xf, comb, w_exp_gate[l].astype(BF16), w_exp_up[l].astype(BF16),
                      w_exp_down[l].astype(BF16),
                      ln2_g[l].astype(F32).reshape(1, d), ln2_b[l].astype(F32).reshape(1, d), alpha)
    return xf.reshape(batch, seq, d).astype(x.dtype)
```

```python
import functools
import math

import jax
import jax.numpy as jnp
from jax import lax
from jax.experimental import pallas as pl
from jax.experimental.pallas import tpu as pltpu

F32 = jnp.float32
BF16 = jnp.bfloat16

DA_HEADS = 8
DA_HEAD_DIM = 64
DA_V_DIM = 2 * DA_HEAD_DIM
CONV_WIDTH = 512
CONV_K = 3
MEM_HEADS = 4
MEM_HEAD_DIM = 128
MEM_WIDTH = MEM_HEADS * MEM_HEAD_DIM
N_BRANCH = 3
ROPE_THETA = 10000.0
N_EXPERTS = 16
N_GROUPS = 4
GROUP_LANE = 16
EXPERTS_PER_GROUP = N_EXPERTS // N_GROUPS
LN_EPS = 1e-5
RMS_EPS = 1e-5

LANES = 128
SUBLANES = 8
VMEM_LIMIT_BYTES = 56 * 1024 * 1024

PROJ_TM = 2048
PROJ_TN = 512
ATTN_TQ = 1024
MERGE_TM = 512
MOE_TM = 1024
MOE_CHUNK = 288


def _params(semantics, flags=None):
    return pltpu.CompilerParams(dimension_semantics=semantics,
                                vmem_limit_bytes=VMEM_LIMIT_BYTES, flags=flags)


def _rope_table_kernel(pos_ref, inv_ref, cos_ref, sin_ref):
    ang = pos_ref[...].astype(F32) * inv_ref[...]
    lane = lax.broadcasted_iota(jnp.int32, ang.shape, 1)
    first_half = (lane % DA_HEAD_DIM) < (DA_HEAD_DIM // 2)
    s = jnp.sin(ang)
    cos_ref[...] = jnp.cos(ang)
    sin_ref[...] = jnp.where(first_half, -s, s)


def _rope_tables(positions, tm):
    t = positions.size
    half = DA_HEAD_DIM // 2
    inv_freq = ROPE_THETA ** (-jnp.arange(0, DA_HEAD_DIM, 2, dtype=F32) / DA_HEAD_DIM)
    inv_lanes = jnp.tile(inv_freq, LANES // half).reshape(1, LANES)
    pos = positions.reshape(t, 1)
    return pl.pallas_call(
        _rope_table_kernel,
        grid=(t // tm,),
        in_specs=[pl.BlockSpec((tm, 1), lambda i: (i, 0)),
                  pl.BlockSpec((1, LANES), lambda i: (0, 0))],
        out_specs=[pl.BlockSpec((tm, LANES), lambda i: (i, 0)),
                   pl.BlockSpec((tm, LANES), lambda i: (i, 0))],
        out_shape=[jax.ShapeDtypeStruct((t, LANES), F32),
                   jax.ShapeDtypeStruct((t, LANES), F32)],
        compiler_params=_params(("parallel",)),
        name="rope_tables",
    )(pos, inv_lanes)


def _lambda_kernel(q1_ref, k1_ref, q2_ref, k2_ref, init_ref, lam_ref):
    a = jnp.sum(q1_ref[...] * k1_ref[...], axis=-1, keepdims=True)
    b = jnp.sum(q2_ref[...] * k2_ref[...], axis=-1, keepdims=True)
    lam_ref[...] = jnp.exp(a) - jnp.exp(b) + init_ref[...]


def _lambdas(lq1, lk1, lq2, lk2, lam_init):
    depth = lq1.shape[0]
    init = jnp.broadcast_to(jnp.asarray(lam_init, F32).reshape(depth, 1), (depth, LANES))
    return pl.pallas_call(
        _lambda_kernel,
        out_shape=jax.ShapeDtypeStruct((depth, LANES), F32),
        name="diff_lambda",
    )(lq1.astype(F32), lk1.astype(F32), lq2.astype(F32), lk2.astype(F32), init)


_Q_TILE0, _K_TILE0, _V_TILE0, _CV_TILE0, _QM_TILE, _G_TILE0, _N_TILES = 0, 2, 4, 6, 9, 10, 16
_Q_SCALE = (DA_HEAD_DIM ** -0.5) * math.log2(math.e)


def _rotary(acc, cos_ref, sin_ref):
    tn = acc.shape[1]
    reps = tn // LANES
    c = jnp.concatenate([cos_ref[...]] * reps, axis=1)
    s = jnp.concatenate([sin_ref[...]] * reps, axis=1)
    lane = lax.broadcasted_iota(jnp.int32, acc.shape, 1)
    half = DA_HEAD_DIM // 2
    first_half = (lane % DA_HEAD_DIM) < half
    partner = jnp.where(first_half,
                        pltpu.roll(acc, tn - half, axis=1),
                        pltpu.roll(acc, half, axis=1))
    return acc * c + partner * s


def _proj_segment_kernel(x_ref, w_ref, *refs, epilogue):
    *aux, o_ref = refs
    acc = jnp.dot(x_ref[...], w_ref[...], preferred_element_type=F32)
    o_ref[...] = epilogue(acc, *aux).astype(o_ref.dtype)


def _projection(xb, w, cos_t, sin_t, batch, seq):
    t, d = xb.shape
    tm, tn = min(PROJ_TM, seq), PROJ_TN
    assert seq % tm == 0 and w.shape == (d, _N_TILES * tn)
    tps = seq // tm
    tables = [cos_t, sin_t]

    def segment(name, tile0, n_tiles, epilogue, aux, out_dtype, transposed):
        if transposed:
            out_spec = pl.BlockSpec((None, tn, tm), lambda i, j: (i // tps, j, i % tps))
            out_shape = jax.ShapeDtypeStruct((batch, n_tiles * tn, seq), out_dtype)
        else:
            out_spec = pl.BlockSpec((tm, tn), lambda i, j: (i, j))
            out_shape = jax.ShapeDtypeStruct((t, n_tiles * tn), out_dtype)
        return pl.pallas_call(
            functools.partial(_proj_segment_kernel, epilogue=epilogue),
            grid=(t // tm, n_tiles),
            in_specs=[pl.BlockSpec((tm, d), lambda i, j: (i, 0)),
                      pl.BlockSpec((d, tn), lambda i, j: (0, tile0 + j))]
                     + [pl.BlockSpec((tm, LANES), lambda i, j: (i, 0)) for _ in aux],
            out_specs=out_spec,
            out_shape=out_shape,
            compiler_params=_params(("parallel", "arbitrary")),
            name=name,
        )(xb, w, *aux)

    qt = segment("proj_q", _Q_TILE0, 2,
                 lambda acc, c, s: (_rotary(acc, c, s) * _Q_SCALE).T, tables, BF16, True)
    k = segment("proj_k", _K_TILE0, 2, _rotary, tables, BF16, False)
    vt = segment("proj_v", _V_TILE0, 2, lambda acc: acc.T, [], BF16, True)
    cv = segment("proj_conv", _CV_TILE0, 3, lambda acc: acc, [], F32, False)
    qm = segment("proj_mem_q", _QM_TILE, 1, lambda acc: acc, [], BF16, False)
    g = segment("proj_gates", _G_TILE0, 6, jax.nn.sigmoid, [], BF16, False)
    return qt, k, vt, cv, qm, g


def _matmul_kernel(a_ref, b_ref, o_ref):
    o_ref[...] = jnp.dot(a_ref[...], b_ref[...], preferred_element_type=F32).astype(o_ref.dtype)


def _matmul(a, b, tn, out_dtype):
    m, k = a.shape
    n = b.shape[1]
    return pl.pallas_call(
        _matmul_kernel,
        grid=(n // tn,),
        in_specs=[pl.BlockSpec((m, k), lambda j: (0, 0)),
                  pl.BlockSpec((k, tn), lambda j: (0, j))],
        out_specs=pl.BlockSpec((m, tn), lambda j: (0, j)),
        out_shape=jax.ShapeDtypeStruct((m, n), out_dtype),
        compiler_params=_params(("parallel",)),
        name="mem_kv_projection",
    )(a, b)


def _sublane_all(op, x):
    for shift in (4, 2, 1):
        x = op(x, pltpu.roll(x, shift, axis=0))
    return x


ATTN_MAX_EXCESS = 100.0


def _attn_kernel(lam_ref, init_ref, g_ref, qt_ref, k_ref, vt_ref, o_ref,
                 s0_sc, s1_sc, x0_sc, x1_sc, p0_sc, p1_sc, a0_sc, a1_sc, m_sc, l_sc, acc_sc, e_sc):
    s_sc, x_sc, p_sc, a_sc = (s0_sc, s1_sc), (x0_sc, x1_sc), (p0_sc, p1_sc), (a0_sc, a1_sc)
    qi = pl.program_id(2)
    dv, tq = qt_ref.shape
    tk, r = s0_sc.shape
    assert tq == 2 * tk and r == 2 * tq
    qt = qt_ref[...]
    sub = lax.broadcasted_iota(jnp.int32, qt.shape, 0)
    zero = jnp.zeros_like(qt)
    qq = jnp.concatenate([jnp.where(sub < DA_HEAD_DIM, qt, zero),
                          jnp.where(sub >= DA_HEAD_DIM, qt, zero)], axis=1)
    ones = jnp.ones((2 * SUBLANES, tk), vt_ref.dtype)

    def raw_scores(kb, diag):
        start = pl.multiple_of(kb * tk, tk)
        s = jnp.dot(k_ref[pl.ds(start, tk), :], qq, preferred_element_type=F32)
        if diag is not None:
            key = lax.broadcasted_iota(jnp.int32, s.shape, 0) + diag * tk
            qry = lax.broadcasted_iota(jnp.int32, s.shape, 1) & (tq - 1)
            s = jnp.where(key <= qry, s, -jnp.inf)
        return s.reshape(tk // SUBLANES, SUBLANES, r)

    def pv(kb, slot):
        start = pl.multiple_of(kb * tk, tk)
        lhs = jnp.concatenate([vt_ref[:, pl.ds(start, tk)], ones], axis=0)
        res = jnp.dot(lhs, p_sc[slot][...], preferred_element_type=F32)
        alpha = a_sc[slot][...]
        acc3 = acc_sc[...].reshape(dv // SUBLANES, SUBLANES, r) * alpha[None]
        acc_sc[...] = acc3.reshape(dv, r) + res[:dv]
        l_sc[...] = alpha * l_sc[...] + res[dv:dv + SUBLANES]

    def reset(m0):
        m_sc[...] = m0
        l_sc[...] = jnp.zeros(l_sc.shape, F32)
        acc_sc[...] = jnp.zeros(acc_sc.shape, F32)

    d0, d1 = 2 * qi, 2 * qi + 1

    def scores_exp(kb, slot, diag=None):
        s3 = raw_scores(kb, diag)
        x0_sc[...] = jnp.max(s3, axis=0)
        p_sc[slot][...] = jnp.exp2(s3 - m_sc[...][None]).reshape(tk, r).astype(p_sc[slot].dtype)

    def advance(slot):
        bm = _sublane_all(jnp.maximum, x0_sc[...])
        m_prev = m_sc[...]
        e_sc[...] = jnp.maximum(e_sc[...], bm - m_prev)
        m_new = jnp.maximum(m_prev, bm)
        a_sc[slot][...] = jnp.exp2(m_prev - m_new)
        m_sc[...] = m_new

    start = pl.multiple_of(qi * tq, tq)
    own = k_ref[pl.ds(start, tq), :].astype(F32).T * qt.astype(F32)
    seed = jnp.concatenate([jnp.sum(own[:DA_HEAD_DIM], axis=0, keepdims=True),
                            jnp.sum(own[DA_HEAD_DIM:], axis=0, keepdims=True)], axis=1)
    reset(jnp.broadcast_to(seed, m_sc.shape))
    e_sc[...] = jnp.full(e_sc.shape, -jnp.inf, F32)
    a0_sc[...] = jnp.ones(a0_sc.shape, F32)
    scores_exp(d0, 0, diag=0)

    def fast_pair(j, carry):
        advance(1)
        scores_exp(2 * j, 1)
        pv(jnp.where(j == 0, d0, 2 * j - 1), 0)
        advance(0)
        scores_exp(2 * j + 1, 0)
        pv(2 * j, 1)
        return carry

    lax.fori_loop(0, qi, fast_pair, 0)

    cols = (slice(tk, tq), slice(tq + tk, r))

    def d1_scores_exp():
        start = pl.multiple_of(d1 * tk, tk)
        qh = jnp.concatenate([qq[:, c] for c in cols], axis=1)
        s = jnp.dot(k_ref[pl.ds(start, tk), :], qh, preferred_element_type=F32)
        key = lax.broadcasted_iota(jnp.int32, s.shape, 0)
        qry = lax.broadcasted_iota(jnp.int32, s.shape, 1) & (tk - 1)
        s3 = jnp.where(key <= qry, s, -jnp.inf).reshape(tk // SUBLANES, SUBLANES, tq)
        bm = jnp.max(s3, axis=0)
        ninf = jnp.full((SUBLANES, tk), -jnp.inf, F32)
        x0_sc[...] = jnp.concatenate([ninf, bm[:, :tk], ninf, bm[:, tk:]], axis=1)
        mh = jnp.concatenate([m_sc[:, c] for c in cols], axis=1)
        p = jnp.exp2(s3 - mh[None]).reshape(tk, tq).astype(p1_sc.dtype)
        p1_sc[:, cols[0]] = p[:, :tk]
        p1_sc[:, cols[1]] = p[:, tk:]

    def d1_pv():
        start = pl.multiple_of(d1 * tk, tk)
        lhs = jnp.concatenate([vt_ref[:, pl.ds(start, tk)], ones], axis=0)
        ph = jnp.concatenate([p1_sc[:, c] for c in cols], axis=1)
        res = jnp.dot(lhs, ph, preferred_element_type=F32)
        for i, c in enumerate(cols):
            alpha = a1_sc[:, c]
            part = res[:, i * tk:(i + 1) * tk]
            acc3 = acc_sc[:, c].reshape(dv // SUBLANES, SUBLANES, tk) * alpha[None]
            acc_sc[:, c] = acc3.reshape(dv, tk) + part[:dv]
            l_sc[:, c] = alpha * l_sc[:, c] + part[dv:dv + SUBLANES]

    advance(1)
    d1_scores_exp()
    pv(jnp.where(qi == 0, d0, d0 - 1), 0)
    e_sc[...] = jnp.maximum(e_sc[...], _sublane_all(jnp.maximum, x0_sc[...]) - m_sc[...])
    d1_pv()

    def scores(kb, slot, diag=None):
        s3 = raw_scores(kb, diag)
        s_sc[slot][...] = s3.reshape(tk, r)
        x_sc[slot][...] = jnp.max(s3, axis=0)

    def softmax(slot):
        s3 = s_sc[slot][...].reshape(tk // SUBLANES, SUBLANES, r)
        m_prev = m_sc[...]
        m_new = jnp.maximum(m_prev, _sublane_all(jnp.maximum, x_sc[slot][...]))
        p_sc[slot][...] = jnp.exp2(s3 - m_new[None]).reshape(tk, r).astype(p_sc[slot].dtype)
        a_sc[slot][...] = jnp.exp2(m_prev - m_new)
        m_sc[...] = m_new

    def blk(t):
        return jnp.where(t < 2, d0 + t, t - 2)

    @pl.when(jnp.max(e_sc[...]) > ATTN_MAX_EXCESS)
    def _():
        reset(jnp.full(m_sc.shape, -jnp.inf, F32))
        scores(d0, 0, diag=0)
        scores(d1, 1, diag=1)
        softmax(0)

        def pair(j, carry):
            t = 2 * j + 1
            scores(t - 1, 0)
            softmax(1)
            pv(blk(t - 1), 0)
            scores(t, 1)
            softmax(0)
            pv(blk(t), 1)
            return carry

        lax.fori_loop(0, qi, pair, 0)
        n = 2 * qi + 1
        softmax(1)
        pv(blk(n - 1), 0)
        pv(blk(n), 1)

    ot = (acc_sc[...].reshape(dv // SUBLANES, SUBLANES, r) / l_sc[...][None]).reshape(dv, r)
    o = ot[:, :tq] - lam_ref[0:1, 0:1] * ot[:, tq:]
    o = o * lax.rsqrt(jnp.mean(jnp.square(o), axis=0, keepdims=True) + RMS_EPS)
    o = o.T * g_ref[...] * (1.0 - init_ref[...])
    o_ref[...] = o.astype(o_ref.dtype)


def _attention(qt, k, vt, lam, lam_init, subln_g, batch, seq):
    t = k.shape[0]
    tq = min(ATTN_TQ, seq)
    tk = tq // 2
    assert seq % tq == 0 and tq & (tq - 1) == 0
    nq = seq // tq
    h = DA_HEADS
    r = 2 * tq
    row = pl.BlockSpec((1, LANES), lambda b, hh, i: (0, 0))
    return pl.pallas_call(
        _attn_kernel,
        grid=(batch, h, nq),
        in_specs=[row, row, row,
                  pl.BlockSpec((None, DA_V_DIM, tq), lambda b, hh, i: (b, hh, i)),
                  pl.BlockSpec((seq, DA_V_DIM), lambda b, hh, i: (b, hh)),
                  pl.BlockSpec((None, DA_V_DIM, seq), lambda b, hh, i: (b, hh, 0))],
        out_specs=pl.BlockSpec((tq, DA_V_DIM), lambda b, hh, i: (b * nq + i, hh)),
        out_shape=jax.ShapeDtypeStruct((t, h * DA_V_DIM), BF16),
        scratch_shapes=[pltpu.VMEM((tk, r), F32), pltpu.VMEM((tk, r), F32),
                        pltpu.VMEM((SUBLANES, r), F32), pltpu.VMEM((SUBLANES, r), F32),
                        pltpu.VMEM((tk, r), BF16), pltpu.VMEM((tk, r), BF16),
                        pltpu.VMEM((SUBLANES, r), F32), pltpu.VMEM((SUBLANES, r), F32),
                        pltpu.VMEM((SUBLANES, r), F32),
                        pltpu.VMEM((SUBLANES, r), F32),
                        pltpu.VMEM((DA_V_DIM, r), F32),
                        pltpu.VMEM((SUBLANES, r), F32)],
        compiler_params=_params(("parallel", "parallel", "arbitrary")),
        name="diff_attention",
    )(lam, lam_init, subln_g, qt, k, vt)


def _layer_norm(h, g, b):
    mu = jnp.mean(h, axis=-1, keepdims=True)
    d = h - mu
    var = jnp.mean(jnp.square(d), axis=-1, keepdims=True)
    return d * lax.rsqrt(var + LN_EPS) * g + b


def _route(logits):
    lane = lax.broadcasted_iota(jnp.int32, logits.shape, 1)
    valid = lane < N_EXPERTS
    logits = jnp.where(valid, logits, -jnp.inf)
    mx = jnp.max(logits, axis=-1, keepdims=True)
    ex = jnp.exp(logits - mx)
    scores = ex / jnp.sum(ex, axis=-1, keepdims=True)
    group = lane // EXPERTS_PER_GROUP
    neg = jnp.full_like(scores, -1.0)
    best = jnp.max(jnp.where(group == 0, scores, neg), axis=-1, keepdims=True)
    gidx = jnp.zeros(best.shape, jnp.int32)
    for gi in range(1, N_GROUPS):
        cand = jnp.max(jnp.where(group == gi, scores, neg), axis=-1, keepdims=True)
        take = cand > best
        gidx = jnp.where(take, gi, gidx)
        best = jnp.where(take, cand, best)
    vals = jnp.where((group == gidx) & valid, scores, neg)
    big = jnp.full_like(lane, LANES)
    v1 = jnp.max(vals, axis=-1, keepdims=True)
    i1 = jnp.min(jnp.where(vals == v1, lane, big), axis=-1, keepdims=True)
    vals2 = jnp.where(lane == i1, neg, vals)
    v2 = jnp.max(vals2, axis=-1, keepdims=True)
    i2 = jnp.min(jnp.where(vals2 == v2, lane, big), axis=-1, keepdims=True)
    tot = v1 + v2
    comb = jnp.where(lane == i1, v1 / tot, 0.0) + jnp.where(lane == i2, v2 / tot, 0.0)
    return jnp.where(lane == GROUP_LANE, gidx.astype(F32), comb)


def _merge_kernel(ya_ref, cv_ref, cvp_ref, qm_ref, kvm_ref, g_ref, x_ref,
                  wa_ref, wc_ref, wm_ref, wo_ref, cw_ref, lng_ref, lnb_ref, wrh_ref, wrl_ref, rb_ref,
                  xo_ref, xob_ref, comb_ref, *, tiles_per_seq, alpha):
    i = pl.program_id(0)
    tm = ya_ref.shape[0]
    cw = CONV_WIDTH

    z = cv_ref[:, cw:2 * cw] * cv_ref[:, 2 * cw:3 * cw]
    zp = cvp_ref[:, cw:2 * cw] * cvp_ref[:, 2 * cw:3 * cw]
    zp = jnp.where(i % tiles_per_seq == 0, jnp.zeros_like(zp), zp)
    row = lax.broadcasted_iota(jnp.int32, z.shape, 0)
    zm1 = jnp.where(row == 0, zp[SUBLANES - 1:SUBLANES], pltpu.roll(z, 1, axis=0))
    zm2 = jnp.where(row == 0, zp[SUBLANES - 2:SUBLANES - 1],
                    jnp.where(row == 1, zp[SUBLANES - 1:SUBLANES], pltpu.roll(z, 2, axis=0)))
    w = cw_ref[...]
    y = w[0:1] * zm2
    y = y + w[1:2] * zm1
    y = y + w[2:3] * z
    y_conv = (cv_ref[:, 0:cw] * y).astype(BF16)

    outs = []
    for hd in range(MEM_HEADS):
        lo = hd * MEM_HEAD_DIM
        qh = qm_ref[:, lo:lo + MEM_HEAD_DIM]
        kh = kvm_ref[:, lo:lo + MEM_HEAD_DIM]
        vh = kvm_ref[:, MEM_WIDTH + lo:MEM_WIDTH + lo + MEM_HEAD_DIM]
        s = lax.dot_general(qh, kh, (((1,), (1,)), ((), ())),
                            preferred_element_type=F32) * (MEM_HEAD_DIM ** -0.5)
        s = s - jnp.max(s, axis=-1, keepdims=True)
        e = jnp.exp(s)
        p = e / jnp.sum(e, axis=-1, keepdims=True)
        outs.append(jnp.dot(p.astype(BF16), vh, preferred_element_type=F32))
    y_mem = jnp.concatenate(outs, axis=1).astype(BF16)

    d = x_ref.shape[1]
    merged = g_ref[:, 0:d] * jnp.dot(ya_ref[...], wa_ref[...], preferred_element_type=F32)
    merged = merged + g_ref[:, d:2 * d] * jnp.dot(y_conv, wc_ref[...], preferred_element_type=F32)
    merged = merged + g_ref[:, 2 * d:3 * d] * jnp.dot(y_mem, wm_ref[...], preferred_element_type=F32)
    mix = jnp.dot(merged.astype(BF16), wo_ref[...], preferred_element_type=F32)
    x1 = _layer_norm(alpha * x_ref[...] + mix, lng_ref[...], lnb_ref[...])
    xo_ref[...] = x1
    xob_ref[...] = x1.astype(BF16)

    x1h = x1.astype(BF16)
    x1l = (x1 - x1h.astype(F32)).astype(BF16)
    logits = (jnp.dot(x1h, wrh_ref[...], preferred_element_type=F32)
              + (jnp.dot(x1h, wrl_ref[...], preferred_element_type=F32)
                 + jnp.dot(x1l, wrh_ref[...], preferred_element_type=F32))) + rb_ref[...]
    comb_ref[...] = _route(logits)


def _merge(ya, cv, qm, kvm, g, xf, wa, wc, wm, wo, conv_w, ln_g, ln_b, wrh, wrl, rb, seq, mem_len, alpha):
    t, d = xf.shape
    tm = min(MERGE_TM, seq)
    assert seq % tm == 0 and tm % SUBLANES == 0
    tiles_per_seq = seq // tm
    halo = tm // SUBLANES

    def full(a):
        return pl.BlockSpec(a.shape, lambda i: (0,) * a.ndim)

    def rows(width):
        return pl.BlockSpec((tm, width), lambda i: (i, 0))

    kern = functools.partial(_merge_kernel, tiles_per_seq=tiles_per_seq, alpha=alpha)
    return pl.pallas_call(
        kern,
        grid=(t // tm,),
        in_specs=[rows(ya.shape[1]),
                  rows(cv.shape[1]),
                  pl.BlockSpec((SUBLANES, cv.shape[1]), lambda i: (jnp.maximum(i * halo - 1, 0), 0)),
                  rows(qm.shape[1]),
                  pl.BlockSpec((mem_len, kvm.shape[1]), lambda i: (i // tiles_per_seq, 0)),
                  rows(g.shape[1]),
                  rows(d),
                  full(wa), full(wc), full(wm), full(wo), full(conv_w),
                  full(ln_g), full(ln_b), full(wrh), full(wrl), full(rb)],
        out_specs=[rows(d), rows(d), rows(LANES)],
        out_shape=[jax.ShapeDtypeStruct((t, d), F32),
                   jax.ShapeDtypeStruct((t, d), BF16),
                   jax.ShapeDtypeStruct((t, LANES), F32)],
        compiler_params=_params(("parallel",)),
        name="merge_ln_router",
    )(ya, cv, cv, qm, kvm, g, xf, wa, wc, wm, wo, conv_w, ln_g, ln_b, wrh, wrl, rb)


def _moe_kernel(xb_ref, xf_ref, comb_ref, tri_ref, wg_ref, wu_ref, wd_ref, lng_ref, lnb_ref,
                xo_ref, xob_ref,
                acc_sc, pos_sc, posrow_sc, grow_sc, cnt_sc, xg_sc, wts_sc, y_sc, *, alpha, chunk):
    e = pl.program_id(1)
    grp = e // EXPERTS_PER_GROUP
    k = e % EXPERTS_PER_GROUP
    tm = xb_ref.shape[0]
    lane = lax.broadcasted_iota(jnp.int32, (tm, LANES), 1)

    @pl.when(e == 0)
    def _():
        acc_sc[...] = jnp.zeros(acc_sc.shape, F32)
        gidx = jnp.sum(jnp.where(lane == GROUP_LANE, comb_ref[...], 0.0), axis=1, keepdims=True)
        member = (lane.astype(F32) == gidx).astype(BF16)
        before = jnp.dot(tri_ref[...], member, preferred_element_type=F32)
        pos_sc[...] = jnp.sum(jnp.where(lane.astype(F32) == gidx, before, 0.0), axis=1, keepdims=True)
        cnt_sc[...] = jnp.sum(member.astype(F32), axis=0, keepdims=True)
        grow = comb_ref[...].T[GROUP_LANE:GROUP_LANE + 1]
        sub = lax.broadcasted_iota(jnp.int32, (LANES, tm), 0).astype(F32)
        posrow_sc[...] = jnp.sum(jnp.where(sub == grow, before.T, 0.0), axis=0, keepdims=True)
        grow_sc[...] = grow

    cnt = jnp.sum(jnp.where(lane[0:1] == grp, cnt_sc[...], 0.0)).astype(jnp.int32)
    n_chunks = (cnt + (chunk - 1)) // chunk

    def onehot(c):
        gidx = jnp.sum(jnp.where(lane == GROUP_LANE, comb_ref[...], 0.0), axis=1, keepdims=True)
        slot = jnp.where(gidx == grp.astype(F32), pos_sc[...], -1.0) - (c * chunk).astype(F32)
        col = lax.broadcasted_iota(jnp.int32, (tm, chunk), 1).astype(F32)
        return (col == slot).astype(BF16)

    def onehot_rows(c):
        slot = (jnp.where(grow_sc[...] == grp.astype(F32), posrow_sc[...], -1.0)
                - (c * chunk).astype(F32))
        row = lax.broadcasted_iota(jnp.int32, (chunk, tm), 0).astype(F32)
        return (row == slot).astype(BF16)

    def rows(c):
        return pl.ds(pl.multiple_of(c * chunk, chunk), chunk)

    @pl.when(k == 0)
    def _():
        comb = jnp.where(lane < N_EXPERTS, comb_ref[...], 0.0)
        comb_hi = comb.astype(BF16)
        comb_lo = (comb - comb_hi.astype(F32)).astype(BF16)

        def gather(c, carry):
            sel = onehot_rows(c)
            xg_sc[rows(c), :] = jnp.dot(sel, xb_ref[...], preferred_element_type=F32).astype(BF16)
            wts_sc[rows(c), :] = (jnp.dot(sel, comb_hi, preferred_element_type=F32)
                                  + jnp.dot(sel, comb_lo, preferred_element_type=F32))
            y_sc[rows(c), :] = jnp.zeros((chunk, y_sc.shape[1]), F32)
            return carry

        lax.fori_loop(0, n_chunks, gather, 0)

    def ffn(c, carry):
        xg = xg_sc[rows(c), :]
        hg = jnp.dot(xg, wg_ref[...], preferred_element_type=F32)
        hu = jnp.dot(xg, wu_ref[...], preferred_element_type=F32)
        h = (hg * jax.nn.sigmoid(hg)) * hu
        y = jnp.dot(h.astype(BF16), wd_ref[...], preferred_element_type=F32)
        wts = wts_sc[rows(c), :]
        wl = lax.broadcasted_iota(jnp.int32, wts.shape, 1)
        w = jnp.sum(jnp.where(wl == e, wts, 0.0), axis=1, keepdims=True)
        y_sc[rows(c), :] += jnp.where(w != 0.0, w * y, 0.0)
        return carry

    lax.fori_loop(0, n_chunks, ffn, 0)

    @pl.when(k == EXPERTS_PER_GROUP - 1)
    def _():
        def scatter(c, carry):
            acc_sc[...] += jnp.dot(onehot(c), y_sc[rows(c), :].astype(BF16),
                                   preferred_element_type=F32)
            return carry

        lax.fori_loop(0, n_chunks, scatter, 0)

    @pl.when(e == pl.num_programs(1) - 1)
    def _():
        x2 = _layer_norm(alpha * xf_ref[...] + acc_sc[...], lng_ref[...], lnb_ref[...])
        xo_ref[...] = x2
        xob_ref[...] = x2.astype(BF16)


def _moe(xb, xf, comb, wg, wu, wd, ln_g, ln_b, alpha):
    t, d = xf.shape
    tm = min(MOE_TM, t)
    assert t % tm == 0
    n_e, _, dff = wg.shape
    chunk = min(MOE_CHUNK, tm)
    cap = -(-tm // chunk) * chunk
    tri = jnp.tril(jnp.ones((tm, tm), BF16), -1)
    rows = lambda width: pl.BlockSpec((tm, width), lambda i, e: (i, 0))
    vec = pl.BlockSpec((1, d), lambda i, e: (0, 0))
    return pl.pallas_call(
        functools.partial(_moe_kernel, alpha=alpha, chunk=chunk),
        grid=(t // tm, n_e),
        in_specs=[rows(d), rows(d), rows(LANES),
                  pl.BlockSpec((tm, tm), lambda i, e: (0, 0)),
                  pl.BlockSpec((None, d, dff), lambda i, e: (e, 0, 0)),
                  pl.BlockSpec((None, d, dff), lambda i, e: (e, 0, 0)),
                  pl.BlockSpec((None, dff, d), lambda i, e: (e, 0, 0)),
                  vec, vec],
        out_specs=[rows(d), rows(d)],
        out_shape=[jax.ShapeDtypeStruct((t, d), F32),
                   jax.ShapeDtypeStruct((t, d), BF16)],
        scratch_shapes=[pltpu.VMEM((tm, d), F32),
                        pltpu.VMEM((tm, 1), F32),
                        pltpu.VMEM((1, tm), F32),
                        pltpu.VMEM((1, tm), F32),
                        pltpu.VMEM((1, LANES), F32),
                        pltpu.VMEM((cap, d), BF16),
                        pltpu.VMEM((cap, LANES), F32),
                        pltpu.VMEM((cap, d), F32)],
        compiler_params=_params(("parallel", "arbitrary")),
        name="moe_grouped",
    )(xb, xf, comb, tri, wg, wu, wd, ln_g, ln_b)


def kernel(x, mem, positions, w_in, lambda_q1, lambda_k1, lambda_q2, lambda_k2, diff_subln_g, conv_w, w_mem_kv, w_br_attn, w_br_conv, w_br_mem, w_out, ln1_g, ln1_b, ln2_g, ln2_b, w_router, router_bias, w_exp_gate, w_exp_up, w_exp_down):
    batch, seq, d = x.shape
    depth = w_in.shape[0]
    mem_len = mem.shape[1]
    t = batch * seq
    alpha = (2 * depth) ** 0.25

    cos_t, sin_t = _rope_tables(positions, min(PROJ_TM, t))
    lam_init = [0.8 - 0.6 * math.exp(-0.3 * l) for l in range(depth)]
    lam = _lambdas(lambda_q1, lambda_k1, lambda_q2, lambda_k2, lam_init)
    init_rows = jnp.broadcast_to(jnp.asarray(lam_init, F32).reshape(depth, 1), (depth, LANES))

    wr = jnp.zeros((d, LANES), F32).at[:, :N_EXPERTS].set(w_router.astype(F32))
    rb = jnp.zeros((1, LANES), F32).at[0, :N_EXPERTS].set(router_bias.astype(F32))
    wrh = wr.astype(BF16)
    wrl = (wr - wrh.astype(F32)).astype(BF16)
    mem_b = mem.reshape(batch * mem_len, d).astype(BF16)

    xf = x.reshape(t, d).astype(F32)
    xb = xf.astype(BF16)
    for l in range(depth):
        qt, k, vt, cv, qm, g = _projection(xb, w_in[l].astype(BF16), cos_t, sin_t, batch, seq)
        ya = _attention(qt, k, vt, lam[l:l + 1], init_rows[l:l + 1],
                        diff_subln_g[l].astype(F32).reshape(1, DA_V_DIM), batch, seq)
        kvm = _matmul(mem_b, w_mem_kv[l].astype(BF16), MEM_WIDTH, BF16)
        xf, xb, comb = _merge(ya, cv, qm, kvm, g, xf,
                              w_br_attn[l].astype(BF16), w_br_conv[l].astype(BF16),
                              w_br_mem[l].astype(BF16), w_out[l].astype(BF16),
                              conv_w[l].astype(F32),
                              ln1_g[l].astype(F32).reshape(1, d), ln1_b[l].astype(F32).reshape(1, d),
                              wrh, wrl, rb, seq, mem_len, alpha)
        xf, xb = _moe(xb, xf, comb, w_exp_gate[l].astype(BF16), w_exp_up[l].astype(BF16),
                      w_exp_down[l].astype(BF16),
                      ln2_g[l].astype(F32).reshape(1, d), ln2_b[l].astype(F32).reshape(1, d), alpha)
    return xf.reshape(batch, seq, d).astype(x.dtype)
```

```python
import functools
import math

import jax
import jax.numpy as jnp
from jax import lax
from jax.experimental import pallas as pl
from jax.experimental.pallas import tpu as pltpu

F32 = jnp.float32
BF16 = jnp.bfloat16

DA_HEADS = 8
DA_HEAD_DIM = 64
DA_V_DIM = 2 * DA_HEAD_DIM
CONV_WIDTH = 512
CONV_K = 3
MEM_HEADS = 4
MEM_HEAD_DIM = 128
MEM_WIDTH = MEM_HEADS * MEM_HEAD_DIM
N_BRANCH = 3
ROPE_THETA = 10000.0
N_EXPERTS = 16
N_GROUPS = 4
GROUP_LANE = 16
EXPERTS_PER_GROUP = N_EXPERTS // N_GROUPS
LN_EPS = 1e-5
RMS_EPS = 1e-5

LANES = 128
SUBLANES = 8
VMEM_LIMIT_BYTES = 56 * 1024 * 1024

PROJ_TM = 2048
PROJ_TN = 512
ATTN_TQ = 1024
MERGE_TM = 512
MOE_TM = 1024
MOE_CHUNK = 288


def _params(semantics, flags=None):
    return pltpu.CompilerParams(dimension_semantics=semantics,
                                vmem_limit_bytes=VMEM_LIMIT_BYTES, flags=flags)


def _rope_table_kernel(pos_ref, inv_ref, cos_ref, sin_ref):
    ang = pos_ref[...].astype(F32) * inv_ref[...]
    lane = lax.broadcasted_iota(jnp.int32, ang.shape, 1)
    first_half = (lane % DA_HEAD_DIM) < (DA_HEAD_DIM // 2)
    s = jnp.sin(ang)
    cos_ref[...] = jnp.cos(ang)
    sin_ref[...] = jnp.where(first_half, -s, s)


def _rope_tables(positions, tm):
    t = positions.size
    half = DA_HEAD_DIM // 2
    inv_freq = ROPE_THETA ** (-jnp.arange(0, DA_HEAD_DIM, 2, dtype=F32) / DA_HEAD_DIM)
    inv_lanes = jnp.tile(inv_freq, LANES // half).reshape(1, LANES)
    pos = positions.reshape(t, 1)
    return pl.pallas_call(
        _rope_table_kernel,
        grid=(t // tm,),
        in_specs=[pl.BlockSpec((tm, 1), lambda i: (i, 0)),
                  pl.BlockSpec((1, LANES), lambda i: (0, 0))],
        out_specs=[pl.BlockSpec((tm, LANES), lambda i: (i, 0)),
                   pl.BlockSpec((tm, LANES), lambda i: (i, 0))],
        out_shape=[jax.ShapeDtypeStruct((t, LANES), F32),
                   jax.ShapeDtypeStruct((t, LANES), F32)],
        compiler_params=_params(("parallel",)),
        name="rope_tables",
    )(pos, inv_lanes)


def _lambda_kernel(q1_ref, k1_ref, q2_ref, k2_ref, init_ref, lam_ref):
    a = jnp.sum(q1_ref[...] * k1_ref[...], axis=-1, keepdims=True)
    b = jnp.sum(q2_ref[...] * k2_ref[...], axis=-1, keepdims=True)
    lam_ref[...] = jnp.exp(a) - jnp.exp(b) + init_ref[...]


def _lambdas(lq1, lk1, lq2, lk2, lam_init):
    depth = lq1.shape[0]
    init = jnp.broadcast_to(jnp.asarray(lam_init, F32).reshape(depth, 1), (depth, LANES))
    return pl.pallas_call(
        _lambda_kernel,
        out_shape=jax.ShapeDtypeStruct((depth, LANES), F32),
        name="diff_lambda",
    )(lq1.astype(F32), lk1.astype(F32), lq2.astype(F32), lk2.astype(F32), init)


_Q_TILE0, _K_TILE0, _V_TILE0, _CV_TILE0, _QM_TILE, _G_TILE0, _N_TILES = 0, 2, 4, 6, 9, 10, 16
_Q_SCALE = (DA_HEAD_DIM ** -0.5) * math.log2(math.e)


def _rotary(acc, cos_ref, sin_ref):
    tn = acc.shape[1]
    reps = tn // LANES
    c = jnp.concatenate([cos_ref[...]] * reps, axis=1)
    s = jnp.concatenate([sin_ref[...]] * reps, axis=1)
    lane = lax.broadcasted_iota(jnp.int32, acc.shape, 1)
    half = DA_HEAD_DIM // 2
    first_half = (lane % DA_HEAD_DIM) < half
    partner = jnp.where(first_half,
                        pltpu.roll(acc, tn - half, axis=1),
                        pltpu.roll(acc, half, axis=1))
    return acc * c + partner * s


def _proj_segment_kernel(x_ref, w_ref, *refs, epilogue):
    *aux, o_ref = refs
    acc = jnp.dot(x_ref[...], w_ref[...], preferred_element_type=F32)
    o_ref[...] = epilogue(acc, *aux).astype(o_ref.dtype)


def _projection(xb, w, cos_t, sin_t, batch, seq):
    t, d = xb.shape
    tm, tn = min(PROJ_TM, seq), PROJ_TN
    assert seq % tm == 0 and w.shape == (d, _N_TILES * tn)
    tps = seq // tm
    tables = [cos_t, sin_t]

    def segment(name, tile0, n_tiles, epilogue, aux, out_dtype, transposed, wide=1):
        assert tile0 % wide == 0 and n_tiles % wide == 0
        tw, steps, first = tn * wide, n_tiles // wide, tile0 // wide
        if transposed:
            out_spec = pl.BlockSpec((None, tw, tm), lambda i, j: (i // tps, j, i % tps))
            out_shape = jax.ShapeDtypeStruct((batch, n_tiles * tn, seq), out_dtype)
        else:
            out_spec = pl.BlockSpec((tm, tw), lambda i, j: (i, j))
            out_shape = jax.ShapeDtypeStruct((t, n_tiles * tn), out_dtype)
        return pl.pallas_call(
            functools.partial(_proj_segment_kernel, epilogue=epilogue),
            grid=(t // tm, steps),
            in_specs=[pl.BlockSpec((tm, d), lambda i, j: (i, 0)),
                      pl.BlockSpec((d, tw), lambda i, j: (0, first + j))]
                     + [pl.BlockSpec((tm, LANES), lambda i, j: (i, 0)) for _ in aux],
            out_specs=out_spec,
            out_shape=out_shape,
            compiler_params=_params(("parallel", "arbitrary")),
            name=name,
        )(xb, w, *aux)

    qt = segment("proj_q", _Q_TILE0, 2,
                 lambda acc, c, s: (_rotary(acc, c, s) * _Q_SCALE).T, tables, BF16, True)
    k = segment("proj_k", _K_TILE0, 2, _rotary, tables, BF16, False)
    vt = segment("proj_v", _V_TILE0, 2, lambda acc: acc.T, [], BF16, True, wide=2)
    cv = segment("proj_conv", _CV_TILE0, 3, lambda acc: acc, [], F32, False)
    qm = segment("proj_mem_q", _QM_TILE, 1, lambda acc: acc, [], BF16, False)
    g = segment("proj_gates", _G_TILE0, 6, jax.nn.sigmoid, [], BF16, False, wide=2)
    return qt, k, vt, cv, qm, g


def _matmul_kernel(a_ref, b_ref, o_ref):
    o_ref[...] = jnp.dot(a_ref[...], b_ref[...], preferred_element_type=F32).astype(o_ref.dtype)


def _matmul(a, b, tn, out_dtype):
    m, k = a.shape
    n = b.shape[1]
    return pl.pallas_call(
        _matmul_kernel,
        grid=(n // tn,),
        in_specs=[pl.BlockSpec((m, k), lambda j: (0, 0)),
                  pl.BlockSpec((k, tn), lambda j: (0, j))],
        out_specs=pl.BlockSpec((m, tn), lambda j: (0, j)),
        out_shape=jax.ShapeDtypeStruct((m, n), out_dtype),
        compiler_params=_params(("parallel",)),
        name="mem_kv_projection",
    )(a, b)


def _sublane_all(op, x):
    for shift in (4, 2, 1):
        x = op(x, pltpu.roll(x, shift, axis=0))
    return x


ATTN_MAX_EXCESS = 100.0


def _attn_kernel(lam_ref, init_ref, g_ref, qt_ref, k_ref, vt_ref, o_ref,
                 s0_sc, s1_sc, x0_sc, x1_sc, p0_sc, p1_sc, a0_sc, a1_sc, m_sc, l_sc, acc_sc, e_sc):
    s_sc, x_sc, p_sc, a_sc = (s0_sc, s1_sc), (x0_sc, x1_sc), (p0_sc, p1_sc), (a0_sc, a1_sc)
    qi = pl.program_id(2)
    dv, tq = qt_ref.shape
    tk, r = s0_sc.shape
    assert tq == 2 * tk and r == 2 * tq
    qt = qt_ref[...]
    sub = lax.broadcasted_iota(jnp.int32, qt.shape, 0)
    zero = jnp.zeros_like(qt)
    qq = jnp.concatenate([jnp.where(sub < DA_HEAD_DIM, qt, zero),
                          jnp.where(sub >= DA_HEAD_DIM, qt, zero)], axis=1)
    ones = jnp.ones((2 * SUBLANES, tk), vt_ref.dtype)

    def raw_scores(kb, diag):
        start = pl.multiple_of(kb * tk, tk)
        s = jnp.dot(k_ref[pl.ds(start, tk), :], qq, preferred_element_type=F32)
        if diag is not None:
            key = lax.broadcasted_iota(jnp.int32, s.shape, 0) + diag * tk
            qry = lax.broadcasted_iota(jnp.int32, s.shape, 1) & (tq - 1)
            s = jnp.where(key <= qry, s, -jnp.inf)
        return s.reshape(tk // SUBLANES, SUBLANES, r)

    def pv(kb, slot):
        start = pl.multiple_of(kb * tk, tk)
        lhs = jnp.concatenate([vt_ref[:, pl.ds(start, tk)], ones], axis=0)
        res = jnp.dot(lhs, p_sc[slot][...], preferred_element_type=F32)
        alpha = a_sc[slot][...]
        acc3 = acc_sc[...].reshape(dv // SUBLANES, SUBLANES, r) * alpha[None]
        acc_sc[...] = acc3.reshape(dv, r) + res[:dv]
        l_sc[...] = alpha * l_sc[...] + res[dv:dv + SUBLANES]

    def reset(m0):
        m_sc[...] = m0
        l_sc[...] = jnp.zeros(l_sc.shape, F32)
        acc_sc[...] = jnp.zeros(acc_sc.shape, F32)

    d0, d1 = 2 * qi, 2 * qi + 1

    def scores_exp(kb, slot, diag=None):
        s3 = raw_scores(kb, diag)
        x0_sc[...] = jnp.max(s3, axis=0)
        p_sc[slot][...] = jnp.exp2(s3 - m_sc[...][None]).reshape(tk, r).astype(p_sc[slot].dtype)

    def advance(slot):
        bm = _sublane_all(jnp.maximum, x0_sc[...])
        m_prev = m_sc[...]
        e_sc[...] = jnp.maximum(e_sc[...], bm - m_prev)
        m_new = jnp.maximum(m_prev, bm)
        a_sc[slot][...] = jnp.exp2(m_prev - m_new)
        m_sc[...] = m_new

    start = pl.multiple_of(qi * tq, tq)
    own = k_ref[pl.ds(start, tq), :].astype(F32).T * qt.astype(F32)
    seed = jnp.concatenate([jnp.sum(own[:DA_HEAD_DIM], axis=0, keepdims=True),
                            jnp.sum(own[DA_HEAD_DIM:], axis=0, keepdims=True)], axis=1)
    reset(jnp.broadcast_to(seed, m_sc.shape))
    e_sc[...] = jnp.full(e_sc.shape, -jnp.inf, F32)
    a0_sc[...] = jnp.ones(a0_sc.shape, F32)
    scores_exp(d0, 0, diag=0)

    def fast_pair(j):
        advance(1)
        scores_exp(2 * j, 1)
        pv(jnp.where(j == 0, d0, 2 * j - 1), 0)
        advance(0)
        scores_exp(2 * j + 1, 0)
        pv(2 * j, 1)

    def fast_quad(i, carry):
        fast_pair(2 * i)
        fast_pair(2 * i + 1)
        return carry

    lax.fori_loop(0, qi // 2, fast_quad, 0)

    @pl.when(qi % 2 == 1)
    def _():
        fast_pair(qi - 1)

    cols = (slice(tk, tq), slice(tq + tk, r))

    def d1_scores_exp():
        start = pl.multiple_of(d1 * tk, tk)
        qh = jnp.concatenate([qq[:, c] for c in cols], axis=1)
        s = jnp.dot(k_ref[pl.ds(start, tk), :], qh, preferred_element_type=F32)
        key = lax.broadcasted_iota(jnp.int32, s.shape, 0)
        qry = lax.broadcasted_iota(jnp.int32, s.shape, 1) & (tk - 1)
        s3 = jnp.where(key <= qry, s, -jnp.inf).reshape(tk // SUBLANES, SUBLANES, tq)
        bm = jnp.max(s3, axis=0)
        ninf = jnp.full((SUBLANES, tk), -jnp.inf, F32)
        x0_sc[...] = jnp.concatenate([ninf, bm[:, :tk], ninf, bm[:, tk:]], axis=1)
        mh = jnp.concatenate([m_sc[:, c] for c in cols], axis=1)
        p = jnp.exp2(s3 - mh[None]).reshape(tk, tq).astype(p1_sc.dtype)
        p1_sc[:, cols[0]] = p[:, :tk]
        p1_sc[:, cols[1]] = p[:, tk:]

    def d1_pv():
        start = pl.multiple_of(d1 * tk, tk)
        lhs = jnp.concatenate([vt_ref[:, pl.ds(start, tk)], ones], axis=0)
        ph = jnp.concatenate([p1_sc[:, c] for c in cols], axis=1)
        res = jnp.dot(lhs, ph, preferred_element_type=F32)
        for i, c in enumerate(cols):
            alpha = a1_sc[:, c]
            part = res[:, i * tk:(i + 1) * tk]
            acc3 = acc_sc[:, c].reshape(dv // SUBLANES, SUBLANES, tk) * alpha[None]
            acc_sc[:, c] = acc3.reshape(dv, tk) + part[:dv]
            l_sc[:, c] = alpha * l_sc[:, c] + part[dv:dv + SUBLANES]

    advance(1)
    d1_scores_exp()
    pv(jnp.where(qi == 0, d0, d0 - 1), 0)
    e_sc[...] = jnp.maximum(e_sc[...], _sublane_all(jnp.maximum, x0_sc[...]) - m_sc[...])
    d1_pv()

    def scores(kb, slot, diag=None):
        s3 = raw_scores(kb, diag)
        s_sc[slot][...] = s3.reshape(tk, r)
        x_sc[slot][...] = jnp.max(s3, axis=0)

    def softmax(slot):
        s3 = s_sc[slot][...].reshape(tk // SUBLANES, SUBLANES, r)
        m_prev = m_sc[...]
        m_new = jnp.maximum(m_prev, _sublane_all(jnp.maximum, x_sc[slot][...]))
        p_sc[slot][...] = jnp.exp2(s3 - m_new[None]).reshape(tk, r).astype(p_sc[slot].dtype)
        a_sc[slot][...] = jnp.exp2(m_prev - m_new)
        m_sc[...] = m_new

    def blk(t):
        return jnp.where(t < 2, d0 + t, t - 2)

    @pl.when(jnp.max(e_sc[...]) > ATTN_MAX_EXCESS)
    def _():
        reset(jnp.full(m_sc.shape, -jnp.inf, F32))
        scores(d0, 0, diag=0)
        scores(d1, 1, diag=1)
        softmax(0)

        def pair(j, carry):
            t = 2 * j + 1
            scores(t - 1, 0)
            softmax(1)
            pv(blk(t - 1), 0)
            scores(t, 1)
            softmax(0)
            pv(blk(t), 1)
            return carry

        lax.fori_loop(0, qi, pair, 0)
        n = 2 * qi + 1
        softmax(1)
        pv(blk(n - 1), 0)
        pv(blk(n), 1)

    ot = (acc_sc[...].reshape(dv // SUBLANES, SUBLANES, r) / l_sc[...][None]).reshape(dv, r)
    o = ot[:, :tq] - lam_ref[0:1, 0:1] * ot[:, tq:]
    o = o * lax.rsqrt(jnp.mean(jnp.square(o), axis=0, keepdims=True) + RMS_EPS)
    o = o.T * g_ref[...] * (1.0 - init_ref[...])
    o_ref[...] = o.astype(o_ref.dtype)


def _attention(qt, k, vt, lam, lam_init, subln_g, batch, seq):
    t = k.shape[0]
    tq = min(ATTN_TQ, seq)
    tk = tq // 2
    assert seq % tq == 0 and tq & (tq - 1) == 0
    nq = seq // tq
    h = DA_HEADS
    r = 2 * tq
    row = pl.BlockSpec((1, LANES), lambda b, hh, i: (0, 0))
    return pl.pallas_call(
        _attn_kernel,
        grid=(batch, h, nq),
        in_specs=[row, row, row,
                  pl.BlockSpec((None, DA_V_DIM, tq), lambda b, hh, i: (b, hh, i)),
                  pl.BlockSpec((seq, DA_V_DIM), lambda b, hh, i: (b, hh)),
                  pl.BlockSpec((None, DA_V_DIM, seq), lambda b, hh, i: (b, hh, 0))],
        out_specs=pl.BlockSpec((tq, DA_V_DIM), lambda b, hh, i: (b * nq + i, hh)),
        out_shape=jax.ShapeDtypeStruct((t, h * DA_V_DIM), BF16),
        scratch_shapes=[pltpu.VMEM((tk, r), F32), pltpu.VMEM((tk, r), F32),
                        pltpu.VMEM((SUBLANES, r), F32), pltpu.VMEM((SUBLANES, r), F32),
                        pltpu.VMEM((tk, r), BF16), pltpu.VMEM((tk, r), BF16),
                        pltpu.VMEM((SUBLANES, r), F32), pltpu.VMEM((SUBLANES, r), F32),
                        pltpu.VMEM((SUBLANES, r), F32),
                        pltpu.VMEM((SUBLANES, r), F32),
                        pltpu.VMEM((DA_V_DIM, r), F32),
                        pltpu.VMEM((SUBLANES, r), F32)],
        compiler_params=_params(("parallel", "parallel", "arbitrary")),
        name="diff_attention",
    )(lam, lam_init, subln_g, qt, k, vt)


def _layer_norm(h, g, b):
    mu = jnp.mean(h, axis=-1, keepdims=True)
    d = h - mu
    var = jnp.mean(jnp.square(d), axis=-1, keepdims=True)
    return d * lax.rsqrt(var + LN_EPS) * g + b


def _route(logits):
    lane = lax.broadcasted_iota(jnp.int32, logits.shape, 1)
    valid = lane < N_EXPERTS
    logits = jnp.where(valid, logits, -jnp.inf)
    mx = jnp.max(logits, axis=-1, keepdims=True)
    ex = jnp.exp(logits - mx)
    scores = ex / jnp.sum(ex, axis=-1, keepdims=True)
    group = lane // EXPERTS_PER_GROUP
    neg = jnp.full_like(scores, -1.0)
    best = jnp.max(jnp.where(group == 0, scores, neg), axis=-1, keepdims=True)
    gidx = jnp.zeros(best.shape, jnp.int32)
    for gi in range(1, N_GROUPS):
        cand = jnp.max(jnp.where(group == gi, scores, neg), axis=-1, keepdims=True)
        take = cand > best
        gidx = jnp.where(take, gi, gidx)
        best = jnp.where(take, cand, best)
    vals = jnp.where((group == gidx) & valid, scores, neg)
    big = jnp.full_like(lane, LANES)
    v1 = jnp.max(vals, axis=-1, keepdims=True)
    i1 = jnp.min(jnp.where(vals == v1, lane, big), axis=-1, keepdims=True)
    vals2 = jnp.where(lane == i1, neg, vals)
    v2 = jnp.max(vals2, axis=-1, keepdims=True)
    i2 = jnp.min(jnp.where(vals2 == v2, lane, big), axis=-1, keepdims=True)
    tot = v1 + v2
    comb = jnp.where(lane == i1, v1 / tot, 0.0) + jnp.where(lane == i2, v2 / tot, 0.0)
    return jnp.where(lane == GROUP_LANE, gidx.astype(F32), comb)


def _merge_kernel(ya_ref, cv_ref, cvp_ref, qm_ref, kvm_ref, g_ref, x_ref,
                  wa_ref, wc_ref, wm_ref, wo_ref, cw_ref, lng_ref, lnb_ref, wrh_ref, wrl_ref, rb_ref,
                  xo_ref, xob_ref, comb_ref, *, tiles_per_seq, alpha):
    i = pl.program_id(0)
    tm = ya_ref.shape[0]
    cw = CONV_WIDTH

    z = cv_ref[:, cw:2 * cw] * cv_ref[:, 2 * cw:3 * cw]
    zp = cvp_ref[:, cw:2 * cw] * cvp_ref[:, 2 * cw:3 * cw]
    zp = jnp.where(i % tiles_per_seq == 0, jnp.zeros_like(zp), zp)
    row = lax.broadcasted_iota(jnp.int32, z.shape, 0)
    zm1 = jnp.where(row == 0, zp[SUBLANES - 1:SUBLANES], pltpu.roll(z, 1, axis=0))
    zm2 = jnp.where(row == 0, zp[SUBLANES - 2:SUBLANES - 1],
                    jnp.where(row == 1, zp[SUBLANES - 1:SUBLANES], pltpu.roll(z, 2, axis=0)))
    w = cw_ref[...]
    y = w[0:1] * zm2
    y = y + w[1:2] * zm1
    y = y + w[2:3] * z
    y_conv = (cv_ref[:, 0:cw] * y).astype(BF16)

    outs = []
    for hd in range(MEM_HEADS):
        lo = hd * MEM_HEAD_DIM
        qh = qm_ref[:, lo:lo + MEM_HEAD_DIM]
        kh = kvm_ref[:, lo:lo + MEM_HEAD_DIM]
        vh = kvm_ref[:, MEM_WIDTH + lo:MEM_WIDTH + lo + MEM_HEAD_DIM]
        s = lax.dot_general(qh, kh, (((1,), (1,)), ((), ())),
                            preferred_element_type=F32) * (MEM_HEAD_DIM ** -0.5)
        s = s - jnp.max(s, axis=-1, keepdims=True)
        e = jnp.exp(s)
        p = e / jnp.sum(e, axis=-1, keepdims=True)
        outs.append(jnp.dot(p.astype(BF16), vh, preferred_element_type=F32))
    y_mem = jnp.concatenate(outs, axis=1).astype(BF16)

    d = x_ref.shape[1]
    merged = g_ref[:, 0:d] * jnp.dot(ya_ref[...], wa_ref[...], preferred_element_type=F32)
    merged = merged + g_ref[:, d:2 * d] * jnp.dot(y_conv, wc_ref[...], preferred_element_type=F32)
    merged = merged + g_ref[:, 2 * d:3 * d] * jnp.dot(y_mem, wm_ref[...], preferred_element_type=F32)
    mix = jnp.dot(merged.astype(BF16), wo_ref[...], preferred_element_type=F32)
    x1 = _layer_norm(alpha * x_ref[...] + mix, lng_ref[...], lnb_ref[...])
    xo_ref[...] = x1
    xob_ref[...] = x1.astype(BF16)

    x1h = x1.astype(BF16)
    x1l = (x1 - x1h.astype(F32)).astype(BF16)
    logits = (jnp.dot(x1h, wrh_ref[...], preferred_element_type=F32)
              + (jnp.dot(x1h, wrl_ref[...], preferred_element_type=F32)
                 + jnp.dot(x1l, wrh_ref[...], preferred_element_type=F32))) + rb_ref[...]
    comb_ref[...] = _route(logits)


def _merge(ya, cv, qm, kvm, g, xf, wa, wc, wm, wo, conv_w, ln_g, ln_b, wrh, wrl, rb, seq, mem_len, alpha):
    t, d = xf.shape
    tm = min(MERGE_TM, seq)
    assert seq % tm == 0 and tm % SUBLANES == 0
    tiles_per_seq = seq // tm
    halo = tm // SUBLANES

    def full(a):
        return pl.BlockSpec(a.shape, lambda i: (0,) * a.ndim)

    def rows(width):
        return pl.BlockSpec((tm, width), lambda i: (i, 0))

    kern = functools.partial(_merge_kernel, tiles_per_seq=tiles_per_seq, alpha=alpha)
    return pl.pallas_call(
        kern,
        grid=(t // tm,),
        in_specs=[rows(ya.shape[1]),
                  rows(cv.shape[1]),
                  pl.BlockSpec((SUBLANES, cv.shape[1]), lambda i: (jnp.maximum(i * halo - 1, 0), 0)),
                  rows(qm.shape[1]),
                  pl.BlockSpec((mem_len, kvm.shape[1]), lambda i: (i // tiles_per_seq, 0)),
                  rows(g.shape[1]),
                  rows(d),
                  full(wa), full(wc), full(wm), full(wo), full(conv_w),
                  full(ln_g), full(ln_b), full(wrh), full(wrl), full(rb)],
        out_specs=[rows(d), rows(d), rows(LANES)],
        out_shape=[jax.ShapeDtypeStruct((t, d), F32),
                   jax.ShapeDtypeStruct((t, d), BF16),
                   jax.ShapeDtypeStruct((t, LANES), F32)],
        compiler_params=_params(("parallel",)),
        name="merge_ln_router",
    )(ya, cv, cv, qm, kvm, g, xf, wa, wc, wm, wo, conv_w, ln_g, ln_b, wrh, wrl, rb)


def _moe_kernel(xb_ref, xf_ref, comb_ref, tri_ref, wg_ref, wu_ref, wd_ref, lng_ref, lnb_ref,
                xo_ref, xob_ref,
                acc_sc, pos_sc, posrow_sc, grow_sc, cnt_sc, xg_sc, wts_sc, y_sc, *, alpha, chunk):
    e = pl.program_id(1)
    grp = e // EXPERTS_PER_GROUP
    k = e % EXPERTS_PER_GROUP
    tm = xb_ref.shape[0]
    lane = lax.broadcasted_iota(jnp.int32, (tm, LANES), 1)

    @pl.when(e == 0)
    def _():
        acc_sc[...] = jnp.zeros(acc_sc.shape, F32)
        gidx = jnp.sum(jnp.where(lane == GROUP_LANE, comb_ref[...], 0.0), axis=1, keepdims=True)
        member = (lane.astype(F32) == gidx).astype(BF16)
        before = jnp.dot(tri_ref[...], member, preferred_element_type=F32)
        pos_sc[...] = jnp.sum(jnp.where(lane.astype(F32) == gidx, before, 0.0), axis=1, keepdims=True)
        cnt_sc[...] = jnp.sum(member.astype(F32), axis=0, keepdims=True)
        grow = comb_ref[...].T[GROUP_LANE:GROUP_LANE + 1]
        sub = lax.broadcasted_iota(jnp.int32, (LANES, tm), 0).astype(F32)
        posrow_sc[...] = jnp.sum(jnp.where(sub == grow, before.T, 0.0), axis=0, keepdims=True)
        grow_sc[...] = grow

    cnt = jnp.sum(jnp.where(lane[0:1] == grp, cnt_sc[...], 0.0)).astype(jnp.int32)
    n_chunks = (cnt + (chunk - 1)) // chunk

    def onehot(c):
        gidx = jnp.sum(jnp.where(lane == GROUP_LANE, comb_ref[...], 0.0), axis=1, keepdims=True)
        slot = jnp.where(gidx == grp.astype(F32), pos_sc[...], -1.0) - (c * chunk).astype(F32)
        col = lax.broadcasted_iota(jnp.int32, (tm, chunk), 1).astype(F32)
        return (col == slot).astype(BF16)

    def onehot_rows(c):
        slot = (jnp.where(grow_sc[...] == grp.astype(F32), posrow_sc[...], -1.0)
                - (c * chunk).astype(F32))
        row = lax.broadcasted_iota(jnp.int32, (chunk, tm), 0).astype(F32)
        return (row == slot).astype(BF16)

    def rows(c):
        return pl.ds(pl.multiple_of(c * chunk, chunk), chunk)

    @pl.when(k == 0)
    def _():
        comb = jnp.where(lane < N_EXPERTS, comb_ref[...], 0.0)
        comb_hi = comb.astype(BF16)
        comb_lo = (comb - comb_hi.astype(F32)).astype(BF16)

        def gather(c, carry):
            sel = onehot_rows(c)
            xg_sc[rows(c), :] = jnp.dot(sel, xb_ref[...], preferred_element_type=F32).astype(BF16)
            wts_sc[rows(c), :] = (jnp.dot(sel, comb_hi, preferred_element_type=F32)
                                  + jnp.dot(sel, comb_lo, preferred_element_type=F32))
            y_sc[rows(c), :] = jnp.zeros((chunk, y_sc.shape[1]), F32)
            return carry

        lax.fori_loop(0, n_chunks, gather, 0)

    def ffn(c, carry):
        xg = xg_sc[rows(c), :]
        hg = jnp.dot(xg, wg_ref[...], preferred_element_type=F32)
        hu = jnp.dot(xg, wu_ref[...], preferred_element_type=F32)
        h = (hg * jax.nn.sigmoid(hg)) * hu
        y = jnp.dot(h.astype(BF16), wd_ref[...], preferred_element_type=F32)
        wts = wts_sc[rows(c), :]
        wl = lax.broadcasted_iota(jnp.int32, wts.shape, 1)
        w = jnp.sum(jnp.where(wl == e, wts, 0.0), axis=1, keepdims=True)
        y_sc[rows(c), :] += jnp.where(w != 0.0, w * y, 0.0)
        return carry

    lax.fori_loop(0, n_chunks, ffn, 0)

    @pl.when(k == EXPERTS_PER_GROUP - 1)
    def _():
        def scatter(c, carry):
            acc_sc[...] += jnp.dot(onehot(c), y_sc[rows(c), :].astype(BF16),
                                   preferred_element_type=F32)
            return carry

        lax.fori_loop(0, n_chunks, scatter, 0)

    @pl.when(e == pl.num_programs(1) - 1)
    def _():
        x2 = _layer_norm(alpha * xf_ref[...] + acc_sc[...], lng_ref[...], lnb_ref[...])
        xo_ref[...] = x2
        xob_ref[...] = x2.astype(BF16)


def _moe(xb, xf, comb, wg, wu, wd, ln_g, ln_b, alpha):
    t, d = xf.shape
    tm = min(MOE_TM, t)
    assert t % tm == 0
    n_e, _, dff = wg.shape
    chunk = min(MOE_CHUNK, tm)
    cap = -(-tm // chunk) * chunk
    tri = jnp.tril(jnp.ones((tm, tm), BF16), -1)
    rows = lambda width: pl.BlockSpec((tm, width), lambda i, e: (i, 0))
    vec = pl.BlockSpec((1, d), lambda i, e: (0, 0))
    return pl.pallas_call(
        functools.partial(_moe_kernel, alpha=alpha, chunk=chunk),
        grid=(t // tm, n_e),
        in_specs=[rows(d), rows(d), rows(LANES),
                  pl.BlockSpec((tm, tm), lambda i, e: (0, 0)),
                  pl.BlockSpec((None, d, dff), lambda i, e: (e, 0, 0)),
                  pl.BlockSpec((None, d, dff), lambda i, e: (e, 0, 0)),
                  pl.BlockSpec((None, dff, d), lambda i, e: (e, 0, 0)),
                  vec, vec],
        out_specs=[rows(d), rows(d)],
        out_shape=[jax.ShapeDtypeStruct((t, d), F32),
                   jax.ShapeDtypeStruct((t, d), BF16)],
        scratch_shapes=[pltpu.VMEM((tm, d), F32),
                        pltpu.VMEM((tm, 1), F32),
                        pltpu.VMEM((1, tm), F32),
                        pltpu.VMEM((1, tm), F32),
                        pltpu.VMEM((1, LANES), F32),
                        pltpu.VMEM((cap, d), BF16),
                        pltpu.VMEM((cap, LANES), F32),
                        pltpu.VMEM((cap, d), F32)],
        compiler_params=_params(("parallel", "arbitrary")),
        name="moe_grouped",
    )(xb, xf, comb, tri, wg, wu, wd, ln_g, ln_b)


def kernel(x, mem, positions, w_in, lambda_q1, lambda_k1, lambda_q2, lambda_k2, diff_subln_g, conv_w, w_mem_kv, w_br_attn, w_br_conv, w_br_mem, w_out, ln1_g, ln1_b, ln2_g, ln2_b, w_router, router_bias, w_exp_gate, w_exp_up, w_exp_down):
    batch, seq, d = x.shape
    depth = w_in.shape[0]
    mem_len = mem.shape[1]
    t = batch * seq
    alpha = (2 * depth) ** 0.25

    cos_t, sin_t = _rope_tables(positions, min(PROJ_TM, t))
    lam_init = [0.8 - 0.6 * math.exp(-0.3 * l) for l in range(depth)]
    lam = _lambdas(lambda_q1, lambda_k1, lambda_q2, lambda_k2, lam_init)
    init_rows = jnp.broadcast_to(jnp.asarray(lam_init, F32).reshape(depth, 1), (depth, LANES))

    wr = jnp.zeros((d, LANES), F32).at[:, :N_EXPERTS].set(w_router.astype(F32))
    rb = jnp.zeros((1, LANES), F32).at[0, :N_EXPERTS].set(router_bias.astype(F32))
    wrh = wr.astype(BF16)
    wrl = (wr - wrh.astype(F32)).astype(BF16)
    mem_b = mem.reshape(batch * mem_len, d).astype(BF16)

    xf = x.reshape(t, d).astype(F32)
    xb = xf.astype(BF16)
    for l in range(depth):
        qt, k, vt, cv, qm, g = _projection(xb, w_in[l].astype(BF16), cos_t, sin_t, batch, seq)
        ya = _attention(qt, k, vt, lam[l:l + 1], init_rows[l:l + 1],
                        diff_subln_g[l].astype(F32).reshape(1, DA_V_DIM), batch, seq)
        kvm = _matmul(mem_b, w_mem_kv[l].astype(BF16), MEM_WIDTH, BF16)
        xf, xb, comb = _merge(ya, cv, qm, kvm, g, xf,
                              w_br_attn[l].astype(BF16), w_br_conv[l].astype(BF16),
                              w_br_mem[l].astype(BF16), w_out[l].astype(BF16),
                              conv_w[l].astype(F32),
                              ln1_g[l].astype(F32).reshape(1, d), ln1_b[l].astype(F32).reshape(1, d),
                              wrh, wrl, rb, seq, mem_len, alpha)
        xf, xb = _moe(xb, xf, comb, w_exp_gate[l].astype(BF16), w_exp_up[l].astype(BF16),
                      w_exp_down[l].astype(BF16),
                      ln2_g[l].astype(F32).reshape(1, d), ln2_b[l].astype(F32).reshape(1, d), alpha)
    return xf.reshape(batch, seq, d).astype(x.dtype)
```

```python
import functools
import math

import jax
import jax.numpy as jnp
from jax import lax
from jax.experimental import pallas as pl
from jax.experimental.pallas import tpu as pltpu

F32 = jnp.float32
BF16 = jnp.bfloat16

DA_HEADS = 8
DA_HEAD_DIM = 64
DA_V_DIM = 2 * DA_HEAD_DIM
CONV_WIDTH = 512
CONV_K = 3
MEM_HEADS = 4
MEM_HEAD_DIM = 128
MEM_WIDTH = MEM_HEADS * MEM_HEAD_DIM
N_BRANCH = 3
ROPE_THETA = 10000.0
N_EXPERTS = 16
N_GROUPS = 4
GROUP_LANE = 16
EXPERTS_PER_GROUP = N_EXPERTS // N_GROUPS
LN_EPS = 1e-5
RMS_EPS = 1e-5

LANES = 128
SUBLANES = 8
VMEM_LIMIT_BYTES = 56 * 1024 * 1024

PROJ_TM = 2048
PROJ_TN = 512
ATTN_TQ = 1024
MERGE_TM = 512
MOE_TM = 1024
MOE_CHUNK = 288


def _params(semantics, flags=None):
    return pltpu.CompilerParams(dimension_semantics=semantics,
                                vmem_limit_bytes=VMEM_LIMIT_BYTES, flags=flags)


def _rope_table_kernel(pos_ref, inv_ref, cos_ref, sin_ref):
    ang = pos_ref[...].astype(F32) * inv_ref[...]
    lane = lax.broadcasted_iota(jnp.int32, ang.shape, 1)
    first_half = (lane % DA_HEAD_DIM) < (DA_HEAD_DIM // 2)
    s = jnp.sin(ang)
    cos_ref[...] = jnp.cos(ang)
    sin_ref[...] = jnp.where(first_half, -s, s)


def _rope_tables(positions, tm):
    t = positions.size
    half = DA_HEAD_DIM // 2
    inv_freq = ROPE_THETA ** (-jnp.arange(0, DA_HEAD_DIM, 2, dtype=F32) / DA_HEAD_DIM)
    inv_lanes = jnp.tile(inv_freq, LANES // half).reshape(1, LANES)
    pos = positions.reshape(t, 1)
    return pl.pallas_call(
        _rope_table_kernel,
        grid=(t // tm,),
        in_specs=[pl.BlockSpec((tm, 1), lambda i: (i, 0)),
                  pl.BlockSpec((1, LANES), lambda i: (0, 0))],
        out_specs=[pl.BlockSpec((tm, LANES), lambda i: (i, 0)),
                   pl.BlockSpec((tm, LANES), lambda i: (i, 0))],
        out_shape=[jax.ShapeDtypeStruct((t, LANES), F32),
                   jax.ShapeDtypeStruct((t, LANES), F32)],
        compiler_params=_params(("parallel",)),
        name="rope_tables",
    )(pos, inv_lanes)


def _lambda_kernel(q1_ref, k1_ref, q2_ref, k2_ref, init_ref, lam_ref):
    a = jnp.sum(q1_ref[...] * k1_ref[...], axis=-1, keepdims=True)
    b = jnp.sum(q2_ref[...] * k2_ref[...], axis=-1, keepdims=True)
    lam_ref[...] = jnp.exp(a) - jnp.exp(b) + init_ref[...]


def _lambdas(lq1, lk1, lq2, lk2, lam_init):
    depth = lq1.shape[0]
    init = jnp.broadcast_to(jnp.asarray(lam_init, F32).reshape(depth, 1), (depth, LANES))
    return pl.pallas_call(
        _lambda_kernel,
        out_shape=jax.ShapeDtypeStruct((depth, LANES), F32),
        name="diff_lambda",
    )(lq1.astype(F32), lk1.astype(F32), lq2.astype(F32), lk2.astype(F32), init)


_Q_TILE0, _K_TILE0, _V_TILE0, _CV_TILE0, _QM_TILE, _G_TILE0, _N_TILES = 0, 2, 4, 6, 9, 10, 16
_Q_SCALE = (DA_HEAD_DIM ** -0.5) * math.log2(math.e)


def _rotary(acc, cos_ref, sin_ref):
    tn = acc.shape[1]
    reps = tn // LANES
    c = jnp.concatenate([cos_ref[...]] * reps, axis=1)
    s = jnp.concatenate([sin_ref[...]] * reps, axis=1)
    lane = lax.broadcasted_iota(jnp.int32, acc.shape, 1)
    half = DA_HEAD_DIM // 2
    first_half = (lane % DA_HEAD_DIM) < half
    partner = jnp.where(first_half,
                        pltpu.roll(acc, tn - half, axis=1),
                        pltpu.roll(acc, half, axis=1))
    return acc * c + partner * s


def _proj_segment_kernel(x_ref, w_ref, *refs, epilogue):
    *aux, o_ref = refs
    acc = jnp.dot(x_ref[...], w_ref[...], preferred_element_type=F32)
    o_ref[...] = epilogue(acc, *aux).astype(o_ref.dtype)


def _projection(xb, w, cos_t, sin_t, batch, seq):
    t, d = xb.shape
    tm, tn = min(PROJ_TM, seq), PROJ_TN
    assert seq % tm == 0 and w.shape == (d, _N_TILES * tn)
    tps = seq // tm
    tables = [cos_t, sin_t]

    def segment(name, tile0, n_tiles, epilogue, aux, out_dtype, transposed, wide=1):
        assert tile0 % wide == 0 and n_tiles % wide == 0
        tw, steps, first = tn * wide, n_tiles // wide, tile0 // wide
        if transposed:
            out_spec = pl.BlockSpec((None, tw, tm), lambda i, j: (i // tps, j, i % tps))
            out_shape = jax.ShapeDtypeStruct((batch, n_tiles * tn, seq), out_dtype)
        else:
            out_spec = pl.BlockSpec((tm, tw), lambda i, j: (i, j))
            out_shape = jax.ShapeDtypeStruct((t, n_tiles * tn), out_dtype)
        return pl.pallas_call(
            functools.partial(_proj_segment_kernel, epilogue=epilogue),
            grid=(t // tm, steps),
            in_specs=[pl.BlockSpec((tm, d), lambda i, j: (i, 0)),
                      pl.BlockSpec((d, tw), lambda i, j: (0, first + j))]
                     + [pl.BlockSpec((tm, LANES), lambda i, j: (i, 0)) for _ in aux],
            out_specs=out_spec,
            out_shape=out_shape,
            compiler_params=_params(("parallel", "arbitrary")),
            name=name,
        )(xb, w, *aux)

    qt = segment("proj_q", _Q_TILE0, 2,
                 lambda acc, c, s: (_rotary(acc, c, s) * _Q_SCALE).T, tables, BF16, True, wide=2)
    k = segment("proj_k", _K_TILE0, 2, _rotary, tables, BF16, False, wide=2)
    vt = segment("proj_v", _V_TILE0, 2, lambda acc: acc.T, [], BF16, True, wide=2)
    cv = segment("proj_conv", _CV_TILE0, 3, lambda acc: acc, [], F32, False, wide=3)
    qm = segment("proj_mem_q", _QM_TILE, 1, lambda acc: acc, [], BF16, False)
    g = segment("proj_gates", _G_TILE0, 6, jax.nn.sigmoid, [], BF16, False, wide=2)
    return qt, k, vt, cv, qm, g


def _matmul_kernel(a_ref, b_ref, o_ref):
    o_ref[...] = jnp.dot(a_ref[...], b_ref[...], preferred_element_type=F32).astype(o_ref.dtype)


def _matmul(a, b, tn, out_dtype):
    m, k = a.shape
    n = b.shape[1]
    return pl.pallas_call(
        _matmul_kernel,
        grid=(n // tn,),
        in_specs=[pl.BlockSpec((m, k), lambda j: (0, 0)),
                  pl.BlockSpec((k, tn), lambda j: (0, j))],
        out_specs=pl.BlockSpec((m, tn), lambda j: (0, j)),
        out_shape=jax.ShapeDtypeStruct((m, n), out_dtype),
        compiler_params=_params(("parallel",)),
        name="mem_kv_projection",
    )(a, b)


def _sublane_all(op, x):
    for shift in (4, 2, 1):
        x = op(x, pltpu.roll(x, shift, axis=0))
    return x


ATTN_MAX_EXCESS = 100.0


def _attn_kernel(lam_ref, init_ref, g_ref, qt_ref, k_ref, vt_ref, o_ref,
                 s0_sc, s1_sc, x0_sc, x1_sc, p0_sc, p1_sc, a0_sc, a1_sc, m_sc, l_sc, acc_sc, e_sc):
    s_sc, x_sc, p_sc, a_sc = (s0_sc, s1_sc), (x0_sc, x1_sc), (p0_sc, p1_sc), (a0_sc, a1_sc)
    qi = pl.program_id(2)
    dv, tq = qt_ref.shape
    tk, r = s0_sc.shape
    assert tq == 2 * tk and r == 2 * tq
    qt = qt_ref[...]
    sub = lax.broadcasted_iota(jnp.int32, qt.shape, 0)
    zero = jnp.zeros_like(qt)
    qq = jnp.concatenate([jnp.where(sub < DA_HEAD_DIM, qt, zero),
                          jnp.where(sub >= DA_HEAD_DIM, qt, zero)], axis=1)
    ones = jnp.ones((2 * SUBLANES, tk), vt_ref.dtype)

    def raw_scores(kb, diag):
        start = pl.multiple_of(kb * tk, tk)
        s = jnp.dot(k_ref[pl.ds(start, tk), :], qq, preferred_element_type=F32)
        if diag is not None:
            key = lax.broadcasted_iota(jnp.int32, s.shape, 0) + diag * tk
            qry = lax.broadcasted_iota(jnp.int32, s.shape, 1) & (tq - 1)
            s = jnp.where(key <= qry, s, -jnp.inf)
        return s.reshape(tk // SUBLANES, SUBLANES, r)

    def pv(kb, slot):
        start = pl.multiple_of(kb * tk, tk)
        lhs = jnp.concatenate([vt_ref[:, pl.ds(start, tk)], ones], axis=0)
        res = jnp.dot(lhs, p_sc[slot][...], preferred_element_type=F32)
        alpha = a_sc[slot][...]
        acc3 = acc_sc[...].reshape(dv // SUBLANES, SUBLANES, r) * alpha[None]
        acc_sc[...] = acc3.reshape(dv, r) + res[:dv]
        l_sc[...] = alpha * l_sc[...] + res[dv:dv + SUBLANES]

    def reset(m0):
        m_sc[...] = m0
        l_sc[...] = jnp.zeros(l_sc.shape, F32)
        acc_sc[...] = jnp.zeros(acc_sc.shape, F32)

    d0, d1 = 2 * qi, 2 * qi + 1

    def scores_exp(kb, slot, diag=None):
        s3 = raw_scores(kb, diag)
        x0_sc[...] = jnp.max(s3, axis=0)
        p_sc[slot][...] = jnp.exp2(s3 - m_sc[...][None]).reshape(tk, r).astype(p_sc[slot].dtype)

    def advance(slot):
        bm = _sublane_all(jnp.maximum, x0_sc[...])
        m_prev = m_sc[...]
        e_sc[...] = jnp.maximum(e_sc[...], bm - m_prev)
        m_new = jnp.maximum(m_prev, bm)
        a_sc[slot][...] = jnp.exp2(m_prev - m_new)
        m_sc[...] = m_new

    start = pl.multiple_of(qi * tq, tq)
    own = k_ref[pl.ds(start, tq), :].astype(F32).T * qt.astype(F32)
    seed = jnp.concatenate([jnp.sum(own[:DA_HEAD_DIM], axis=0, keepdims=True),
                            jnp.sum(own[DA_HEAD_DIM:], axis=0, keepdims=True)], axis=1)
    reset(jnp.broadcast_to(seed, m_sc.shape))
    e_sc[...] = jnp.full(e_sc.shape, -jnp.inf, F32)
    a0_sc[...] = jnp.ones(a0_sc.shape, F32)
    scores_exp(d0, 0, diag=0)

    def fast_pair(j):
        advance(1)
        scores_exp(2 * j, 1)
        pv(jnp.where(j == 0, d0, 2 * j - 1), 0)
        advance(0)
        scores_exp(2 * j + 1, 0)
        pv(2 * j, 1)

    def fast_quad(i, carry):
        fast_pair(2 * i)
        fast_pair(2 * i + 1)
        return carry

    lax.fori_loop(0, qi // 2, fast_quad, 0)

    @pl.when(qi % 2 == 1)
    def _():
        fast_pair(qi - 1)

    cols = (slice(tk, tq), slice(tq + tk, r))

    def d1_scores_exp():
        start = pl.multiple_of(d1 * tk, tk)
        qh = jnp.concatenate([qq[:, c] for c in cols], axis=1)
        s = jnp.dot(k_ref[pl.ds(start, tk), :], qh, preferred_element_type=F32)
        key = lax.broadcasted_iota(jnp.int32, s.shape, 0)
        qry = lax.broadcasted_iota(jnp.int32, s.shape, 1) & (tk - 1)
        s3 = jnp.where(key <= qry, s, -jnp.inf).reshape(tk // SUBLANES, SUBLANES, tq)
        bm = jnp.max(s3, axis=0)
        ninf = jnp.full((SUBLANES, tk), -jnp.inf, F32)
        x0_sc[...] = jnp.concatenate([ninf, bm[:, :tk], ninf, bm[:, tk:]], axis=1)
        mh = jnp.concatenate([m_sc[:, c] for c in cols], axis=1)
        p = jnp.exp2(s3 - mh[None]).reshape(tk, tq).astype(p1_sc.dtype)
        p1_sc[:, cols[0]] = p[:, :tk]
        p1_sc[:, cols[1]] = p[:, tk:]

    def d1_pv():
        start = pl.multiple_of(d1 * tk, tk)
        lhs = jnp.concatenate([vt_ref[:, pl.ds(start, tk)], ones], axis=0)
        ph = jnp.concatenate([p1_sc[:, c] for c in cols], axis=1)
        res = jnp.dot(lhs, ph, preferred_element_type=F32)
        for i, c in enumerate(cols):
            alpha = a1_sc[:, c]
            part = res[:, i * tk:(i + 1) * tk]
            acc3 = acc_sc[:, c].reshape(dv // SUBLANES, SUBLANES, tk) * alpha[None]
            acc_sc[:, c] = acc3.reshape(dv, tk) + part[:dv]
            l_sc[:, c] = alpha * l_sc[:, c] + part[dv:dv + SUBLANES]

    advance(1)
    d1_scores_exp()
    pv(jnp.where(qi == 0, d0, d0 - 1), 0)
    e_sc[...] = jnp.maximum(e_sc[...], _sublane_all(jnp.maximum, x0_sc[...]) - m_sc[...])
    d1_pv()

    def scores(kb, slot, diag=None):
        s3 = raw_scores(kb, diag)
        s_sc[slot][...] = s3.reshape(tk, r)
        x_sc[slot][...] = jnp.max(s3, axis=0)

    def softmax(slot):
        s3 = s_sc[slot][...].reshape(tk // SUBLANES, SUBLANES, r)
        m_prev = m_sc[...]
        m_new = jnp.maximum(m_prev, _sublane_all(jnp.maximum, x_sc[slot][...]))
        p_sc[slot][...] = jnp.exp2(s3 - m_new[None]).reshape(tk, r).astype(p_sc[slot].dtype)
        a_sc[slot][...] = jnp.exp2(m_prev - m_new)
        m_sc[...] = m_new

    def blk(t):
        return jnp.where(t < 2, d0 + t, t - 2)

    @pl.when(jnp.max(e_sc[...]) > ATTN_MAX_EXCESS)
    def _():
        reset(jnp.full(m_sc.shape, -jnp.inf, F32))
        scores(d0, 0, diag=0)
        scores(d1, 1, diag=1)
        softmax(0)

        def pair(j, carry):
            t = 2 * j + 1
            scores(t - 1, 0)
            softmax(1)
            pv(blk(t - 1), 0)
            scores(t, 1)
            softmax(0)
            pv(blk(t), 1)
            return carry

        lax.fori_loop(0, qi, pair, 0)
        n = 2 * qi + 1
        softmax(1)
        pv(blk(n - 1), 0)
        pv(blk(n), 1)

    ot = (acc_sc[...].reshape(dv // SUBLANES, SUBLANES, r) / l_sc[...][None]).reshape(dv, r)
    o = ot[:, :tq] - lam_ref[0:1, 0:1] * ot[:, tq:]
    o = o * lax.rsqrt(jnp.mean(jnp.square(o), axis=0, keepdims=True) + RMS_EPS)
    o = o.T * g_ref[...] * (1.0 - init_ref[...])
    o_ref[...] = o.astype(o_ref.dtype)


def _attention(qt, k, vt, lam, lam_init, subln_g, batch, seq):
    t = k.shape[0]
    tq = min(ATTN_TQ, seq)
    tk = tq // 2
    assert seq % tq == 0 and tq & (tq - 1) == 0
    nq = seq // tq
    h = DA_HEADS
    r = 2 * tq
    row = pl.BlockSpec((1, LANES), lambda b, hh, i: (0, 0))
    return pl.pallas_call(
        _attn_kernel,
        grid=(batch, h, nq),
        in_specs=[row, row, row,
                  pl.BlockSpec((None, DA_V_DIM, tq), lambda b, hh, i: (b, hh, i)),
                  pl.BlockSpec((seq, DA_V_DIM), lambda b, hh, i: (b, hh)),
                  pl.BlockSpec((None, DA_V_DIM, seq), lambda b, hh, i: (b, hh, 0))],
        out_specs=pl.BlockSpec((tq, DA_V_DIM), lambda b, hh, i: (b * nq + i, hh)),
        out_shape=jax.ShapeDtypeStruct((t, h * DA_V_DIM), BF16),
        scratch_shapes=[pltpu.VMEM((tk, r), F32), pltpu.VMEM((tk, r), F32),
                        pltpu.VMEM((SUBLANES, r), F32), pltpu.VMEM((SUBLANES, r), F32),
                        pltpu.VMEM((tk, r), BF16), pltpu.VMEM((tk, r), BF16),
                        pltpu.VMEM((SUBLANES, r), F32), pltpu.VMEM((SUBLANES, r), F32),
                        pltpu.VMEM((SUBLANES, r), F32),
                        pltpu.VMEM((SUBLANES, r), F32),
                        pltpu.VMEM((DA_V_DIM, r), F32),
                        pltpu.VMEM((SUBLANES, r), F32)],
        compiler_params=_params(("parallel", "parallel", "arbitrary")),
        name="diff_attention",
    )(lam, lam_init, subln_g, qt, k, vt)


def _layer_norm(h, g, b):
    mu = jnp.mean(h, axis=-1, keepdims=True)
    d = h - mu
    var = jnp.mean(jnp.square(d), axis=-1, keepdims=True)
    return d * lax.rsqrt(var + LN_EPS) * g + b


def _route(logits):
    lane = lax.broadcasted_iota(jnp.int32, logits.shape, 1)
    valid = lane < N_EXPERTS
    logits = jnp.where(valid, logits, -jnp.inf)
    mx = jnp.max(logits, axis=-1, keepdims=True)
    ex = jnp.exp(logits - mx)
    scores = ex / jnp.sum(ex, axis=-1, keepdims=True)
    group = lane // EXPERTS_PER_GROUP
    neg = jnp.full_like(scores, -1.0)
    best = jnp.max(jnp.where(group == 0, scores, neg), axis=-1, keepdims=True)
    gidx = jnp.zeros(best.shape, jnp.int32)
    for gi in range(1, N_GROUPS):
        cand = jnp.max(jnp.where(group == gi, scores, neg), axis=-1, keepdims=True)
        take = cand > best
        gidx = jnp.where(take, gi, gidx)
        best = jnp.where(take, cand, best)
    vals = jnp.where((group == gidx) & valid, scores, neg)
    big = jnp.full_like(lane, LANES)
    v1 = jnp.max(vals, axis=-1, keepdims=True)
    i1 = jnp.min(jnp.where(vals == v1, lane, big), axis=-1, keepdims=True)
    vals2 = jnp.where(lane == i1, neg, vals)
    v2 = jnp.max(vals2, axis=-1, keepdims=True)
    i2 = jnp.min(jnp.where(vals2 == v2, lane, big), axis=-1, keepdims=True)
    tot = v1 + v2
    comb = jnp.where(lane == i1, v1 / tot, 0.0) + jnp.where(lane == i2, v2 / tot, 0.0)
    return jnp.where(lane == GROUP_LANE, gidx.astype(F32), comb)


def _merge_kernel(ya_ref, cv_ref, cvp_ref, qm_ref, kvm_ref, g_ref, x_ref,
                  wa_ref, wc_ref, wm_ref, wo_ref, cw_ref, lng_ref, lnb_ref, wrh_ref, wrl_ref, rb_ref,
                  xo_ref, xob_ref, comb_ref, *, tiles_per_seq, alpha):
    i = pl.program_id(0)
    tm = ya_ref.shape[0]
    cw = CONV_WIDTH

    z = cv_ref[:, cw:2 * cw] * cv_ref[:, 2 * cw:3 * cw]
    zp = cvp_ref[:, cw:2 * cw] * cvp_ref[:, 2 * cw:3 * cw]
    zp = jnp.where(i % tiles_per_seq == 0, jnp.zeros_like(zp), zp)
    row = lax.broadcasted_iota(jnp.int32, z.shape, 0)
    zm1 = jnp.where(row == 0, zp[SUBLANES - 1:SUBLANES], pltpu.roll(z, 1, axis=0))
    zm2 = jnp.where(row == 0, zp[SUBLANES - 2:SUBLANES - 1],
                    jnp.where(row == 1, zp[SUBLANES - 1:SUBLANES], pltpu.roll(z, 2, axis=0)))
    w = cw_ref[...]
    y = w[0:1] * zm2
    y = y + w[1:2] * zm1
    y = y + w[2:3] * z
    y_conv = (cv_ref[:, 0:cw] * y).astype(BF16)

    outs = []
    for hd in range(MEM_HEADS):
        lo = hd * MEM_HEAD_DIM
        qh = qm_ref[:, lo:lo + MEM_HEAD_DIM]
        kh = kvm_ref[:, lo:lo + MEM_HEAD_DIM]
        vh = kvm_ref[:, MEM_WIDTH + lo:MEM_WIDTH + lo + MEM_HEAD_DIM]
        s = lax.dot_general(qh, kh, (((1,), (1,)), ((), ())),
                            preferred_element_type=F32) * (MEM_HEAD_DIM ** -0.5)
        s = s - jnp.max(s, axis=-1, keepdims=True)
        e = jnp.exp(s)
        p = e / jnp.sum(e, axis=-1, keepdims=True)
        outs.append(jnp.dot(p.astype(BF16), vh, preferred_element_type=F32))
    y_mem = jnp.concatenate(outs, axis=1).astype(BF16)

    d = x_ref.shape[1]
    merged = g_ref[:, 0:d] * jnp.dot(ya_ref[...], wa_ref[...], preferred_element_type=F32)
    merged = merged + g_ref[:, d:2 * d] * jnp.dot(y_conv, wc_ref[...], preferred_element_type=F32)
    merged = merged + g_ref[:, 2 * d:3 * d] * jnp.dot(y_mem, wm_ref[...], preferred_element_type=F32)
    mix = jnp.dot(merged.astype(BF16), wo_ref[...], preferred_element_type=F32)
    x1 = _layer_norm(alpha * x_ref[...] + mix, lng_ref[...], lnb_ref[...])
    xo_ref[...] = x1
    xob_ref[...] = x1.astype(BF16)

    x1h = x1.astype(BF16)
    x1l = (x1 - x1h.astype(F32)).astype(BF16)
    logits = (jnp.dot(x1h, wrh_ref[...], preferred_element_type=F32)
              + (jnp.dot(x1h, wrl_ref[...], preferred_element_type=F32)
                 + jnp.dot(x1l, wrh_ref[...], preferred_element_type=F32))) + rb_ref[...]
    comb_ref[...] = _route(logits)


def _merge(ya, cv, qm, kvm, g, xf, wa, wc, wm, wo, conv_w, ln_g, ln_b, wrh, wrl, rb, seq, mem_len, alpha):
    t, d = xf.shape
    tm = min(MERGE_TM, seq)
    assert seq % tm == 0 and tm % SUBLANES == 0
    tiles_per_seq = seq // tm
    halo = tm // SUBLANES

    def full(a):
        return pl.BlockSpec(a.shape, lambda i: (0,) * a.ndim)

    def rows(width):
        return pl.BlockSpec((tm, width), lambda i: (i, 0))

    kern = functools.partial(_merge_kernel, tiles_per_seq=tiles_per_seq, alpha=alpha)
    return pl.pallas_call(
        kern,
        grid=(t // tm,),
        in_specs=[rows(ya.shape[1]),
                  rows(cv.shape[1]),
                  pl.BlockSpec((SUBLANES, cv.shape[1]), lambda i: (jnp.maximum(i * halo - 1, 0), 0)),
                  rows(qm.shape[1]),
                  pl.BlockSpec((mem_len, kvm.shape[1]), lambda i: (i // tiles_per_seq, 0)),
                  rows(g.shape[1]),
                  rows(d),
                  full(wa), full(wc), full(wm), full(wo), full(conv_w),
                  full(ln_g), full(ln_b), full(wrh), full(wrl), full(rb)],
        out_specs=[rows(d), rows(d), rows(LANES)],
        out_shape=[jax.ShapeDtypeStruct((t, d), F32),
                   jax.ShapeDtypeStruct((t, d), BF16),
                   jax.ShapeDtypeStruct((t, LANES), F32)],
        compiler_params=_params(("parallel",)),
        name="merge_ln_router",
    )(ya, cv, cv, qm, kvm, g, xf, wa, wc, wm, wo, conv_w, ln_g, ln_b, wrh, wrl, rb)


def _moe_kernel(xb_ref, xf_ref, comb_ref, tri_ref, wg_ref, wu_ref, wd_ref, lng_ref, lnb_ref,
                xo_ref, xob_ref,
                acc_sc, pos_sc, posrow_sc, grow_sc, cnt_sc, xg_sc, wts_sc, y_sc, *, alpha, chunk):
    e = pl.program_id(1)
    grp = e // EXPERTS_PER_GROUP
    k = e % EXPERTS_PER_GROUP
    tm = xb_ref.shape[0]
    lane = lax.broadcasted_iota(jnp.int32, (tm, LANES), 1)

    @pl.when(e == 0)
    def _():
        acc_sc[...] = jnp.zeros(acc_sc.shape, F32)
        gidx = jnp.sum(jnp.where(lane == GROUP_LANE, comb_ref[...], 0.0), axis=1, keepdims=True)
        member = (lane.astype(F32) == gidx).astype(BF16)
        before = jnp.dot(tri_ref[...], member, preferred_element_type=F32)
        pos_sc[...] = jnp.sum(jnp.where(lane.astype(F32) == gidx, before, 0.0), axis=1, keepdims=True)
        cnt_sc[...] = jnp.sum(member.astype(F32), axis=0, keepdims=True)
        grow = comb_ref[...].T[GROUP_LANE:GROUP_LANE + 1]
        sub = lax.broadcasted_iota(jnp.int32, (LANES, tm), 0).astype(F32)
        posrow_sc[...] = jnp.sum(jnp.where(sub == grow, before.T, 0.0), axis=0, keepdims=True)
        grow_sc[...] = grow

    cnt = jnp.sum(jnp.where(lane[0:1] == grp, cnt_sc[...], 0.0)).astype(jnp.int32)
    n_chunks = (cnt + (chunk - 1)) // chunk

    def onehot(c):
        gidx = jnp.sum(jnp.where(lane == GROUP_LANE, comb_ref[...], 0.0), axis=1, keepdims=True)
        slot = jnp.where(gidx == grp.astype(F32), pos_sc[...], -1.0) - (c * chunk).astype(F32)
        col = lax.broadcasted_iota(jnp.int32, (tm, chunk), 1).astype(F32)
        return (col == slot).astype(BF16)

    def onehot_rows(c):
        slot = (jnp.where(grow_sc[...] == grp.astype(F32), posrow_sc[...], -1.0)
                - (c * chunk).astype(F32))
        row = lax.broadcasted_iota(jnp.int32, (chunk, tm), 0).astype(F32)
        return (row == slot).astype(BF16)

    def rows(c):
        return pl.ds(pl.multiple_of(c * chunk, chunk), chunk)

    @pl.when(k == 0)
    def _():
        comb = jnp.where(lane < N_EXPERTS, comb_ref[...], 0.0)
        comb_hi = comb.astype(BF16)
        comb_lo = (comb - comb_hi.astype(F32)).astype(BF16)

        def gather(c, carry):
            sel = onehot_rows(c)
            xg_sc[rows(c), :] = jnp.dot(sel, xb_ref[...], preferred_element_type=F32).astype(BF16)
            wts_sc[rows(c), :] = (jnp.dot(sel, comb_hi, preferred_element_type=F32)
                                  + jnp.dot(sel, comb_lo, preferred_element_type=F32))
            y_sc[rows(c), :] = jnp.zeros((chunk, y_sc.shape[1]), F32)
            return carry

        lax.fori_loop(0, n_chunks, gather, 0)

    def ffn(c, carry):
        xg = xg_sc[rows(c), :]
        hg = jnp.dot(xg, wg_ref[...], preferred_element_type=F32)
        hu = jnp.dot(xg, wu_ref[...], preferred_element_type=F32)
        h = (hg * jax.nn.sigmoid(hg)) * hu
        y = jnp.dot(h.astype(BF16), wd_ref[...], preferred_element_type=F32)
        wts = wts_sc[rows(c), :]
        wl = lax.broadcasted_iota(jnp.int32, wts.shape, 1)
        w = jnp.sum(jnp.where(wl == e, wts, 0.0), axis=1, keepdims=True)
        y_sc[rows(c), :] += jnp.where(w != 0.0, w * y, 0.0)
        return carry

    lax.fori_loop(0, n_chunks, ffn, 0)

    @pl.when(k == EXPERTS_PER_GROUP - 1)
    def _():
        def scatter(c, carry):
            acc_sc[...] += jnp.dot(onehot(c), y_sc[rows(c), :].astype(BF16),
                                   preferred_element_type=F32)
            return carry

        lax.fori_loop(0, n_chunks, scatter, 0)

    @pl.when(e == pl.num_programs(1) - 1)
    def _():
        x2 = _layer_norm(alpha * xf_ref[...] + acc_sc[...], lng_ref[...], lnb_ref[...])
        xo_ref[...] = x2
        xob_ref[...] = x2.astype(BF16)


def _moe(xb, xf, comb, wg, wu, wd, ln_g, ln_b, alpha):
    t, d = xf.shape
    tm = min(MOE_TM, t)
    assert t % tm == 0
    n_e, _, dff = wg.shape
    chunk = min(MOE_CHUNK, tm)
    cap = -(-tm // chunk) * chunk
    tri = jnp.tril(jnp.ones((tm, tm), BF16), -1)
    rows = lambda width: pl.BlockSpec((tm, width), lambda i, e: (i, 0))
    vec = pl.BlockSpec((1, d), lambda i, e: (0, 0))
    return pl.pallas_call(
        functools.partial(_moe_kernel, alpha=alpha, chunk=chunk),
        grid=(t // tm, n_e),
        in_specs=[rows(d), rows(d), rows(LANES),
                  pl.BlockSpec((tm, tm), lambda i, e: (0, 0)),
                  pl.BlockSpec((None, d, dff), lambda i, e: (e, 0, 0)),
                  pl.BlockSpec((None, d, dff), lambda i, e: (e, 0, 0)),
                  pl.BlockSpec((None, dff, d), lambda i, e: (e, 0, 0)),
                  vec, vec],
        out_specs=[rows(d), rows(d)],
        out_shape=[jax.ShapeDtypeStruct((t, d), F32),
                   jax.ShapeDtypeStruct((t, d), BF16)],
        scratch_shapes=[pltpu.VMEM((tm, d), F32),
                        pltpu.VMEM((tm, 1), F32),
                        pltpu.VMEM((1, tm), F32),
                        pltpu.VMEM((1, tm), F32),
                        pltpu.VMEM((1, LANES), F32),
                        pltpu.VMEM((cap, d), BF16),
                        pltpu.VMEM((cap, LANES), F32),
                        pltpu.VMEM((cap, d), F32)],
        compiler_params=_params(("parallel", "arbitrary")),
        name="moe_grouped",
    )(xb, xf, comb, tri, wg, wu, wd, ln_g, ln_b)


def kernel(x, mem, positions, w_in, lambda_q1, lambda_k1, lambda_q2, lambda_k2, diff_subln_g, conv_w, w_mem_kv, w_br_attn, w_br_conv, w_br_mem, w_out, ln1_g, ln1_b, ln2_g, ln2_b, w_router, router_bias, w_exp_gate, w_exp_up, w_exp_down):
    batch, seq, d = x.shape
    depth = w_in.shape[0]
    mem_len = mem.shape[1]
    t = batch * seq
    alpha = (2 * depth) ** 0.25

    cos_t, sin_t = _rope_tables(positions, min(PROJ_TM, t))
    lam_init = [0.8 - 0.6 * math.exp(-0.3 * l) for l in range(depth)]
    lam = _lambdas(lambda_q1, lambda_k1, lambda_q2, lambda_k2, lam_init)
    init_rows = jnp.broadcast_to(jnp.asarray(lam_init, F32).reshape(depth, 1), (depth, LANES))

    wr = jnp.zeros((d, LANES), F32).at[:, :N_EXPERTS].set(w_router.astype(F32))
    rb = jnp.zeros((1, LANES), F32).at[0, :N_EXPERTS].set(router_bias.astype(F32))
    wrh = wr.astype(BF16)
    wrl = (wr - wrh.astype(F32)).astype(BF16)
    mem_b = mem.reshape(batch * mem_len, d).astype(BF16)

    xf = x.reshape(t, d).astype(F32)
    xb = xf.astype(BF16)
    for l in range(depth):
        qt, k, vt, cv, qm, g = _projection(xb, w_in[l].astype(BF16), cos_t, sin_t, batch, seq)
        ya = _attention(qt, k, vt, lam[l:l + 1], init_rows[l:l + 1],
                        diff_subln_g[l].astype(F32).reshape(1, DA_V_DIM), batch, seq)
        kvm = _matmul(mem_b, w_mem_kv[l].astype(BF16), MEM_WIDTH, BF16)
        xf, xb, comb = _merge(ya, cv, qm, kvm, g, xf,
                              w_br_attn[l].astype(BF16), w_br_conv[l].astype(BF16),
                              w_br_mem[l].astype(BF16), w_out[l].astype(BF16),
                              conv_w[l].astype(F32),
                              ln1_g[l].astype(F32).reshape(1, d), ln1_b[l].astype(F32).reshape(1, d),
                              wrh, wrl, rb, seq, mem_len, alpha)
        xf, xb = _moe(xb, xf, comb, w_exp_gate[l].astype(BF16), w_exp_up[l].astype(BF16),
                      w_exp_down[l].astype(BF16),
                      ln2_g[l].astype(F32).reshape(1, d), ln2_b[l].astype(F32).reshape(1, d), alpha)
    return xf.reshape(batch, seq, d).astype(x.dtype)
```

```python
import functools
import math

import jax
import jax.numpy as jnp
from jax import lax
from jax.experimental import pallas as pl
from jax.experimental.pallas import tpu as pltpu

F32 = jnp.float32
BF16 = jnp.bfloat16

DA_HEADS = 8
DA_HEAD_DIM = 64
DA_V_DIM = 2 * DA_HEAD_DIM
CONV_WIDTH = 512
CONV_K = 3
MEM_HEADS = 4
MEM_HEAD_DIM = 128
MEM_WIDTH = MEM_HEADS * MEM_HEAD_DIM
N_BRANCH = 3
ROPE_THETA = 10000.0
N_EXPERTS = 16
N_GROUPS = 4
GROUP_LANE = 16
EXPERTS_PER_GROUP = N_EXPERTS // N_GROUPS
LN_EPS = 1e-5
RMS_EPS = 1e-5

LANES = 128
SUBLANES = 8
VMEM_LIMIT_BYTES = 56 * 1024 * 1024

PROJ_TM = 2048
PROJ_TN = 512
ATTN_TQ = 1024
MERGE_TM = 512
MOE_TM = 1024
MOE_CHUNK = 288


def _params(semantics):
    return pltpu.CompilerParams(dimension_semantics=semantics,
                                vmem_limit_bytes=VMEM_LIMIT_BYTES)


def _rope_table_kernel(pos_ref, inv_ref, cos_ref, sin_ref):
    ang = pos_ref[...].astype(F32) * inv_ref[...]
    lane = lax.broadcasted_iota(jnp.int32, ang.shape, 1)
    first_half = (lane % DA_HEAD_DIM) < (DA_HEAD_DIM // 2)
    s = jnp.sin(ang)
    cos_ref[...] = jnp.cos(ang)
    sin_ref[...] = jnp.where(first_half, -s, s)


def _rope_tables(positions, tm):
    t = positions.size
    half = DA_HEAD_DIM // 2
    inv_freq = ROPE_THETA ** (-jnp.arange(0, DA_HEAD_DIM, 2, dtype=F32) / DA_HEAD_DIM)
    inv_lanes = jnp.tile(inv_freq, LANES // half).reshape(1, LANES)
    pos = positions.reshape(t, 1)
    return pl.pallas_call(
        _rope_table_kernel,
        grid=(t // tm,),
        in_specs=[pl.BlockSpec((tm, 1), lambda i: (i, 0)),
                  pl.BlockSpec((1, LANES), lambda i: (0, 0))],
        out_specs=[pl.BlockSpec((tm, LANES), lambda i: (i, 0)),
                   pl.BlockSpec((tm, LANES), lambda i: (i, 0))],
        out_shape=[jax.ShapeDtypeStruct((t, LANES), F32),
                   jax.ShapeDtypeStruct((t, LANES), F32)],
        compiler_params=_params(("parallel",)),
        name="rope_tables",
    )(pos, inv_lanes)


def _lambda_kernel(q1_ref, k1_ref, q2_ref, k2_ref, init_ref, lam_ref):
    a = jnp.sum(q1_ref[...] * k1_ref[...], axis=-1, keepdims=True)
    b = jnp.sum(q2_ref[...] * k2_ref[...], axis=-1, keepdims=True)
    lam_ref[...] = jnp.exp(a) - jnp.exp(b) + init_ref[...]


def _lambdas(lq1, lk1, lq2, lk2, lam_init):
    depth = lq1.shape[0]
    init = jnp.broadcast_to(jnp.asarray(lam_init, F32).reshape(depth, 1), (depth, LANES))
    return pl.pallas_call(
        _lambda_kernel,
        out_shape=jax.ShapeDtypeStruct((depth, LANES), F32),
        name="diff_lambda",
    )(lq1.astype(F32), lk1.astype(F32), lq2.astype(F32), lk2.astype(F32), init)


_Q_TILE0, _K_TILE0, _V_TILE0, _CV_TILE0, _QM_TILE, _G_TILE0, _N_TILES = 0, 2, 4, 6, 9, 10, 16
_Q_SCALE = (DA_HEAD_DIM ** -0.5) * math.log2(math.e)


def _rotary(acc, cos_ref, sin_ref):
    tn = acc.shape[1]
    reps = tn // LANES
    c = jnp.concatenate([cos_ref[...]] * reps, axis=1)
    s = jnp.concatenate([sin_ref[...]] * reps, axis=1)
    lane = lax.broadcasted_iota(jnp.int32, acc.shape, 1)
    half = DA_HEAD_DIM // 2
    first_half = (lane % DA_HEAD_DIM) < half
    partner = jnp.where(first_half,
                        pltpu.roll(acc, tn - half, axis=1),
                        pltpu.roll(acc, half, axis=1))
    return acc * c + partner * s


def _proj_segment_kernel(x_ref, w_ref, *refs, epilogue):
    *aux, o_ref = refs
    acc = jnp.dot(x_ref[...], w_ref[...], preferred_element_type=F32)
    o_ref[...] = epilogue(acc, *aux).astype(o_ref.dtype)


def _projection(xb, w, cos_t, sin_t, batch, seq):
    t, d = xb.shape
    tm, tn = min(PROJ_TM, seq), PROJ_TN
    assert seq % tm == 0 and w.shape == (d, _N_TILES * tn)
    tps = seq // tm
    tables = [cos_t, sin_t]

    def segment(name, tile0, n_tiles, epilogue, aux, out_dtype, transposed, wide=1):
        assert tile0 % wide == 0 and n_tiles % wide == 0
        tw, steps, first = tn * wide, n_tiles // wide, tile0 // wide
        if transposed:
            out_spec = pl.BlockSpec((None, tw, tm), lambda i, j: (i // tps, j, i % tps))
            out_shape = jax.ShapeDtypeStruct((batch, n_tiles * tn, seq), out_dtype)
        else:
            out_spec = pl.BlockSpec((tm, tw), lambda i, j: (i, j))
            out_shape = jax.ShapeDtypeStruct((t, n_tiles * tn), out_dtype)
        return pl.pallas_call(
            functools.partial(_proj_segment_kernel, epilogue=epilogue),
            grid=(t // tm, steps),
            in_specs=[pl.BlockSpec((tm, d), lambda i, j: (i, 0)),
                      pl.BlockSpec((d, tw), lambda i, j: (0, first + j))]
                     + [pl.BlockSpec((tm, LANES), lambda i, j: (i, 0)) for _ in aux],
            out_specs=out_spec,
            out_shape=out_shape,
            compiler_params=_params(("parallel", "arbitrary")),
            name=name,
        )(xb, w, *aux)

    qt = segment("proj_q", _Q_TILE0, 2,
                 lambda acc, c, s: (_rotary(acc, c, s) * _Q_SCALE).T, tables, BF16, True, wide=2)
    k = segment("proj_k", _K_TILE0, 2, _rotary, tables, BF16, False, wide=2)
    vt = segment("proj_v", _V_TILE0, 2, lambda acc: acc.T, [], BF16, True, wide=2)
    cv = segment("proj_conv", _CV_TILE0, 3, lambda acc: acc, [], F32, False, wide=3)
    qm = segment("proj_mem_q", _QM_TILE, 1, lambda acc: acc, [], BF16, False)
    g = segment("proj_gates", _G_TILE0, 6, jax.nn.sigmoid, [], BF16, False, wide=2)
    return qt, k, vt, cv, qm, g


def _matmul_kernel(a_ref, b_ref, o_ref):
    o_ref[...] = jnp.dot(a_ref[...], b_ref[...], preferred_element_type=F32).astype(o_ref.dtype)


def _matmul(a, b, tn, out_dtype):
    m, k = a.shape
    n = b.shape[1]
    return pl.pallas_call(
        _matmul_kernel,
        grid=(n // tn,),
        in_specs=[pl.BlockSpec((m, k), lambda j: (0, 0)),
                  pl.BlockSpec((k, tn), lambda j: (0, j))],
        out_specs=pl.BlockSpec((m, tn), lambda j: (0, j)),
        out_shape=jax.ShapeDtypeStruct((m, n), out_dtype),
        compiler_params=_params(("parallel",)),
        name="mem_kv_projection",
    )(a, b)


def _sublane_all(op, x):
    for shift in (4, 2, 1):
        x = op(x, pltpu.roll(x, shift, axis=0))
    return x


ATTN_MAX_EXCESS = 100.0


def _attn_kernel(lam_ref, init_ref, g_ref, qt_ref, k_ref, vt_ref, o_ref,
                 s0_sc, s1_sc, x0_sc, x1_sc, p0_sc, p1_sc, a0_sc, a1_sc, m_sc, l_sc, acc_sc, e_sc):
    s_sc, x_sc, p_sc, a_sc = (s0_sc, s1_sc), (x0_sc, x1_sc), (p0_sc, p1_sc), (a0_sc, a1_sc)
    qi = pl.program_id(2)
    dv, tq = qt_ref.shape
    tk, r = s0_sc.shape
    assert tq == 2 * tk and r == 2 * tq
    qt = qt_ref[...]
    sub = lax.broadcasted_iota(jnp.int32, qt.shape, 0)
    zero = jnp.zeros_like(qt)
    qq = jnp.concatenate([jnp.where(sub < DA_HEAD_DIM, qt, zero),
                          jnp.where(sub >= DA_HEAD_DIM, qt, zero)], axis=1)
    ones = jnp.ones((2 * SUBLANES, tk), vt_ref.dtype)

    def raw_scores(kb, diag):
        start = pl.multiple_of(kb * tk, tk)
        s = jnp.dot(k_ref[pl.ds(start, tk), :], qq, preferred_element_type=F32)
        if diag is not None:
            key = lax.broadcasted_iota(jnp.int32, s.shape, 0) + diag * tk
            qry = lax.broadcasted_iota(jnp.int32, s.shape, 1) & (tq - 1)
            s = jnp.where(key <= qry, s, -jnp.inf)
        return s.reshape(tk // SUBLANES, SUBLANES, r)

    def pv(kb, slot):
        start = pl.multiple_of(kb * tk, tk)
        lhs = jnp.concatenate([vt_ref[:, pl.ds(start, tk)], ones], axis=0)
        res = jnp.dot(lhs, p_sc[slot][...], preferred_element_type=F32)
        alpha = a_sc[slot][...]
        acc3 = acc_sc[...].reshape(dv // SUBLANES, SUBLANES, r) * alpha[None]
        acc_sc[...] = acc3.reshape(dv, r) + res[:dv]
        l_sc[...] = alpha * l_sc[...] + res[dv:dv + SUBLANES]

    def reset(m0):
        m_sc[...] = m0
        l_sc[...] = jnp.zeros(l_sc.shape, F32)
        acc_sc[...] = jnp.zeros(acc_sc.shape, F32)

    d0, d1 = 2 * qi, 2 * qi + 1

    def scores_exp(kb, slot, diag=None):
        s3 = raw_scores(kb, diag)
        x0_sc[...] = jnp.max(s3, axis=0)
        p_sc[slot][...] = jnp.exp2(s3 - m_sc[...][None]).reshape(tk, r).astype(p_sc[slot].dtype)

    def advance(slot):
        bm = _sublane_all(jnp.maximum, x0_sc[...])
        m_prev = m_sc[...]
        e_sc[...] = jnp.maximum(e_sc[...], bm - m_prev)
        m_new = jnp.maximum(m_prev, bm)
        a_sc[slot][...] = jnp.exp2(m_prev - m_new)
        m_sc[...] = m_new

    start = pl.multiple_of(qi * tq, tq)
    own = k_ref[pl.ds(start, tq), :].astype(F32).T * qt.astype(F32)
    seed = jnp.concatenate([jnp.sum(own[:DA_HEAD_DIM], axis=0, keepdims=True),
                            jnp.sum(own[DA_HEAD_DIM:], axis=0, keepdims=True)], axis=1)
    reset(jnp.broadcast_to(seed, m_sc.shape))
    e_sc[...] = jnp.full(e_sc.shape, -jnp.inf, F32)
    a0_sc[...] = jnp.ones(a0_sc.shape, F32)
    scores_exp(d0, 0, diag=0)

    def fast_pair(j):
        advance(1)
        scores_exp(2 * j, 1)
        pv(jnp.where(j == 0, d0, 2 * j - 1), 0)
        advance(0)
        scores_exp(2 * j + 1, 0)
        pv(2 * j, 1)

    def fast_quad(i, carry):
        fast_pair(2 * i)
        fast_pair(2 * i + 1)
        return carry

    lax.fori_loop(0, qi // 2, fast_quad, 0)

    @pl.when(qi % 2 == 1)
    def _():
        fast_pair(qi - 1)

    cols = (slice(tk, tq), slice(tq + tk, r))

    def d1_scores_exp():
        start = pl.multiple_of(d1 * tk, tk)
        qh = jnp.concatenate([qq[:, c] for c in cols], axis=1)
        s = jnp.dot(k_ref[pl.ds(start, tk), :], qh, preferred_element_type=F32)
        key = lax.broadcasted_iota(jnp.int32, s.shape, 0)
        qry = lax.broadcasted_iota(jnp.int32, s.shape, 1) & (tk - 1)
        s3 = jnp.where(key <= qry, s, -jnp.inf).reshape(tk // SUBLANES, SUBLANES, tq)
        bm = jnp.max(s3, axis=0)
        ninf = jnp.full((SUBLANES, tk), -jnp.inf, F32)
        x0_sc[...] = jnp.concatenate([ninf, bm[:, :tk], ninf, bm[:, tk:]], axis=1)
        mh = jnp.concatenate([m_sc[:, c] for c in cols], axis=1)
        p = jnp.exp2(s3 - mh[None]).reshape(tk, tq).astype(p1_sc.dtype)
        p1_sc[:, cols[0]] = p[:, :tk]
        p1_sc[:, cols[1]] = p[:, tk:]

    def d1_pv():
        start = pl.multiple_of(d1 * tk, tk)
        lhs = jnp.concatenate([vt_ref[:, pl.ds(start, tk)], ones], axis=0)
        ph = jnp.concatenate([p1_sc[:, c] for c in cols], axis=1)
        res = jnp.dot(lhs, ph, preferred_element_type=F32)
        for i, c in enumerate(cols):
            alpha = a1_sc[:, c]
            part = res[:, i * tk:(i + 1) * tk]
            acc3 = acc_sc[:, c].reshape(dv // SUBLANES, SUBLANES, tk) * alpha[None]
            acc_sc[:, c] = acc3.reshape(dv, tk) + part[:dv]
            l_sc[:, c] = alpha * l_sc[:, c] + part[dv:dv + SUBLANES]

    advance(1)
    d1_scores_exp()
    pv(jnp.where(qi == 0, d0, d0 - 1), 0)
    e_sc[...] = jnp.maximum(e_sc[...], _sublane_all(jnp.maximum, x0_sc[...]) - m_sc[...])
    d1_pv()

    def scores(kb, slot, diag=None):
        s3 = raw_scores(kb, diag)
        s_sc[slot][...] = s3.reshape(tk, r)
        x_sc[slot][...] = jnp.max(s3, axis=0)

    def softmax(slot):
        s3 = s_sc[slot][...].reshape(tk // SUBLANES, SUBLANES, r)
        m_prev = m_sc[...]
        m_new = jnp.maximum(m_prev, _sublane_all(jnp.maximum, x_sc[slot][...]))
        p_sc[slot][...] = jnp.exp2(s3 - m_new[None]).reshape(tk, r).astype(p_sc[slot].dtype)
        a_sc[slot][...] = jnp.exp2(m_prev - m_new)
        m_sc[...] = m_new

    def blk(t):
        return jnp.where(t < 2, d0 + t, t - 2)

    @pl.when(jnp.max(e_sc[...]) > ATTN_MAX_EXCESS)
    def _():
        reset(jnp.full(m_sc.shape, -jnp.inf, F32))
        scores(d0, 0, diag=0)
        scores(d1, 1, diag=1)
        softmax(0)

        def pair(j, carry):
            t = 2 * j + 1
            scores(t - 1, 0)
            softmax(1)
            pv(blk(t - 1), 0)
            scores(t, 1)
            softmax(0)
            pv(blk(t), 1)
            return carry

        lax.fori_loop(0, qi, pair, 0)
        n = 2 * qi + 1
        softmax(1)
        pv(blk(n - 1), 0)
        pv(blk(n), 1)

    inv_l = 1.0 / l_sc[...]
    ot = (acc_sc[...].reshape(dv // SUBLANES, SUBLANES, r) * inv_l[None]).reshape(dv, r)
    o = ot[:, :tq] - lam_ref[0:1, 0:1] * ot[:, tq:]
    o = o * lax.rsqrt(jnp.mean(jnp.square(o), axis=0, keepdims=True) + RMS_EPS)
    o = o.T * g_ref[...] * (1.0 - init_ref[...])
    o_ref[...] = o.astype(o_ref.dtype)


def _attention(qt, k, vt, lam, lam_init, subln_g, batch, seq):
    t = k.shape[0]
    tq = min(ATTN_TQ, seq)
    tk = tq // 2
    assert seq % tq == 0 and tq & (tq - 1) == 0
    nq = seq // tq
    h = DA_HEADS
    r = 2 * tq
    row = pl.BlockSpec((1, LANES), lambda b, hh, i: (0, 0))
    return pl.pallas_call(
        _attn_kernel,
        grid=(batch, h, nq),
        in_specs=[row, row, row,
                  pl.BlockSpec((None, DA_V_DIM, tq), lambda b, hh, i: (b, hh, i)),
                  pl.BlockSpec((seq, DA_V_DIM), lambda b, hh, i: (b, hh)),
                  pl.BlockSpec((None, DA_V_DIM, seq), lambda b, hh, i: (b, hh, 0))],
        out_specs=pl.BlockSpec((tq, DA_V_DIM), lambda b, hh, i: (b * nq + i, hh)),
        out_shape=jax.ShapeDtypeStruct((t, h * DA_V_DIM), BF16),
        scratch_shapes=[pltpu.VMEM((tk, r), F32), pltpu.VMEM((tk, r), F32),
                        pltpu.VMEM((SUBLANES, r), F32), pltpu.VMEM((SUBLANES, r), F32),
                        pltpu.VMEM((tk, r), BF16), pltpu.VMEM((tk, r), BF16),
                        pltpu.VMEM((SUBLANES, r), F32), pltpu.VMEM((SUBLANES, r), F32),
                        pltpu.VMEM((SUBLANES, r), F32),
                        pltpu.VMEM((SUBLANES, r), F32),
                        pltpu.VMEM((DA_V_DIM, r), F32),
                        pltpu.VMEM((SUBLANES, r), F32)],
        compiler_params=_params(("parallel", "parallel", "arbitrary")),
        name="diff_attention",
    )(lam, lam_init, subln_g, qt, k, vt)


def _layer_norm(h, g, b):
    mu = jnp.mean(h, axis=-1, keepdims=True)
    d = h - mu
    var = jnp.mean(jnp.square(d), axis=-1, keepdims=True)
    return d * lax.rsqrt(var + LN_EPS) * g + b


def _route(logits):
    lane = lax.broadcasted_iota(jnp.int32, logits.shape, 1)
    valid = lane < N_EXPERTS
    logits = jnp.where(valid, logits, -jnp.inf)
    mx = jnp.max(logits, axis=-1, keepdims=True)
    ex = jnp.exp(logits - mx)
    scores = ex / jnp.sum(ex, axis=-1, keepdims=True)
    group = lane // EXPERTS_PER_GROUP
    neg = jnp.full_like(scores, -1.0)
    best = jnp.max(jnp.where(group == 0, scores, neg), axis=-1, keepdims=True)
    gidx = jnp.zeros(best.shape, jnp.int32)
    for gi in range(1, N_GROUPS):
        cand = jnp.max(jnp.where(group == gi, scores, neg), axis=-1, keepdims=True)
        take = cand > best
        gidx = jnp.where(take, gi, gidx)
        best = jnp.where(take, cand, best)
    vals = jnp.where((group == gidx) & valid, scores, neg)
    big = jnp.full_like(lane, LANES)
    v1 = jnp.max(vals, axis=-1, keepdims=True)
    i1 = jnp.min(jnp.where(vals == v1, lane, big), axis=-1, keepdims=True)
    vals2 = jnp.where(lane == i1, neg, vals)
    v2 = jnp.max(vals2, axis=-1, keepdims=True)
    i2 = jnp.min(jnp.where(vals2 == v2, lane, big), axis=-1, keepdims=True)
    tot = v1 + v2
    comb = jnp.where(lane == i1, v1 / tot, 0.0) + jnp.where(lane == i2, v2 / tot, 0.0)
    return jnp.where(lane == GROUP_LANE, gidx.astype(F32), comb)


def _merge_kernel(ya_ref, cv_ref, cvp_ref, qm_ref, kvm_ref, g_ref, x_ref,
                  wa_ref, wc_ref, wm_ref, wo_ref, cw_ref, lng_ref, lnb_ref, wrh_ref, wrl_ref, rb_ref,
                  xo_ref, xob_ref, comb_ref, *, tiles_per_seq, alpha):
    i = pl.program_id(0)
    tm = ya_ref.shape[0]
    cw = CONV_WIDTH

    z = cv_ref[:, cw:2 * cw] * cv_ref[:, 2 * cw:3 * cw]
    zp = cvp_ref[:, cw:2 * cw] * cvp_ref[:, 2 * cw:3 * cw]
    zp = jnp.where(i % tiles_per_seq == 0, jnp.zeros_like(zp), zp)
    row = lax.broadcasted_iota(jnp.int32, z.shape, 0)
    zm1 = jnp.where(row == 0, zp[SUBLANES - 1:SUBLANES], pltpu.roll(z, 1, axis=0))
    zm2 = jnp.where(row == 0, zp[SUBLANES - 2:SUBLANES - 1],
                    jnp.where(row == 1, zp[SUBLANES - 1:SUBLANES], pltpu.roll(z, 2, axis=0)))
    w = cw_ref[...]
    y = w[0:1] * zm2
    y = y + w[1:2] * zm1
    y = y + w[2:3] * z
    y_conv = (cv_ref[:, 0:cw] * y).astype(BF16)

    outs = []
    for hd in range(MEM_HEADS):
        lo = hd * MEM_HEAD_DIM
        qh = qm_ref[:, lo:lo + MEM_HEAD_DIM]
        kh = kvm_ref[:, lo:lo + MEM_HEAD_DIM]
        vh = kvm_ref[:, MEM_WIDTH + lo:MEM_WIDTH + lo + MEM_HEAD_DIM]
        s = lax.dot_general(qh, kh, (((1,), (1,)), ((), ())),
                            preferred_element_type=F32) * (MEM_HEAD_DIM ** -0.5)
        s = s - jnp.max(s, axis=-1, keepdims=True)
        e = jnp.exp(s)
        p = e / jnp.sum(e, axis=-1, keepdims=True)
        outs.append(jnp.dot(p.astype(BF16), vh, preferred_element_type=F32))
    y_mem = jnp.concatenate(outs, axis=1).astype(BF16)

    d = x_ref.shape[1]
    merged = g_ref[:, 0:d] * jnp.dot(ya_ref[...], wa_ref[...], preferred_element_type=F32)
    merged = merged + g_ref[:, d:2 * d] * jnp.dot(y_conv, wc_ref[...], preferred_element_type=F32)
    merged = merged + g_ref[:, 2 * d:3 * d] * jnp.dot(y_mem, wm_ref[...], preferred_element_type=F32)
    mix = jnp.dot(merged.astype(BF16), wo_ref[...], preferred_element_type=F32)
    x1 = _layer_norm(alpha * x_ref[...] + mix, lng_ref[...], lnb_ref[...])
    xo_ref[...] = x1
    xob_ref[...] = x1.astype(BF16)

    x1h = x1.astype(BF16)
    x1l = (x1 - x1h.astype(F32)).astype(BF16)
    logits = (jnp.dot(x1h, wrh_ref[...], preferred_element_type=F32)
              + (jnp.dot(x1h, wrl_ref[...], preferred_element_type=F32)
                 + jnp.dot(x1l, wrh_ref[...], preferred_element_type=F32))) + rb_ref[...]
    comb_ref[...] = _route(logits)


def _merge(ya, cv, qm, kvm, g, xf, wa, wc, wm, wo, conv_w, ln_g, ln_b, wrh, wrl, rb, seq, mem_len, alpha):
    t, d = xf.shape
    tm = min(MERGE_TM, seq)
    assert seq % tm == 0 and tm % SUBLANES == 0
    assert conv_w.shape == (CONV_K, CONV_WIDTH) and g.shape[1] == N_BRANCH * d
    tiles_per_seq = seq // tm
    halo = tm // SUBLANES

    def full(a):
        return pl.BlockSpec(a.shape, lambda i: (0,) * a.ndim)

    def rows(width):
        return pl.BlockSpec((tm, width), lambda i: (i, 0))

    kern = functools.partial(_merge_kernel, tiles_per_seq=tiles_per_seq, alpha=alpha)
    return pl.pallas_call(
        kern,
        grid=(t // tm,),
        in_specs=[rows(ya.shape[1]),
                  rows(cv.shape[1]),
                  pl.BlockSpec((SUBLANES, cv.shape[1]), lambda i: (jnp.maximum(i * halo - 1, 0), 0)),
                  rows(qm.shape[1]),
                  pl.BlockSpec((mem_len, kvm.shape[1]), lambda i: (i // tiles_per_seq, 0)),
                  rows(g.shape[1]),
                  rows(d),
                  full(wa), full(wc), full(wm), full(wo), full(conv_w),
                  full(ln_g), full(ln_b), full(wrh), full(wrl), full(rb)],
        out_specs=[rows(d), rows(d), rows(LANES)],
        out_shape=[jax.ShapeDtypeStruct((t, d), F32),
                   jax.ShapeDtypeStruct((t, d), BF16),
                   jax.ShapeDtypeStruct((t, LANES), F32)],
        compiler_params=_params(("parallel",)),
        name="merge_ln_router",
    )(ya, cv, cv, qm, kvm, g, xf, wa, wc, wm, wo, conv_w, ln_g, ln_b, wrh, wrl, rb)


def _moe_kernel(xb_ref, xf_ref, comb_ref, tri_ref, wg_ref, wu_ref, wd_ref, lng_ref, lnb_ref,
                xo_ref, xob_ref,
                acc_sc, pos_sc, posrow_sc, grow_sc, cnt_sc, xg_sc, wts_sc, y_sc, *, alpha, chunk):
    e = pl.program_id(1)
    grp = e // EXPERTS_PER_GROUP
    k = e % EXPERTS_PER_GROUP
    tm = xb_ref.shape[0]
    lane = lax.broadcasted_iota(jnp.int32, (tm, LANES), 1)

    @pl.when(e == 0)
    def _():
        acc_sc[...] = jnp.zeros(acc_sc.shape, F32)
        gidx = jnp.sum(jnp.where(lane == GROUP_LANE, comb_ref[...], 0.0), axis=1, keepdims=True)
        member = (lane.astype(F32) == gidx).astype(BF16)
        before = jnp.dot(tri_ref[...], member, preferred_element_type=F32)
        pos_sc[...] = jnp.sum(jnp.where(lane.astype(F32) == gidx, before, 0.0), axis=1, keepdims=True)
        cnt_sc[...] = jnp.sum(member.astype(F32), axis=0, keepdims=True)
        grow = comb_ref[...].T[GROUP_LANE:GROUP_LANE + 1]
        sub = lax.broadcasted_iota(jnp.int32, (LANES, tm), 0).astype(F32)
        posrow_sc[...] = jnp.sum(jnp.where(sub == grow, before.T, 0.0), axis=0, keepdims=True)
        grow_sc[...] = grow

    cnt = jnp.sum(jnp.where(lane[0:1] == grp, cnt_sc[...], 0.0)).astype(jnp.int32)
    n_chunks = (cnt + (chunk - 1)) // chunk

    def onehot(c):
        gidx = jnp.sum(jnp.where(lane == GROUP_LANE, comb_ref[...], 0.0), axis=1, keepdims=True)
        slot = jnp.where(gidx == grp.astype(F32), pos_sc[...], -1.0) - (c * chunk).astype(F32)
        col = lax.broadcasted_iota(jnp.int32, (tm, chunk), 1).astype(F32)
        return (col == slot).astype(BF16)

    def onehot_rows(c):
        slot = (jnp.where(grow_sc[...] == grp.astype(F32), posrow_sc[...], -1.0)
                - (c * chunk).astype(F32))
        row = lax.broadcasted_iota(jnp.int32, (chunk, tm), 0).astype(F32)
        return (row == slot).astype(BF16)

    def rows(c):
        return pl.ds(pl.multiple_of(c * chunk, chunk), chunk)

    @pl.when(k == 0)
    def _():
        comb = jnp.where(lane < N_EXPERTS, comb_ref[...], 0.0)
        comb_hi = comb.astype(BF16)
        comb_lo = (comb - comb_hi.astype(F32)).astype(BF16)

        def gather(c, carry):
            sel = onehot_rows(c)
            xg_sc[rows(c), :] = jnp.dot(sel, xb_ref[...], preferred_element_type=F32).astype(BF16)
            wts_sc[rows(c), :] = (jnp.dot(sel, comb_hi, preferred_element_type=F32)
                                  + jnp.dot(sel, comb_lo, preferred_element_type=F32))
            y_sc[rows(c), :] = jnp.zeros((chunk, y_sc.shape[1]), F32)
            return carry

        lax.fori_loop(0, n_chunks, gather, 0)

    def ffn(c, carry):
        xg = xg_sc[rows(c), :]
        hg = jnp.dot(xg, wg_ref[...], preferred_element_type=F32)
        hu = jnp.dot(xg, wu_ref[...], preferred_element_type=F32)
        h = (hg * jax.nn.sigmoid(hg)) * hu
        y = jnp.dot(h.astype(BF16), wd_ref[...], preferred_element_type=F32)
        wts = wts_sc[rows(c), :]
        wl = lax.broadcasted_iota(jnp.int32, wts.shape, 1)
        w = jnp.sum(jnp.where(wl == e, wts, 0.0), axis=1, keepdims=True)
        y_sc[rows(c), :] += jnp.where(w != 0.0, w * y, 0.0)
        return carry

    lax.fori_loop(0, n_chunks, ffn, 0)

    @pl.when(k == EXPERTS_PER_GROUP - 1)
    def _():
        def scatter(c, carry):
            acc_sc[...] += jnp.dot(onehot(c), y_sc[rows(c), :].astype(BF16),
                                   preferred_element_type=F32)
            return carry

        lax.fori_loop(0, n_chunks, scatter, 0)

    @pl.when(e == pl.num_programs(1) - 1)
    def _():
        x2 = _layer_norm(alpha * xf_ref[...] + acc_sc[...], lng_ref[...], lnb_ref[...])
        xo_ref[...] = x2
        xob_ref[...] = x2.astype(BF16)


def _moe(xb, xf, comb, wg, wu, wd, ln_g, ln_b, alpha):
    t, d = xf.shape
    tm = min(MOE_TM, t)
    assert t % tm == 0
    n_e, _, dff = wg.shape
    chunk = min(MOE_CHUNK, tm)
    cap = -(-tm // chunk) * chunk
    tri = jnp.tril(jnp.ones((tm, tm), BF16), -1)
    rows = lambda width: pl.BlockSpec((tm, width), lambda i, e: (i, 0))
    vec = pl.BlockSpec((1, d), lambda i, e: (0, 0))
    return pl.pallas_call(
        functools.partial(_moe_kernel, alpha=alpha, chunk=chunk),
        grid=(t // tm, n_e),
        in_specs=[rows(d), rows(d), rows(LANES),
                  pl.BlockSpec((tm, tm), lambda i, e: (0, 0)),
                  pl.BlockSpec((None, d, dff), lambda i, e: (e, 0, 0)),
                  pl.BlockSpec((None, d, dff), lambda i, e: (e, 0, 0)),
                  pl.BlockSpec((None, dff, d), lambda i, e: (e, 0, 0)),
                  vec, vec],
        out_specs=[rows(d), rows(d)],
        out_shape=[jax.ShapeDtypeStruct((t, d), F32),
                   jax.ShapeDtypeStruct((t, d), BF16)],
        scratch_shapes=[pltpu.VMEM((tm, d), F32),
                        pltpu.VMEM((tm, 1), F32),
                        pltpu.VMEM((1, tm), F32),
                        pltpu.VMEM((1, tm), F32),
                        pltpu.VMEM((1, LANES), F32),
                        pltpu.VMEM((cap, d), BF16),
                        pltpu.VMEM((cap, LANES), F32),
                        pltpu.VMEM((cap, d), F32)],
        compiler_params=_params(("parallel", "arbitrary")),
        name="moe_grouped",
    )(xb, xf, comb, tri, wg, wu, wd, ln_g, ln_b)


def kernel(x, mem, positions, w_in, lambda_q1, lambda_k1, lambda_q2, lambda_k2, diff_subln_g, conv_w, w_mem_kv, w_br_attn, w_br_conv, w_br_mem, w_out, ln1_g, ln1_b, ln2_g, ln2_b, w_router, router_bias, w_exp_gate, w_exp_up, w_exp_down):
    batch, seq, d = x.shape
    depth = w_in.shape[0]
    mem_len = mem.shape[1]
    t = batch * seq
    alpha = (2 * depth) ** 0.25

    cos_t, sin_t = _rope_tables(positions, min(PROJ_TM, t))
    lam_init = [0.8 - 0.6 * math.exp(-0.3 * l) for l in range(depth)]
    lam = _lambdas(lambda_q1, lambda_k1, lambda_q2, lambda_k2, lam_init)
    init_rows = jnp.broadcast_to(jnp.asarray(lam_init, F32).reshape(depth, 1), (depth, LANES))

    wr = jnp.zeros((d, LANES), F32).at[:, :N_EXPERTS].set(w_router.astype(F32))
    rb = jnp.zeros((1, LANES), F32).at[0, :N_EXPERTS].set(router_bias.astype(F32))
    wrh = wr.astype(BF16)
    wrl = (wr - wrh.astype(F32)).astype(BF16)
    mem_b = mem.reshape(batch * mem_len, d).astype(BF16)

    xf = x.reshape(t, d).astype(F32)
    xb = xf.astype(BF16)
    for l in range(depth):
        qt, k, vt, cv, qm, g = _projection(xb, w_in[l].astype(BF16), cos_t, sin_t, batch, seq)
        ya = _attention(qt, k, vt, lam[l:l + 1], init_rows[l:l + 1],
                        diff_subln_g[l].astype(F32).reshape(1, DA_V_DIM), batch, seq)
        kvm = _matmul(mem_b, w_mem_kv[l].astype(BF16), MEM_WIDTH, BF16)
        xf, xb, comb = _merge(ya, cv, qm, kvm, g, xf,
                              w_br_attn[l].astype(BF16), w_br_conv[l].astype(BF16),
                              w_br_mem[l].astype(BF16), w_out[l].astype(BF16),
                              conv_w[l].astype(F32),
                              ln1_g[l].astype(F32).reshape(1, d), ln1_b[l].astype(F32).reshape(1, d),
                              wrh, wrl, rb, seq, mem_len, alpha)
        xf, xb = _moe(xb, xf, comb, w_exp_gate[l].astype(BF16), w_exp_up[l].astype(BF16),
                      w_exp_down[l].astype(BF16),
                      ln2_g[l].astype(F32).reshape(1, d), ln2_b[l].astype(F32).reshape(1, d), alpha)
    return xf.reshape(batch, seq, d).astype(x.dtype)
```

```python
import functools
import math

import jax
import jax.numpy as jnp
from jax import lax
from jax.experimental import pallas as pl
from jax.experimental.pallas import tpu as pltpu

F32 = jnp.float32
BF16 = jnp.bfloat16

DA_HEADS = 8
DA_HEAD_DIM = 64
DA_V_DIM = 2 * DA_HEAD_DIM
CONV_WIDTH = 512
CONV_K = 3
MEM_HEADS = 4
MEM_HEAD_DIM = 128
MEM_WIDTH = MEM_HEADS * MEM_HEAD_DIM
N_BRANCH = 3
ROPE_THETA = 10000.0
N_EXPERTS = 16
N_GROUPS = 4
GROUP_LANE = 16
EXPERTS_PER_GROUP = N_EXPERTS // N_GROUPS
LN_EPS = 1e-5
RMS_EPS = 1e-5

LANES = 128
SUBLANES = 8
VMEM_LIMIT_BYTES = 56 * 1024 * 1024

PROJ_TM = 2048
PROJ_TN = 512
PROJ_RESIDENT_TM = 1024
ATTN_TQ = 1024
MERGE_TM = 512
MOE_TM = 1024
MOE_CHUNK = 288


def _params(semantics):
    return pltpu.CompilerParams(dimension_semantics=semantics,
                                vmem_limit_bytes=VMEM_LIMIT_BYTES)


def _rope_table_kernel(pos_ref, inv_ref, cos_ref, sin_ref):
    ang = pos_ref[...].astype(F32) * inv_ref[...]
    lane = lax.broadcasted_iota(jnp.int32, ang.shape, 1)
    first_half = (lane % DA_HEAD_DIM) < (DA_HEAD_DIM // 2)
    s = jnp.sin(ang)
    cos_ref[...] = jnp.cos(ang)
    sin_ref[...] = jnp.where(first_half, -s, s)


def _rope_tables(positions, tm):
    t = positions.size
    half = DA_HEAD_DIM // 2
    inv_freq = ROPE_THETA ** (-jnp.arange(0, DA_HEAD_DIM, 2, dtype=F32) / DA_HEAD_DIM)
    inv_lanes = jnp.tile(inv_freq, LANES // half).reshape(1, LANES)
    pos = positions.reshape(t, 1)
    return pl.pallas_call(
        _rope_table_kernel,
        grid=(t // tm,),
        in_specs=[pl.BlockSpec((tm, 1), lambda i: (i, 0)),
                  pl.BlockSpec((1, LANES), lambda i: (0, 0))],
        out_specs=[pl.BlockSpec((tm, LANES), lambda i: (i, 0)),
                   pl.BlockSpec((tm, LANES), lambda i: (i, 0))],
        out_shape=[jax.ShapeDtypeStruct((t, LANES), F32),
                   jax.ShapeDtypeStruct((t, LANES), F32)],
        compiler_params=_params(("parallel",)),
        name="rope_tables",
    )(pos, inv_lanes)


def _lambda_kernel(q1_ref, k1_ref, q2_ref, k2_ref, init_ref, lam_ref):
    a = jnp.sum(q1_ref[...] * k1_ref[...], axis=-1, keepdims=True)
    b = jnp.sum(q2_ref[...] * k2_ref[...], axis=-1, keepdims=True)
    lam_ref[...] = jnp.exp(a) - jnp.exp(b) + init_ref[...]


def _lambdas(lq1, lk1, lq2, lk2, lam_init):
    depth = lq1.shape[0]
    init = jnp.broadcast_to(jnp.asarray(lam_init, F32).reshape(depth, 1), (depth, LANES))
    return pl.pallas_call(
        _lambda_kernel,
        out_shape=jax.ShapeDtypeStruct((depth, LANES), F32),
        name="diff_lambda",
    )(lq1.astype(F32), lk1.astype(F32), lq2.astype(F32), lk2.astype(F32), init)


_Q_TILE0, _K_TILE0, _V_TILE0, _CV_TILE0, _QM_TILE, _G_TILE0, _N_TILES = 0, 2, 4, 6, 9, 10, 16
_Q_SCALE = (DA_HEAD_DIM ** -0.5) * math.log2(math.e)


def _rotary(acc, cos_ref, sin_ref):
    tn = acc.shape[1]
    reps = tn // LANES
    c = jnp.concatenate([cos_ref[...]] * reps, axis=1)
    s = jnp.concatenate([sin_ref[...]] * reps, axis=1)
    lane = lax.broadcasted_iota(jnp.int32, acc.shape, 1)
    half = DA_HEAD_DIM // 2
    first_half = (lane % DA_HEAD_DIM) < half
    partner = jnp.where(first_half,
                        pltpu.roll(acc, tn - half, axis=1),
                        pltpu.roll(acc, half, axis=1))
    return acc * c + partner * s


def _proj_segment_kernel(x_ref, w_ref, *refs, epilogue):
    *aux, o_ref = refs
    acc = jnp.dot(x_ref[...], w_ref[...], preferred_element_type=F32)
    o_ref[...] = epilogue(acc, *aux).astype(o_ref.dtype)


def _proj_resident_kernel(x_ref, w_ref, o_ref, *, epilogue, chunk):
    x = x_ref[...]
    for c0 in range(0, w_ref.shape[1], chunk):
        acc = jnp.dot(x, w_ref[:, c0:c0 + chunk], preferred_element_type=F32)
        o_ref[:, c0:c0 + chunk] = epilogue(acc).astype(o_ref.dtype)


def _proj_resident(name, xb, w_seg, epilogue, out_dtype, tm, chunk):
    t, d = xb.shape
    width = w_seg.shape[1]
    assert t % tm == 0 and width % chunk == 0
    return pl.pallas_call(
        functools.partial(_proj_resident_kernel, epilogue=epilogue, chunk=chunk),
        grid=(t // tm,),
        in_specs=[pl.BlockSpec((tm, d), lambda i: (i, 0)),
                  pl.BlockSpec((d, width), lambda i: (0, 0), pipeline_mode=pl.Buffered(1))],
        out_specs=pl.BlockSpec((tm, width), lambda i: (i, 0)),
        out_shape=jax.ShapeDtypeStruct((t, width), out_dtype),
        compiler_params=_params(("parallel",)),
        name=name,
    )(xb, w_seg)


def _projection(xb, w, cos_t, sin_t, batch, seq):
    t, d = xb.shape
    tm, tn = min(PROJ_TM, seq), PROJ_TN
    assert seq % tm == 0 and w.shape == (d, _N_TILES * tn)
    tps = seq // tm
    tables = [cos_t, sin_t]

    def segment(name, tile0, n_tiles, epilogue, aux, out_dtype, transposed, wide=1):
        assert tile0 % wide == 0 and n_tiles % wide == 0
        tw, steps, first = tn * wide, n_tiles // wide, tile0 // wide
        if transposed:
            out_spec = pl.BlockSpec((None, tw, tm), lambda i, j: (i // tps, j, i % tps))
            out_shape = jax.ShapeDtypeStruct((batch, n_tiles * tn, seq), out_dtype)
        else:
            out_spec = pl.BlockSpec((tm, tw), lambda i, j: (i, j))
            out_shape = jax.ShapeDtypeStruct((t, n_tiles * tn), out_dtype)
        return pl.pallas_call(
            functools.partial(_proj_segment_kernel, epilogue=epilogue),
            grid=(t // tm, steps),
            in_specs=[pl.BlockSpec((tm, d), lambda i, j: (i, 0)),
                      pl.BlockSpec((d, tw), lambda i, j: (0, first + j))]
                     + [pl.BlockSpec((tm, LANES), lambda i, j: (i, 0)) for _ in aux],
            out_specs=out_spec,
            out_shape=out_shape,
            compiler_params=_params(("parallel", "arbitrary")),
            name=name,
        )(xb, w, *aux)

    qt = segment("proj_q", _Q_TILE0, 2,
                 lambda acc, c, s: (_rotary(acc, c, s) * _Q_SCALE).T, tables, BF16, True, wide=2)
    k = segment("proj_k", _K_TILE0, 2, _rotary, tables, BF16, False, wide=2)
    vt = segment("proj_v", _V_TILE0, 2, lambda acc: acc.T, [], BF16, True, wide=2)
    cv = segment("proj_conv", _CV_TILE0, 3, lambda acc: acc, [], F32, False, wide=3)
    qm = segment("proj_mem_q", _QM_TILE, 1, lambda acc: acc, [], BF16, False)
    g = _proj_resident("proj_gates", xb, w[:, _G_TILE0 * tn:], jax.nn.sigmoid, BF16,
                       min(PROJ_RESIDENT_TM, seq), tn)
    return qt, k, vt, cv, qm, g


def _matmul_kernel(a_ref, b_ref, o_ref):
    o_ref[...] = jnp.dot(a_ref[...], b_ref[...], preferred_element_type=F32).astype(o_ref.dtype)


def _matmul(a, b, tn, out_dtype):
    m, k = a.shape
    n = b.shape[1]
    return pl.pallas_call(
        _matmul_kernel,
        grid=(n // tn,),
        in_specs=[pl.BlockSpec((m, k), lambda j: (0, 0)),
                  pl.BlockSpec((k, tn), lambda j: (0, j))],
        out_specs=pl.BlockSpec((m, tn), lambda j: (0, j)),
        out_shape=jax.ShapeDtypeStruct((m, n), out_dtype),
        compiler_params=_params(("parallel",)),
        name="mem_kv_projection",
    )(a, b)


def _sublane_all(op, x):
    for shift in (4, 2, 1):
        x = op(x, pltpu.roll(x, shift, axis=0))
    return x


ATTN_MAX_EXCESS = 100.0


def _attn_kernel(lam_ref, init_ref, g_ref, qt_ref, k_ref, vt_ref, o_ref,
                 s0_sc, s1_sc, x0_sc, x1_sc, p0_sc, p1_sc, a0_sc, a1_sc, m_sc, l_sc, acc_sc, e_sc):
    s_sc, x_sc, p_sc, a_sc = (s0_sc, s1_sc), (x0_sc, x1_sc), (p0_sc, p1_sc), (a0_sc, a1_sc)
    qi = pl.program_id(2)
    dv, tq = qt_ref.shape
    tk, r = s0_sc.shape
    assert tq == 2 * tk and r == 2 * tq
    qt = qt_ref[...]
    sub = lax.broadcasted_iota(jnp.int32, qt.shape, 0)
    zero = jnp.zeros_like(qt)
    qq = jnp.concatenate([jnp.where(sub < DA_HEAD_DIM, qt, zero),
                          jnp.where(sub >= DA_HEAD_DIM, qt, zero)], axis=1)
    ones = jnp.ones((2 * SUBLANES, tk), vt_ref.dtype)

    def raw_scores(kb, diag):
        start = pl.multiple_of(kb * tk, tk)
        s = jnp.dot(k_ref[pl.ds(start, tk), :], qq, preferred_element_type=F32)
        if diag is not None:
            key = lax.broadcasted_iota(jnp.int32, s.shape, 0) + diag * tk
            qry = lax.broadcasted_iota(jnp.int32, s.shape, 1) & (tq - 1)
            s = jnp.where(key <= qry, s, -jnp.inf)
        return s.reshape(tk // SUBLANES, SUBLANES, r)

    def pv(kb, slot):
        start = pl.multiple_of(kb * tk, tk)
        lhs = jnp.concatenate([vt_ref[:, pl.ds(start, tk)], ones], axis=0)
        res = jnp.dot(lhs, p_sc[slot][...], preferred_element_type=F32)
        alpha = a_sc[slot][...]
        acc3 = acc_sc[...].reshape(dv // SUBLANES, SUBLANES, r) * alpha[None]
        acc_sc[...] = acc3.reshape(dv, r) + res[:dv]
        l_sc[...] = alpha * l_sc[...] + res[dv:dv + SUBLANES]

    def reset(m0):
        m_sc[...] = m0
        l_sc[...] = jnp.zeros(l_sc.shape, F32)
        acc_sc[...] = jnp.zeros(acc_sc.shape, F32)

    d0, d1 = 2 * qi, 2 * qi + 1

    def scores_exp(kb, slot, diag=None):
        s3 = raw_scores(kb, diag)
        x0_sc[...] = jnp.max(s3, axis=0)
        p_sc[slot][...] = jnp.exp2(s3 - m_sc[...][None]).reshape(tk, r).astype(p_sc[slot].dtype)

    def advance(slot):
        bm = _sublane_all(jnp.maximum, x0_sc[...])
        m_prev = m_sc[...]
        e_sc[...] = jnp.maximum(e_sc[...], bm - m_prev)
        m_new = jnp.maximum(m_prev, bm)
        a_sc[slot][...] = jnp.exp2(m_prev - m_new)
        m_sc[...] = m_new

    start = pl.multiple_of(qi * tq, tq)
    own = k_ref[pl.ds(start, tq), :].astype(F32).T * qt.astype(F32)
    seed = jnp.concatenate([jnp.sum(own[:DA_HEAD_DIM], axis=0, keepdims=True),
                            jnp.sum(own[DA_HEAD_DIM:], axis=0, keepdims=True)], axis=1)
    reset(jnp.broadcast_to(seed, m_sc.shape))
    e_sc[...] = jnp.full(e_sc.shape, -jnp.inf, F32)
    a0_sc[...] = jnp.ones(a0_sc.shape, F32)
    scores_exp(d0, 0, diag=0)

    def fast_pair(j):
        advance(1)
        scores_exp(2 * j, 1)
        pv(jnp.where(j == 0, d0, 2 * j - 1), 0)
        advance(0)
        scores_exp(2 * j + 1, 0)
        pv(2 * j, 1)

    def fast_quad(i, carry):
        fast_pair(2 * i)
        fast_pair(2 * i + 1)
        return carry

    lax.fori_loop(0, qi // 2, fast_quad, 0)

    @pl.when(qi % 2 == 1)
    def _():
        fast_pair(qi - 1)

    cols = (slice(tk, tq), slice(tq + tk, r))

    def d1_scores_exp():
        start = pl.multiple_of(d1 * tk, tk)
        qh = jnp.concatenate([qq[:, c] for c in cols], axis=1)
        s = jnp.dot(k_ref[pl.ds(start, tk), :], qh, preferred_element_type=F32)
        key = lax.broadcasted_iota(jnp.int32, s.shape, 0)
        qry = lax.broadcasted_iota(jnp.int32, s.shape, 1) & (tk - 1)
        s3 = jnp.where(key <= qry, s, -jnp.inf).reshape(tk // SUBLANES, SUBLANES, tq)
        bm = jnp.max(s3, axis=0)
        ninf = jnp.full((SUBLANES, tk), -jnp.inf, F32)
        x0_sc[...] = jnp.concatenate([ninf, bm[:, :tk], ninf, bm[:, tk:]], axis=1)
        mh = jnp.concatenate([m_sc[:, c] for c in cols], axis=1)
        p = jnp.exp2(s3 - mh[None]).reshape(tk, tq).astype(p1_sc.dtype)
        p1_sc[:, cols[0]] = p[:, :tk]
        p1_sc[:, cols[1]] = p[:, tk:]

    def d1_pv():
        start = pl.multiple_of(d1 * tk, tk)
        lhs = jnp.concatenate([vt_ref[:, pl.ds(start, tk)], ones], axis=0)
        ph = jnp.concatenate([p1_sc[:, c] for c in cols], axis=1)
        res = jnp.dot(lhs, ph, preferred_element_type=F32)
        for i, c in enumerate(cols):
            alpha = a1_sc[:, c]
            part = res[:, i * tk:(i + 1) * tk]
            acc3 = acc_sc[:, c].reshape(dv // SUBLANES, SUBLANES, tk) * alpha[None]
            acc_sc[:, c] = acc3.reshape(dv, tk) + part[:dv]
            l_sc[:, c] = alpha * l_sc[:, c] + part[dv:dv + SUBLANES]

    advance(1)
    d1_scores_exp()
    pv(jnp.where(qi == 0, d0, d0 - 1), 0)
    e_sc[...] = jnp.maximum(e_sc[...], _sublane_all(jnp.maximum, x0_sc[...]) - m_sc[...])
    d1_pv()

    def scores(kb, slot, diag=None):
        s3 = raw_scores(kb, diag)
        s_sc[slot][...] = s3.reshape(tk, r)
        x_sc[slot][...] = jnp.max(s3, axis=0)

    def softmax(slot):
        s3 = s_sc[slot][...].reshape(tk // SUBLANES, SUBLANES, r)
        m_prev = m_sc[...]
        m_new = jnp.maximum(m_prev, _sublane_all(jnp.maximum, x_sc[slot][...]))
        p_sc[slot][...] = jnp.exp2(s3 - m_new[None]).reshape(tk, r).astype(p_sc[slot].dtype)
        a_sc[slot][...] = jnp.exp2(m_prev - m_new)
        m_sc[...] = m_new

    def blk(t):
        return jnp.where(t < 2, d0 + t, t - 2)

    @pl.when(jnp.max(e_sc[...]) > ATTN_MAX_EXCESS)
    def _():
        reset(jnp.full(m_sc.shape, -jnp.inf, F32))
        scores(d0, 0, diag=0)
        scores(d1, 1, diag=1)
        softmax(0)

        def pair(j, carry):
            t = 2 * j + 1
            scores(t - 1, 0)
            softmax(1)
            pv(blk(t - 1), 0)
            scores(t, 1)
            softmax(0)
            pv(blk(t), 1)
            return carry

        lax.fori_loop(0, qi, pair, 0)
        n = 2 * qi + 1
        softmax(1)
        pv(blk(n - 1), 0)
        pv(blk(n), 1)

    inv_l = 1.0 / l_sc[...]
    ot = (acc_sc[...].reshape(dv // SUBLANES, SUBLANES, r) * inv_l[None]).reshape(dv, r)
    o = ot[:, :tq] - lam_ref[0:1, 0:1] * ot[:, tq:]
    o = o * lax.rsqrt(jnp.mean(jnp.square(o), axis=0, keepdims=True) + RMS_EPS)
    o = o.T * g_ref[...] * (1.0 - init_ref[...])
    o_ref[...] = o.astype(o_ref.dtype)


def _attention(qt, k, vt, lam, lam_init, subln_g, batch, seq):
    t = k.shape[0]
    tq = min(ATTN_TQ, seq)
    tk = tq // 2
    assert seq % tq == 0 and tq & (tq - 1) == 0
    nq = seq // tq
    h = DA_HEADS
    r = 2 * tq
    row = pl.BlockSpec((1, LANES), lambda b, hh, i: (0, 0))
    return pl.pallas_call(
        _attn_kernel,
        grid=(batch, h, nq),
        in_specs=[row, row, row,
                  pl.BlockSpec((None, DA_V_DIM, tq), lambda b, hh, i: (b, hh, i)),
                  pl.BlockSpec((seq, DA_V_DIM), lambda b, hh, i: (b, hh)),
                  pl.BlockSpec((None, DA_V_DIM, seq), lambda b, hh, i: (b, hh, 0))],
        out_specs=pl.BlockSpec((tq, DA_V_DIM), lambda b, hh, i: (b * nq + i, hh)),
        out_shape=jax.ShapeDtypeStruct((t, h * DA_V_DIM), BF16),
        scratch_shapes=[pltpu.VMEM((tk, r), F32), pltpu.VMEM((tk, r), F32),
                        pltpu.VMEM((SUBLANES, r), F32), pltpu.VMEM((SUBLANES, r), F32),
                        pltpu.VMEM((tk, r), BF16), pltpu.VMEM((tk, r), BF16),
                        pltpu.VMEM((SUBLANES, r), F32), pltpu.VMEM((SUBLANES, r), F32),
                        pltpu.VMEM((SUBLANES, r), F32),
                        pltpu.VMEM((SUBLANES, r), F32),
                        pltpu.VMEM((DA_V_DIM, r), F32),
                        pltpu.VMEM((SUBLANES, r), F32)],
        compiler_params=_params(("parallel", "parallel", "arbitrary")),
        name="diff_attention",
    )(lam, lam_init, subln_g, qt, k, vt)


def _layer_norm(h, g, b):
    mu = jnp.mean(h, axis=-1, keepdims=True)
    d = h - mu
    var = jnp.mean(jnp.square(d), axis=-1, keepdims=True)
    return d * lax.rsqrt(var + LN_EPS) * g + b


def _route(logits):
    lane = lax.broadcasted_iota(jnp.int32, logits.shape, 1)
    valid = lane < N_EXPERTS
    logits = jnp.where(valid, logits, -jnp.inf)
    mx = jnp.max(logits, axis=-1, keepdims=True)
    ex = jnp.exp(logits - mx)
    scores = ex / jnp.sum(ex, axis=-1, keepdims=True)
    group = lane // EXPERTS_PER_GROUP
    neg = jnp.full_like(scores, -1.0)
    best = jnp.max(jnp.where(group == 0, scores, neg), axis=-1, keepdims=True)
    gidx = jnp.zeros(best.shape, jnp.int32)
    for gi in range(1, N_GROUPS):
        cand = jnp.max(jnp.where(group == gi, scores, neg), axis=-1, keepdims=True)
        take = cand > best
        gidx = jnp.where(take, gi, gidx)
        best = jnp.where(take, cand, best)
    vals = jnp.where((group == gidx) & valid, scores, neg)
    big = jnp.full_like(lane, LANES)
    v1 = jnp.max(vals, axis=-1, keepdims=True)
    i1 = jnp.min(jnp.where(vals == v1, lane, big), axis=-1, keepdims=True)
    vals2 = jnp.where(lane == i1, neg, vals)
    v2 = jnp.max(vals2, axis=-1, keepdims=True)
    i2 = jnp.min(jnp.where(vals2 == v2, lane, big), axis=-1, keepdims=True)
    tot = v1 + v2
    comb = jnp.where(lane == i1, v1 / tot, 0.0) + jnp.where(lane == i2, v2 / tot, 0.0)
    return jnp.where(lane == GROUP_LANE, gidx.astype(F32), comb)


def _merge_kernel(ya_ref, cv_ref, cvp_ref, qm_ref, kvm_ref, g_ref, x_ref,
                  wa_ref, wc_ref, wm_ref, wo_ref, cw_ref, lng_ref, lnb_ref, wrh_ref, wrl_ref, rb_ref,
                  xo_ref, xob_ref, comb_ref, *, tiles_per_seq, alpha):
    i = pl.program_id(0)
    tm = ya_ref.shape[0]
    cw = CONV_WIDTH

    z = cv_ref[:, cw:2 * cw] * cv_ref[:, 2 * cw:3 * cw]
    zp = cvp_ref[:, cw:2 * cw] * cvp_ref[:, 2 * cw:3 * cw]
    zp = jnp.where(i % tiles_per_seq == 0, jnp.zeros_like(zp), zp)
    row = lax.broadcasted_iota(jnp.int32, z.shape, 0)
    zm1 = jnp.where(row == 0, zp[SUBLANES - 1:SUBLANES], pltpu.roll(z, 1, axis=0))
    zm2 = jnp.where(row == 0, zp[SUBLANES - 2:SUBLANES - 1],
                    jnp.where(row == 1, zp[SUBLANES - 1:SUBLANES], pltpu.roll(z, 2, axis=0)))
    w = cw_ref[...]
    y = w[0:1] * zm2
    y = y + w[1:2] * zm1
    y = y + w[2:3] * z
    y_conv = (cv_ref[:, 0:cw] * y).astype(BF16)

    outs = []
    for hd in range(MEM_HEADS):
        lo = hd * MEM_HEAD_DIM
        qh = qm_ref[:, lo:lo + MEM_HEAD_DIM]
        kh = kvm_ref[:, lo:lo + MEM_HEAD_DIM]
        vh = kvm_ref[:, MEM_WIDTH + lo:MEM_WIDTH + lo + MEM_HEAD_DIM]
        s = lax.dot_general(qh, kh, (((1,), (1,)), ((), ())),
                            preferred_element_type=F32) * (MEM_HEAD_DIM ** -0.5)
        s = s - jnp.max(s, axis=-1, keepdims=True)
        e = jnp.exp(s)
        p = e / jnp.sum(e, axis=-1, keepdims=True)
        outs.append(jnp.dot(p.astype(BF16), vh, preferred_element_type=F32))
    y_mem = jnp.concatenate(outs, axis=1).astype(BF16)

    d = x_ref.shape[1]
    merged = g_ref[:, 0:d] * jnp.dot(ya_ref[...], wa_ref[...], preferred_element_type=F32)
    merged = merged + g_ref[:, d:2 * d] * jnp.dot(y_conv, wc_ref[...], preferred_element_type=F32)
    merged = merged + g_ref[:, 2 * d:3 * d] * jnp.dot(y_mem, wm_ref[...], preferred_element_type=F32)
    mix = jnp.dot(merged.astype(BF16), wo_ref[...], preferred_element_type=F32)
    x1 = _layer_norm(alpha * x_ref[...] + mix, lng_ref[...], lnb_ref[...])
    xo_ref[...] = x1
    xob_ref[...] = x1.astype(BF16)

    x1h = x1.astype(BF16)
    x1l = (x1 - x1h.astype(F32)).astype(BF16)
    logits = (jnp.dot(x1h, wrh_ref[...], preferred_element_type=F32)
              + (jnp.dot(x1h, wrl_ref[...], preferred_element_type=F32)
                 + jnp.dot(x1l, wrh_ref[...], preferred_element_type=F32))) + rb_ref[...]
    comb_ref[...] = _route(logits)


def _merge(ya, cv, qm, kvm, g, xf, wa, wc, wm, wo, conv_w, ln_g, ln_b, wrh, wrl, rb, seq, mem_len, alpha):
    t, d = xf.shape
    tm = min(MERGE_TM, seq)
    assert seq % tm == 0 and tm % SUBLANES == 0
    assert conv_w.shape == (CONV_K, CONV_WIDTH) and g.shape[1] == N_BRANCH * d
    tiles_per_seq = seq // tm
    halo = tm // SUBLANES

    def full(a):
        return pl.BlockSpec(a.shape, lambda i: (0,) * a.ndim)

    def rows(width):
        return pl.BlockSpec((tm, width), lambda i: (i, 0))

    kern = functools.partial(_merge_kernel, tiles_per_seq=tiles_per_seq, alpha=alpha)
    return pl.pallas_call(
        kern,
        grid=(t // tm,),
        in_specs=[rows(ya.shape[1]),
                  rows(cv.shape[1]),
                  pl.BlockSpec((SUBLANES, cv.shape[1]), lambda i: (jnp.maximum(i * halo - 1, 0), 0)),
                  rows(qm.shape[1]),
                  pl.BlockSpec((mem_len, kvm.shape[1]), lambda i: (i // tiles_per_seq, 0)),
                  rows(g.shape[1]),
                  rows(d),
                  full(wa), full(wc), full(wm), full(wo), full(conv_w),
                  full(ln_g), full(ln_b), full(wrh), full(wrl), full(rb)],
        out_specs=[rows(d), rows(d), rows(LANES)],
        out_shape=[jax.ShapeDtypeStruct((t, d), F32),
                   jax.ShapeDtypeStruct((t, d), BF16),
                   jax.ShapeDtypeStruct((t, LANES), F32)],
        compiler_params=_params(("parallel",)),
        name="merge_ln_router",
    )(ya, cv, cv, qm, kvm, g, xf, wa, wc, wm, wo, conv_w, ln_g, ln_b, wrh, wrl, rb)


def _moe_kernel(xb_ref, xf_ref, comb_ref, tri_ref, wg_ref, wu_ref, wd_ref, lng_ref, lnb_ref,
                xo_ref, xob_ref,
                acc_sc, pos_sc, posrow_sc, grow_sc, cnt_sc, xg_sc, wts_sc, y_sc, *, alpha, chunk):
    e = pl.program_id(1)
    grp = e // EXPERTS_PER_GROUP
    k = e % EXPERTS_PER_GROUP
    tm = xb_ref.shape[0]
    lane = lax.broadcasted_iota(jnp.int32, (tm, LANES), 1)

    @pl.when(e == 0)
    def _():
        acc_sc[...] = jnp.zeros(acc_sc.shape, F32)
        gidx = jnp.sum(jnp.where(lane == GROUP_LANE, comb_ref[...], 0.0), axis=1, keepdims=True)
        member = (lane.astype(F32) == gidx).astype(BF16)
        before = jnp.dot(tri_ref[...], member, preferred_element_type=F32)
        pos_sc[...] = jnp.sum(jnp.where(lane.astype(F32) == gidx, before, 0.0), axis=1, keepdims=True)
        cnt_sc[...] = jnp.sum(member.astype(F32), axis=0, keepdims=True)
        grow = comb_ref[...].T[GROUP_LANE:GROUP_LANE + 1]
        sub = lax.broadcasted_iota(jnp.int32, (LANES, tm), 0).astype(F32)
        posrow_sc[...] = jnp.sum(jnp.where(sub == grow, before.T, 0.0), axis=0, keepdims=True)
        grow_sc[...] = grow

    cnt = jnp.sum(jnp.where(lane[0:1] == grp, cnt_sc[...], 0.0)).astype(jnp.int32)
    n_chunks = (cnt + (chunk - 1)) // chunk

    def onehot(c):
        gidx = jnp.sum(jnp.where(lane == GROUP_LANE, comb_ref[...], 0.0), axis=1, keepdims=True)
        slot = jnp.where(gidx == grp.astype(F32), pos_sc[...], -1.0) - (c * chunk).astype(F32)
        col = lax.broadcasted_iota(jnp.int32, (tm, chunk), 1).astype(F32)
        return (col == slot).astype(BF16)

    def onehot_rows(c):
        slot = (jnp.where(grow_sc[...] == grp.astype(F32), posrow_sc[...], -1.0)
                - (c * chunk).astype(F32))
        row = lax.broadcasted_iota(jnp.int32, (chunk, tm), 0).astype(F32)
        return (row == slot).astype(BF16)

    def rows(c):
        return pl.ds(pl.multiple_of(c * chunk, chunk), chunk)

    @pl.when(k == 0)
    def _():
        comb = jnp.where(lane < N_EXPERTS, comb_ref[...], 0.0)
        comb_hi = comb.astype(BF16)
        comb_lo = (comb - comb_hi.astype(F32)).astype(BF16)

        def gather(c, carry):
            sel = onehot_rows(c)
            xg_sc[rows(c), :] = jnp.dot(sel, xb_ref[...], preferred_element_type=F32).astype(BF16)
            wts_sc[rows(c), :] = (jnp.dot(sel, comb_hi, preferred_element_type=F32)
                                  + jnp.dot(sel, comb_lo, preferred_element_type=F32))
            y_sc[rows(c), :] = jnp.zeros((chunk, y_sc.shape[1]), F32)
            return carry

        lax.fori_loop(0, n_chunks, gather, 0)

    def ffn(c, carry):
        xg = xg_sc[rows(c), :]
        hg = jnp.dot(xg, wg_ref[...], preferred_element_type=F32)
        hu = jnp.dot(xg, wu_ref[...], preferred_element_type=F32)
        h = (hg * jax.nn.sigmoid(hg)) * hu
        y = jnp.dot(h.astype(BF16), wd_ref[...], preferred_element_type=F32)
        wts = wts_sc[rows(c), :]
        wl = lax.broadcasted_iota(jnp.int32, wts.shape, 1)
        w = jnp.sum(jnp.where(wl == e, wts, 0.0), axis=1, keepdims=True)
        y_sc[rows(c), :] += jnp.where(w != 0.0, w * y, 0.0)
        return carry

    lax.fori_loop(0, n_chunks, ffn, 0)

    @pl.when(k == EXPERTS_PER_GROUP - 1)
    def _():
        def scatter(c, carry):
            acc_sc[...] += jnp.dot(onehot(c), y_sc[rows(c), :].astype(BF16),
                                   preferred_element_type=F32)
            return carry

        lax.fori_loop(0, n_chunks, scatter, 0)

    @pl.when(e == pl.num_programs(1) - 1)
    def _():
        x2 = _layer_norm(alpha * xf_ref[...] + acc_sc[...], lng_ref[...], lnb_ref[...])
        xo_ref[...] = x2
        xob_ref[...] = x2.astype(BF16)


def _moe(xb, xf, comb, wg, wu, wd, ln_g, ln_b, alpha):
    t, d = xf.shape
    tm = min(MOE_TM, t)
    assert t % tm == 0
    n_e, _, dff = wg.shape
    chunk = min(MOE_CHUNK, tm)
    cap = -(-tm // chunk) * chunk
    tri = jnp.tril(jnp.ones((tm, tm), BF16), -1)
    rows = lambda width: pl.BlockSpec((tm, width), lambda i, e: (i, 0))
    vec = pl.BlockSpec((1, d), lambda i, e: (0, 0))
    return pl.pallas_call(
        functools.partial(_moe_kernel, alpha=alpha, chunk=chunk),
        grid=(t // tm, n_e),
        in_specs=[rows(d), rows(d), rows(LANES),
                  pl.BlockSpec((tm, tm), lambda i, e: (0, 0)),
                  pl.BlockSpec((None, d, dff), lambda i, e: (e, 0, 0)),
                  pl.BlockSpec((None, d, dff), lambda i, e: (e, 0, 0)),
                  pl.BlockSpec((None, dff, d), lambda i, e: (e, 0, 0)),
                  vec, vec],
        out_specs=[rows(d), rows(d)],
        out_shape=[jax.ShapeDtypeStruct((t, d), F32),
                   jax.ShapeDtypeStruct((t, d), BF16)],
        scratch_shapes=[pltpu.VMEM((tm, d), F32),
                        pltpu.VMEM((tm, 1), F32),
                        pltpu.VMEM((1, tm), F32),
                        pltpu.VMEM((1, tm), F32),
                        pltpu.VMEM((1, LANES), F32),
                        pltpu.VMEM((cap, d), BF16),
                        pltpu.VMEM((cap, LANES), F32),
                        pltpu.VMEM((cap, d), F32)],
        compiler_params=_params(("parallel", "arbitrary")),
        name="moe_grouped",
    )(xb, xf, comb, tri, wg, wu, wd, ln_g, ln_b)


def kernel(x, mem, positions, w_in, lambda_q1, lambda_k1, lambda_q2, lambda_k2, diff_subln_g, conv_w, w_mem_kv, w_br_attn, w_br_conv, w_br_mem, w_out, ln1_g, ln1_b, ln2_g, ln2_b, w_router, router_bias, w_exp_gate, w_exp_up, w_exp_down):
    batch, seq, d = x.shape
    depth = w_in.shape[0]
    mem_len = mem.shape[1]
    t = batch * seq
    alpha = (2 * depth) ** 0.25

    cos_t, sin_t = _rope_tables(positions, min(PROJ_TM, t))
    lam_init = [0.8 - 0.6 * math.exp(-0.3 * l) for l in range(depth)]
    lam = _lambdas(lambda_q1, lambda_k1, lambda_q2, lambda_k2, lam_init)
    init_rows = jnp.broadcast_to(jnp.asarray(lam_init, F32).reshape(depth, 1), (depth, LANES))

    wr = jnp.zeros((d, LANES), F32).at[:, :N_EXPERTS].set(w_router.astype(F32))
    rb = jnp.zeros((1, LANES), F32).at[0, :N_EXPERTS].set(router_bias.astype(F32))
    wrh = wr.astype(BF16)
    wrl = (wr - wrh.astype(F32)).astype(BF16)
    mem_b = mem.reshape(batch * mem_len, d).astype(BF16)

    xf = x.reshape(t, d).astype(F32)
    xb = xf.astype(BF16)
    for l in range(depth):
        qt, k, vt, cv, qm, g = _projection(xb, w_in[l].astype(BF16), cos_t, sin_t, batch, seq)
        ya = _attention(qt, k, vt, lam[l:l + 1], init_rows[l:l + 1],
                        diff_subln_g[l].astype(F32).reshape(1, DA_V_DIM), batch, seq)
        kvm = _matmul(mem_b, w_mem_kv[l].astype(BF16), MEM_WIDTH, BF16)
        xf, xb, comb = _merge(ya, cv, qm, kvm, g, xf,
                              w_br_attn[l].astype(BF16), w_br_conv[l].astype(BF16),
                              w_br_mem[l].astype(BF16), w_out[l].astype(BF16),
                              conv_w[l].astype(F32),
                              ln1_g[l].astype(F32).reshape(1, d), ln1_b[l].astype(F32).reshape(1, d),
                              wrh, wrl, rb, seq, mem_len, alpha)
        xf, xb = _moe(xb, xf, comb, w_exp_gate[l].astype(BF16), w_exp_up[l].astype(BF16),
                      w_exp_down[l].astype(BF16),
                      ln2_g[l].astype(F32).reshape(1, d), ln2_b[l].astype(F32).reshape(1, d), alpha)
    return xf.reshape(batch, seq, d).astype(x.dtype)
```
